```python
import math
import jax, jax.numpy as jnp
from jax import lax
import numpy as np

D_MODEL = 2048
BATCH = 16
SEQ = 2048
DEPTH = 4

D_MIX = D_MODEL
SSD_INNER = D_MIX // 2
SSD_HEAD_DIM = 64
SSD_HEADS = SSD_INNER // SSD_HEAD_DIM
SSD_GROUPS = 2
SSD_HEADS_PER_GROUP = SSD_HEADS // SSD_GROUPS
SSD_STATE = 128
SSD_CONV = 4
SSD_CHUNK = 128
SSD_CONV_DIM = SSD_INNER + 2 * SSD_GROUPS * SSD_STATE
DT_MIN = 0.001
DT_MAX = 0.1
MLA_V_HEAD = 128
MLA_HEADS = (D_MIX - SSD_INNER) // MLA_V_HEAD
MLA_NOPE = 128
MLA_ROPE = 64
MLA_QK_HEAD = MLA_NOPE + MLA_ROPE
Q_LORA = D_MODEL // 4
KV_LORA = D_MODEL // 4
ROPE_THETA = 10000.0
Q_BLOCK = 128
MEM_LEN = 256
X_HEADS = 4
X_HEAD_DIM = 128
X_INNER = X_HEADS * X_HEAD_DIM
FFN_HIDDEN = ((8 * D_MODEL + 3 * 256 - 1) // (3 * 256)) * 256
IN_COLS = SSD_INNER + SSD_CONV_DIM + SSD_HEADS + Q_LORA + KV_LORA + MLA_ROPE
RMS_EPS = 1e-6

kernel_name = 'hymba_ssd_mla_memory_hybrid'


def rms_norm(x, g):
    xf = x.astype(jnp.float32)
    y = xf * lax.rsqrt(jnp.mean(xf * xf, axis=-1, keepdims=True) + RMS_EPS)
    return (y * g.astype(jnp.float32)).astype(x.dtype)


def rope_tables(positions):
    inv_freq = 1.0 / (ROPE_THETA ** (jnp.arange(0, MLA_ROPE, 2, dtype=jnp.float32) / MLA_ROPE))
    ang = positions.astype(jnp.float32)[..., None] * inv_freq
    return jnp.cos(ang), jnp.sin(ang)


def apply_rope(x, cos, sin):
    half = x.shape[-1] // 2
    x1 = x[..., :half].astype(jnp.float32)
    x2 = x[..., half:].astype(jnp.float32)
    c = cos[:, :, None, :]
    s = sin[:, :, None, :]
    return jnp.concatenate([x1 * c - x2 * s, x2 * c + x1 * s], axis=-1).astype(x.dtype)


def causal_depthwise_conv(u, w, b):
    k = w.shape[0]
    y = lax.conv_general_dilated(u, w[:, None, :].astype(u.dtype), window_strides=(1,),
                                 padding=[(k - 1, 0)],
                                 dimension_numbers=('NWC', 'WIO', 'NWC'),
                                 feature_group_count=u.shape[-1])
    return y + b.astype(u.dtype)


def _swap(t):
    return jnp.transpose(t, (0, 3, 4, 1, 2))


def ssd_scan(xdt, a, b_in, c_in):
    bsz, seq = xdt.shape[:2]
    nc = seq // SSD_CHUNK
    xc = xdt.reshape(bsz, nc, SSD_CHUNK, SSD_GROUPS, SSD_HEADS_PER_GROUP, SSD_HEAD_DIM)
    bc = b_in.reshape(bsz, nc, SSD_CHUNK, SSD_GROUPS, SSD_STATE)
    cc = c_in.reshape(bsz, nc, SSD_CHUNK, SSD_GROUPS, SSD_STATE)
    ac = _swap(a.reshape(bsz, nc, SSD_CHUNK, SSD_GROUPS, SSD_HEADS_PER_GROUP))
    a_cs = jnp.cumsum(ac, axis=-1)
    causal = jnp.tril(jnp.ones((SSD_CHUNK, SSD_CHUNK), dtype=bool))
    decay_in = jnp.exp(jnp.where(causal, a_cs[..., :, None] - a_cs[..., None, :], -jnp.inf))
    cb = jnp.einsum('bclgn,bcsgn->bgcls', cc, bc)
    y_diag = jnp.einsum('bgecls,bcsgep->bclgep', cb[:, :, None] * decay_in, xc)
    decay_to_end = jnp.exp(a_cs[..., -1:] - a_cs)
    chunk_states = jnp.einsum('bclgn,bclgep->bcgepn', bc, xc * _swap(decay_to_end)[..., None])
    chunk_states = jnp.concatenate([jnp.zeros_like(chunk_states[:, :1]), chunk_states], axis=1)
    tot = jnp.pad(a_cs[..., -1], ((0, 0), (0, 0), (0, 0), (1, 0)))
    tot_cs = jnp.cumsum(tot, axis=-1)
    causal_c = jnp.tril(jnp.ones((nc + 1, nc + 1), dtype=bool))
    decay_chunk = jnp.exp(jnp.where(causal_c, tot_cs[..., :, None] - tot_cs[..., None, :], -jnp.inf))
    states_in = jnp.einsum('bgezc,bcgepn->bzgepn', decay_chunk, chunk_states)[:, :-1]
    y_off = jnp.einsum('bclgn,bcgepn->bclgep', cc, states_in) * _swap(jnp.exp(a_cs))[..., None]
    return (y_diag + y_off).reshape(bsz, seq, SSD_GROUPS, SSD_HEADS_PER_GROUP, SSD_HEAD_DIM)


def ssd_mixer(z, xbc, dt_raw, conv_w, conv_b, dt_bias, a_log, d_skip, norm_g):
    bsz, seq, _ = z.shape
    gn = SSD_GROUPS * SSD_STATE
    xbc = jax.nn.silu(causal_depthwise_conv(xbc, conv_w, conv_b))
    xs = xbc[..., :SSD_INNER].reshape(bsz, seq, SSD_GROUPS, SSD_HEADS_PER_GROUP, SSD_HEAD_DIM)
    b_in = xbc[..., SSD_INNER:SSD_INNER + gn].reshape(bsz, seq, SSD_GROUPS, SSD_STATE)
    c_in = xbc[..., SSD_INNER + gn:].reshape(bsz, seq, SSD_GROUPS, SSD_STATE)
    dt = jax.nn.softplus(dt_raw.astype(jnp.float32) + dt_bias.astype(jnp.float32))
    dt = dt.reshape(bsz, seq, SSD_GROUPS, SSD_HEADS_PER_GROUP)
    a = dt * (-jnp.exp(a_log.astype(jnp.float32))).reshape(SSD_GROUPS, SSD_HEADS_PER_GROUP)
    y = ssd_scan(xs * dt[..., None], a, b_in, c_in)
    y = y + xs * d_skip.reshape(SSD_GROUPS, SSD_HEADS_PER_GROUP, 1)
    y = (y.reshape(bsz, seq, SSD_INNER) * jax.nn.silu(z)).reshape(bsz, seq, SSD_GROUPS, SSD_INNER // SSD_GROUPS)
    y = rms_norm(y, norm_g.reshape(SSD_GROUPS, SSD_INNER // SSD_GROUPS))
    return y.reshape(bsz, seq, SSD_INNER).astype(z.dtype)


def causal_block_attention(q, k, v, scale):
    seq = q.shape[1]
    outs = []
    for blk in range(seq // Q_BLOCK):
        q0 = blk * Q_BLOCK
        kend = q0 + Q_BLOCK
        s = jnp.einsum('bqhd,bkhd->bhqk', q[:, q0:kend], k[:, :kend]).astype(jnp.float32) * scale
        mask = (q0 + jnp.arange(Q_BLOCK))[:, None] >= jnp.arange(kend)[None, :]
        p = jax.nn.softmax(jnp.where(mask, s, -jnp.inf), axis=-1)
        outs.append(jnp.einsum('bhqk,bkhd->bqhd', p.astype(v.dtype), v[:, :kend]))
    return jnp.concatenate(outs, axis=1)


def mla_mixer(q_a, kv_a, k_rope, cos, sin, q_a_norm_g, w_q_b, kv_a_norm_g, w_kv_b, q_norm_g, k_norm_g):
    bsz, seq, _ = q_a.shape
    q = (rms_norm(q_a, q_a_norm_g) @ w_q_b).reshape(bsz, seq, MLA_HEADS, MLA_QK_HEAD)
    kv = (rms_norm(kv_a, kv_a_norm_g) @ w_kv_b).reshape(bsz, seq, MLA_HEADS, MLA_NOPE + MLA_V_HEAD)
    k_nope = kv[..., :MLA_NOPE]
    v = kv[..., MLA_NOPE:]
    k_pe = jnp.broadcast_to(k_rope[:, :, None, :], (bsz, seq, MLA_HEADS, MLA_ROPE))
    k = jnp.concatenate([k_nope, k_pe], axis=-1)
    q = rms_norm(q, q_norm_g)
    k = rms_norm(k, k_norm_g)
    q = jnp.concatenate([q[..., :MLA_NOPE], apply_rope(q[..., MLA_NOPE:], cos, sin)], axis=-1)
    k = jnp.concatenate([k[..., :MLA_NOPE], apply_rope(k[..., MLA_NOPE:], cos, sin)], axis=-1)
    o = causal_block_attention(q, k, v, MLA_QK_HEAD ** -0.5)
    return o.reshape(bsz, seq, MLA_HEADS * MLA_V_HEAD)


def memory_cross_attention(h, m, w_q, w_k, w_v, q_norm_g, k_norm_g, w_o):
    bsz, seq, _ = h.shape
    mlen = m.shape[1]
    q = rms_norm((h @ w_q).reshape(bsz, seq, X_HEADS, X_HEAD_DIM), q_norm_g)
    k = rms_norm((m @ w_k).reshape(bsz, mlen, X_HEADS, X_HEAD_DIM), k_norm_g)
    v = (m @ w_v).reshape(bsz, mlen, X_HEADS, X_HEAD_DIM)
    s = jnp.einsum('bshd,bmhd->bhsm', q, k).astype(jnp.float32) * (X_HEAD_DIM ** -0.5)
    p = jax.nn.softmax(s, axis=-1)
    o = jnp.einsum('bhsm,bmhd->bshd', p.astype(v.dtype), v)
    return o.reshape(bsz, seq, X_INNER) @ w_o


def swiglu(h, w_gate, w_up, w_down):
    return (jax.nn.silu(h @ w_gate) * (h @ w_up)) @ w_down


def _fwd_setup_inputs(seed: int = 0) -> dict:
    key = jax.random.key(seed)
    ks = jax.random.split(key, 32)
    f32 = jnp.float32
    L = DEPTH

    def nrm(k, shape, fan_in):
        return jax.random.normal(k, shape, f32) * (fan_in ** -0.5)

    def gain(k, shape):
        return 1.0 + 0.02 * jax.random.normal(k, shape, f32)

    x = jax.random.normal(ks[0], (BATCH, SEQ, D_MODEL), f32)
    mem = jax.random.normal(ks[1], (BATCH, MEM_LEN, D_MODEL), f32)
    positions = (jnp.arange(SEQ, dtype=jnp.int32)[None, :]
                 + jax.random.randint(ks[2], (BATCH, 1), 0, 4096, dtype=jnp.int32))
    dt0 = jnp.exp(jax.random.uniform(ks[6], (L, SSD_HEADS), f32, math.log(DT_MIN), math.log(DT_MAX)))
    dt_bias = dt0 + jnp.log(-jnp.expm1(-dt0))
    a_log = jnp.log(jax.random.uniform(ks[7], (L, SSD_HEADS), f32, 1.0, 16.0))
    return {
        'x': x,
        'mem': mem,
        'positions': positions,
        'attn_norm_g': gain(ks[3], (L, D_MODEL)),
        'w_in': nrm(ks[4], (L, D_MODEL, IN_COLS), D_MODEL),
        'conv_w': nrm(ks[5], (L, SSD_CONV, SSD_CONV_DIM), SSD_CONV),
        'conv_b': 0.01 * jax.random.normal(ks[8], (L, SSD_CONV_DIM), f32),
        'dt_bias': dt_bias,
        'a_log': a_log,
        'd_skip': gain(ks[9], (L, SSD_HEADS)),
        'ssd_norm_g': gain(ks[10], (L, SSD_INNER)),
        'q_a_norm_g': gain(ks[11], (L, Q_LORA)),
        'w_q_b': nrm(ks[12], (L, Q_LORA, MLA_HEADS * MLA_QK_HEAD), Q_LORA),
        'kv_a_norm_g': gain(ks[13], (L, KV_LORA)),
        'w_kv_b': nrm(ks[14], (L, KV_LORA, MLA_HEADS * (MLA_NOPE + MLA_V_HEAD)), KV_LORA),
        'mla_q_norm_g': gain(ks[15], (L, MLA_QK_HEAD)),
        'mla_k_norm_g': gain(ks[16], (L, MLA_QK_HEAD)),
        'w_out': nrm(ks[17], (L, D_MIX, D_MODEL), D_MIX),
        'xattn_norm_g': gain(ks[18], (L, D_MODEL)),
        'mem_norm_g': gain(ks[19], (L, D_MODEL)),
        'w_xq': nrm(ks[20], (L, D_MODEL, X_INNER), D_MODEL),
        'w_xk': nrm(ks[21], (L, D_MODEL, X_INNER), D_MODEL),
        'w_xv': nrm(ks[22], (L, D_MODEL, X_INNER), D_MODEL),
        'xq_norm_g': gain(ks[23], (L, X_HEAD_DIM)),
        'xk_norm_g': gain(ks[24], (L, X_HEAD_DIM)),
        'w_xo': nrm(ks[25], (L, X_INNER, D_MODEL), X_INNER),
        'ffn_norm_g': gain(ks[26], (L, D_MODEL)),
        'w_gate': nrm(ks[27], (L, D_MODEL, FFN_HIDDEN), D_MODEL),
        'w_up': nrm(ks[28], (L, D_MODEL, FFN_HIDDEN), D_MODEL),
        'w_down': nrm(ks[29], (L, FFN_HIDDEN, D_MODEL), FFN_HIDDEN),
    }


def _fwd_reference(x, mem, positions, attn_norm_g, w_in, conv_w, conv_b, dt_bias, a_log, d_skip, ssd_norm_g,
              q_a_norm_g, w_q_b, kv_a_norm_g, w_kv_b, mla_q_norm_g, mla_k_norm_g, w_out,
              xattn_norm_g, mem_norm_g, w_xq, w_xk, w_xv, xq_norm_g, xk_norm_g, w_xo,
              ffn_norm_g, w_gate, w_up, w_down):
    cos, sin = rope_tables(positions)
    c0 = SSD_INNER
    c1 = c0 + SSD_CONV_DIM
    c2 = c1 + SSD_HEADS
    c3 = c2 + Q_LORA
    c4 = c3 + KV_LORA
    for l in range(DEPTH):
        h = rms_norm(x, attn_norm_g[l])
        proj = h @ w_in[l]
        y_ssd = ssd_mixer(proj[..., :c0], proj[..., c0:c1], proj[..., c1:c2],
                          conv_w[l], conv_b[l], dt_bias[l], a_log[l], d_skip[l], ssd_norm_g[l])
        y_mla = mla_mixer(proj[..., c2:c3], proj[..., c3:c4], proj[..., c4:], cos, sin,
                          q_a_norm_g[l], w_q_b[l], kv_a_norm_g[l], w_kv_b[l],
                          mla_q_norm_g[l], mla_k_norm_g[l])
        mixed = jnp.concatenate([y_ssd, y_mla.astype(y_ssd.dtype)], axis=-1) @ w_out[l]
        x = x + mixed.astype(x.dtype)
        h = rms_norm(x, xattn_norm_g[l])
        m = rms_norm(mem, mem_norm_g[l])
        x = x + memory_cross_attention(h, m, w_xq[l], w_xk[l], w_xv[l],
                                       xq_norm_g[l], xk_norm_g[l], w_xo[l]).astype(x.dtype)
        h = rms_norm(x, ffn_norm_g[l])
        x = x + swiglu(h, w_gate[l], w_up[l], w_down[l]).astype(x.dtype)
    return x


import jax as _jax
import jax.numpy as _jnp

TWIN_FORMAT = 'train_step'
FWD_PARAMS = ['x', 'mem', 'positions', 'attn_norm_g', 'w_in', 'conv_w', 'conv_b', 'dt_bias', 'a_log', 'd_skip', 'ssd_norm_g', 'q_a_norm_g', 'w_q_b', 'kv_a_norm_g', 'w_kv_b', 'mla_q_norm_g', 'mla_k_norm_g', 'w_out', 'xattn_norm_g', 'mem_norm_g', 'w_xq', 'w_xk', 'w_xv', 'xq_norm_g', 'xk_norm_g', 'w_xo', 'ffn_norm_g', 'w_gate', 'w_up', 'w_down']
TWIN_WEIGHTS = ['attn_norm_g', 'w_in', 'conv_w', 'conv_b', 'dt_bias', 'a_log', 'd_skip', 'ssd_norm_g', 'q_a_norm_g', 'w_q_b', 'kv_a_norm_g', 'w_kv_b', 'mla_q_norm_g', 'mla_k_norm_g', 'w_out', 'xattn_norm_g', 'mem_norm_g', 'w_xq', 'w_xk', 'w_xv', 'xq_norm_g', 'xk_norm_g', 'w_xo', 'ffn_norm_g', 'w_gate', 'w_up', 'w_down']
TWIN_DIFF_INPUT = 'x'
TWIN_INPUTS = ['x', 'mem', 'positions', 'attn_norm_g', 'w_in', 'conv_w', 'conv_b', 'dt_bias', 'a_log', 'd_skip', 'ssd_norm_g', 'q_a_norm_g', 'w_q_b', 'kv_a_norm_g', 'w_kv_b', 'mla_q_norm_g', 'mla_k_norm_g', 'w_out', 'xattn_norm_g', 'mem_norm_g', 'w_xq', 'w_xk', 'w_xv', 'xq_norm_g', 'xk_norm_g', 'w_xo', 'ffn_norm_g', 'w_gate', 'w_up', 'w_down', 'loss_target', 'm_attn_norm_g', 'm_w_in', 'm_conv_w', 'm_conv_b', 'm_dt_bias', 'm_a_log', 'm_d_skip', 'm_ssd_norm_g', 'm_q_a_norm_g', 'm_w_q_b', 'm_kv_a_norm_g', 'm_w_kv_b', 'm_mla_q_norm_g', 'm_mla_k_norm_g', 'm_w_out', 'm_xattn_norm_g', 'm_mem_norm_g', 'm_w_xq', 'm_w_xk', 'm_w_xv', 'm_xq_norm_g', 'm_xk_norm_g', 'm_w_xo', 'm_ffn_norm_g', 'm_w_gate', 'm_w_up', 'm_w_down', 'v_attn_norm_g', 'v_w_in', 'v_conv_w', 'v_conv_b', 'v_dt_bias', 'v_a_log', 'v_d_skip', 'v_ssd_norm_g', 'v_q_a_norm_g', 'v_w_q_b', 'v_kv_a_norm_g', 'v_w_kv_b', 'v_mla_q_norm_g', 'v_mla_k_norm_g', 'v_w_out', 'v_xattn_norm_g', 'v_mem_norm_g', 'v_w_xq', 'v_w_xk', 'v_w_xv', 'v_xq_norm_g', 'v_xk_norm_g', 'v_w_xo', 'v_ffn_norm_g', 'v_w_gate', 'v_w_up', 'v_w_down']
TWIN_OUTPUTS = ['loss', 'grad_x', 'grad_attn_norm_g', 'grad_w_in', 'grad_conv_w', 'grad_conv_b', 'grad_dt_bias', 'grad_a_log', 'grad_d_skip', 'grad_ssd_norm_g', 'grad_q_a_norm_g', 'grad_w_q_b', 'grad_kv_a_norm_g', 'grad_w_kv_b', 'grad_mla_q_norm_g', 'grad_mla_k_norm_g', 'grad_w_out', 'grad_xattn_norm_g', 'grad_mem_norm_g', 'grad_w_xq', 'grad_w_xk', 'grad_w_xv', 'grad_xq_norm_g', 'grad_xk_norm_g', 'grad_w_xo', 'grad_ffn_norm_g', 'grad_w_gate', 'grad_w_up', 'grad_w_down', 'delta_attn_norm_g', 'delta_w_in', 'delta_conv_w', 'delta_conv_b', 'delta_dt_bias', 'delta_a_log', 'delta_d_skip', 'delta_ssd_norm_g', 'delta_q_a_norm_g', 'delta_w_q_b', 'delta_kv_a_norm_g', 'delta_w_kv_b', 'delta_mla_q_norm_g', 'delta_mla_k_norm_g', 'delta_w_out', 'delta_xattn_norm_g', 'delta_mem_norm_g', 'delta_w_xq', 'delta_w_xk', 'delta_w_xv', 'delta_xq_norm_g', 'delta_xk_norm_g', 'delta_w_xo', 'delta_ffn_norm_g', 'delta_w_gate', 'delta_w_up', 'delta_w_down', 'new_m_attn_norm_g', 'new_m_w_in', 'new_m_conv_w', 'new_m_conv_b', 'new_m_dt_bias', 'new_m_a_log', 'new_m_d_skip', 'new_m_ssd_norm_g', 'new_m_q_a_norm_g', 'new_m_w_q_b', 'new_m_kv_a_norm_g', 'new_m_w_kv_b', 'new_m_mla_q_norm_g', 'new_m_mla_k_norm_g', 'new_m_w_out', 'new_m_xattn_norm_g', 'new_m_mem_norm_g', 'new_m_w_xq', 'new_m_w_xk', 'new_m_w_xv', 'new_m_xq_norm_g', 'new_m_xk_norm_g', 'new_m_w_xo', 'new_m_ffn_norm_g', 'new_m_w_gate', 'new_m_w_up', 'new_m_w_down', 'new_v_attn_norm_g', 'new_v_w_in', 'new_v_conv_w', 'new_v_conv_b', 'new_v_dt_bias', 'new_v_a_log', 'new_v_d_skip', 'new_v_ssd_norm_g', 'new_v_q_a_norm_g', 'new_v_w_q_b', 'new_v_kv_a_norm_g', 'new_v_w_kv_b', 'new_v_mla_q_norm_g', 'new_v_mla_k_norm_g', 'new_v_w_out', 'new_v_xattn_norm_g', 'new_v_mem_norm_g', 'new_v_w_xq', 'new_v_w_xk', 'new_v_w_xv', 'new_v_xq_norm_g', 'new_v_xk_norm_g', 'new_v_w_xo', 'new_v_ffn_norm_g', 'new_v_w_gate', 'new_v_w_up', 'new_v_w_down']
TWIN_LEAF_KINDS = {'loss': 'loss', 'grad_x': 'grad_x', 'grad_attn_norm_g': 'grad_w', 'grad_w_in': 'grad_w', 'grad_conv_w': 'grad_w', 'grad_conv_b': 'grad_w', 'grad_dt_bias': 'grad_w', 'grad_a_log': 'grad_w', 'grad_d_skip': 'grad_w', 'grad_ssd_norm_g': 'grad_w', 'grad_q_a_norm_g': 'grad_w', 'grad_w_q_b': 'grad_w', 'grad_kv_a_norm_g': 'grad_w', 'grad_w_kv_b': 'grad_w', 'grad_mla_q_norm_g': 'grad_w', 'grad_mla_k_norm_g': 'grad_w', 'grad_w_out': 'grad_w', 'grad_xattn_norm_g': 'grad_w', 'grad_mem_norm_g': 'grad_w', 'grad_w_xq': 'grad_w', 'grad_w_xk': 'grad_w', 'grad_w_xv': 'grad_w', 'grad_xq_norm_g': 'grad_w', 'grad_xk_norm_g': 'grad_w', 'grad_w_xo': 'grad_w', 'grad_ffn_norm_g': 'grad_w', 'grad_w_gate': 'grad_w', 'grad_w_up': 'grad_w', 'grad_w_down': 'grad_w', 'delta_attn_norm_g': 'delta_w', 'delta_w_in': 'delta_w', 'delta_conv_w': 'delta_w', 'delta_conv_b': 'delta_w', 'delta_dt_bias': 'delta_w', 'delta_a_log': 'delta_w', 'delta_d_skip': 'delta_w', 'delta_ssd_norm_g': 'delta_w', 'delta_q_a_norm_g': 'delta_w', 'delta_w_q_b': 'delta_w', 'delta_kv_a_norm_g': 'delta_w', 'delta_w_kv_b': 'delta_w', 'delta_mla_q_norm_g': 'delta_w', 'delta_mla_k_norm_g': 'delta_w', 'delta_w_out': 'delta_w', 'delta_xattn_norm_g': 'delta_w', 'delta_mem_norm_g': 'delta_w', 'delta_w_xq': 'delta_w', 'delta_w_xk': 'delta_w', 'delta_w_xv': 'delta_w', 'delta_xq_norm_g': 'delta_w', 'delta_xk_norm_g': 'delta_w', 'delta_w_xo': 'delta_w', 'delta_ffn_norm_g': 'delta_w', 'delta_w_gate': 'delta_w', 'delta_w_up': 'delta_w', 'delta_w_down': 'delta_w', 'new_m_attn_norm_g': 'new_m', 'new_m_w_in': 'new_m', 'new_m_conv_w': 'new_m', 'new_m_conv_b': 'new_m', 'new_m_dt_bias': 'new_m', 'new_m_a_log': 'new_m', 'new_m_d_skip': 'new_m', 'new_m_ssd_norm_g': 'new_m', 'new_m_q_a_norm_g': 'new_m', 'new_m_w_q_b': 'new_m', 'new_m_kv_a_norm_g': 'new_m', 'new_m_w_kv_b': 'new_m', 'new_m_mla_q_norm_g': 'new_m', 'new_m_mla_k_norm_g': 'new_m', 'new_m_w_out': 'new_m', 'new_m_xattn_norm_g': 'new_m', 'new_m_mem_norm_g': 'new_m', 'new_m_w_xq': 'new_m', 'new_m_w_xk': 'new_m', 'new_m_w_xv': 'new_m', 'new_m_xq_norm_g': 'new_m', 'new_m_xk_norm_g': 'new_m', 'new_m_w_xo': 'new_m', 'new_m_ffn_norm_g': 'new_m', 'new_m_w_gate': 'new_m', 'new_m_w_up': 'new_m', 'new_m_w_down': 'new_m', 'new_v_attn_norm_g': 'new_v', 'new_v_w_in': 'new_v', 'new_v_conv_w': 'new_v', 'new_v_conv_b': 'new_v', 'new_v_dt_bias': 'new_v', 'new_v_a_log': 'new_v', 'new_v_d_skip': 'new_v', 'new_v_ssd_norm_g': 'new_v', 'new_v_q_a_norm_g': 'new_v', 'new_v_w_q_b': 'new_v', 'new_v_kv_a_norm_g': 'new_v', 'new_v_w_kv_b': 'new_v', 'new_v_mla_q_norm_g': 'new_v', 'new_v_mla_k_norm_g': 'new_v', 'new_v_w_out': 'new_v', 'new_v_xattn_norm_g': 'new_v', 'new_v_mem_norm_g': 'new_v', 'new_v_w_xq': 'new_v', 'new_v_w_xk': 'new_v', 'new_v_w_xv': 'new_v', 'new_v_xq_norm_g': 'new_v', 'new_v_xk_norm_g': 'new_v', 'new_v_w_xo': 'new_v', 'new_v_ffn_norm_g': 'new_v', 'new_v_w_gate': 'new_v', 'new_v_w_up': 'new_v', 'new_v_w_down': 'new_v'}


def _forward(args):
    return _fwd_reference(*[args[k] for k in FWD_PARAMS])


def _output_shape():
    out = _jax.eval_shape(lambda: _forward(_fwd_setup_inputs(0)))
    return out.shape, out.dtype

N_MICROBATCH = 1
ADAM_LR = 0.001
ADAM_B1 = 0.9
ADAM_B2 = 0.999
ADAM_EPS = 1e-08
ADAM_WD = 0.01
ADAM_STEP = 10
PER_EXAMPLE_BATCH_AXIS = {'x': 0, 'mem': 0, 'positions': 0, 'loss_target': 0}
SHARED_INPUTS = []
_WEIGHT_DTYPES = {'attn_norm_g': _jnp.float32, 'w_in': _jnp.float32, 'conv_w': _jnp.float32, 'conv_b': _jnp.float32, 'dt_bias': _jnp.float32, 'a_log': _jnp.float32, 'd_skip': _jnp.float32, 'ssd_norm_g': _jnp.float32, 'q_a_norm_g': _jnp.float32, 'w_q_b': _jnp.float32, 'kv_a_norm_g': _jnp.float32, 'w_kv_b': _jnp.float32, 'mla_q_norm_g': _jnp.float32, 'mla_k_norm_g': _jnp.float32, 'w_out': _jnp.float32, 'xattn_norm_g': _jnp.float32, 'mem_norm_g': _jnp.float32, 'w_xq': _jnp.float32, 'w_xk': _jnp.float32, 'w_xv': _jnp.float32, 'xq_norm_g': _jnp.float32, 'xk_norm_g': _jnp.float32, 'w_xo': _jnp.float32, 'ffn_norm_g': _jnp.float32, 'w_gate': _jnp.float32, 'w_up': _jnp.float32, 'w_down': _jnp.float32}
MOMENT_SCALE = {'attn_norm_g': 6.388364e-01, 'w_in': 4.519483e-01, 'conv_w': 7.810192e-01, 'conv_b': 2.696942e+00, 'dt_bias': 1.211045e+00, 'a_log': 6.321994e+00, 'd_skip': 6.728806e+00, 'ssd_norm_g': 2.355143e+01, 'q_a_norm_g': 1.182495e-01, 'w_q_b': 6.749062e-02, 'kv_a_norm_g': 9.394532e-01, 'w_kv_b': 3.617512e-01, 'mla_q_norm_g': 4.219046e-01, 'mla_k_norm_g': 4.212039e-01, 'w_out': 9.894556e-01, 'xattn_norm_g': 5.071596e-02, 'mem_norm_g': 2.823658e-01, 'w_xq': 1.049911e-01, 'w_xk': 1.044633e-01, 'w_xv': 4.820836e-01, 'xq_norm_g': 2.194293e+00, 'xk_norm_g': 2.192003e+00, 'w_xo': 2.278421e-01, 'ffn_norm_g': 1.225779e+01, 'w_gate': 2.009276e-01, 'w_up': 1.722975e-01, 'w_down': 2.763810e-01}


def _to_microbatches(a, axis):
    t = _jnp.moveaxis(a, axis, 0)
    t = t.reshape((N_MICROBATCH, t.shape[0] // N_MICROBATCH) + t.shape[1:])
    return _jnp.moveaxis(t, 1, axis + 1)


def setup_inputs(seed: int = 0) -> dict:
    inp = _fwd_setup_inputs(seed)
    key = _jax.random.fold_in(_jax.random.key(seed), 7919)
    shape, _ = _output_shape()
    out = dict(inp)
    out["loss_target"] = _jax.random.normal(_jax.random.fold_in(key, 0), shape, _jnp.float32)
    for i, name in enumerate(TWIN_WEIGHTS):
        w = inp[name].astype(_jnp.float32)
        if MOMENT_SCALE is None:
            s = _jnp.sqrt(_jnp.mean(_jnp.square(w)) + 1e-30)
        else:
            s = MOMENT_SCALE[name]
        km, kv = _jax.random.split(_jax.random.fold_in(key, i + 1))
        out[name] = w
        out["m_" + name] = s * _jax.random.normal(km, w.shape, _jnp.float32)
        out["v_" + name] = (s * s) * _jax.random.uniform(kv, w.shape, _jnp.float32, 0.5, 1.5)
    if N_MICROBATCH > 1:
        for name, axis in PER_EXAMPLE_BATCH_AXIS.items():
            out[name] = _to_microbatches(out[name], axis)
    return {'x': out['x'], 'mem': out['mem'], 'positions': out['positions'], 'attn_norm_g': out['attn_norm_g'], 'w_in': out['w_in'], 'conv_w': out['conv_w'], 'conv_b': out['conv_b'], 'dt_bias': out['dt_bias'], 'a_log': out['a_log'], 'd_skip': out['d_skip'], 'ssd_norm_g': out['ssd_norm_g'], 'q_a_norm_g': out['q_a_norm_g'], 'w_q_b': out['w_q_b'], 'kv_a_norm_g': out['kv_a_norm_g'], 'w_kv_b': out['w_kv_b'], 'mla_q_norm_g': out['mla_q_norm_g'], 'mla_k_norm_g': out['mla_k_norm_g'], 'w_out': out['w_out'], 'xattn_norm_g': out['xattn_norm_g'], 'mem_norm_g': out['mem_norm_g'], 'w_xq': out['w_xq'], 'w_xk': out['w_xk'], 'w_xv': out['w_xv'], 'xq_norm_g': out['xq_norm_g'], 'xk_norm_g': out['xk_norm_g'], 'w_xo': out['w_xo'], 'ffn_norm_g': out['ffn_norm_g'], 'w_gate': out['w_gate'], 'w_up': out['w_up'], 'w_down': out['w_down'], 'loss_target': out['loss_target'], 'm_attn_norm_g': out['m_attn_norm_g'], 'm_w_in': out['m_w_in'], 'm_conv_w': out['m_conv_w'], 'm_conv_b': out['m_conv_b'], 'm_dt_bias': out['m_dt_bias'], 'm_a_log': out['m_a_log'], 'm_d_skip': out['m_d_skip'], 'm_ssd_norm_g': out['m_ssd_norm_g'], 'm_q_a_norm_g': out['m_q_a_norm_g'], 'm_w_q_b': out['m_w_q_b'], 'm_kv_a_norm_g': out['m_kv_a_norm_g'], 'm_w_kv_b': out['m_w_kv_b'], 'm_mla_q_norm_g': out['m_mla_q_norm_g'], 'm_mla_k_norm_g': out['m_mla_k_norm_g'], 'm_w_out': out['m_w_out'], 'm_xattn_norm_g': out['m_xattn_norm_g'], 'm_mem_norm_g': out['m_mem_norm_g'], 'm_w_xq': out['m_w_xq'], 'm_w_xk': out['m_w_xk'], 'm_w_xv': out['m_w_xv'], 'm_xq_norm_g': out['m_xq_norm_g'], 'm_xk_norm_g': out['m_xk_norm_g'], 'm_w_xo': out['m_w_xo'], 'm_ffn_norm_g': out['m_ffn_norm_g'], 'm_w_gate': out['m_w_gate'], 'm_w_up': out['m_w_up'], 'm_w_down': out['m_w_down'], 'v_attn_norm_g': out['v_attn_norm_g'], 'v_w_in': out['v_w_in'], 'v_conv_w': out['v_conv_w'], 'v_conv_b': out['v_conv_b'], 'v_dt_bias': out['v_dt_bias'], 'v_a_log': out['v_a_log'], 'v_d_skip': out['v_d_skip'], 'v_ssd_norm_g': out['v_ssd_norm_g'], 'v_q_a_norm_g': out['v_q_a_norm_g'], 'v_w_q_b': out['v_w_q_b'], 'v_kv_a_norm_g': out['v_kv_a_norm_g'], 'v_w_kv_b': out['v_w_kv_b'], 'v_mla_q_norm_g': out['v_mla_q_norm_g'], 'v_mla_k_norm_g': out['v_mla_k_norm_g'], 'v_w_out': out['v_w_out'], 'v_xattn_norm_g': out['v_xattn_norm_g'], 'v_mem_norm_g': out['v_mem_norm_g'], 'v_w_xq': out['v_w_xq'], 'v_w_xk': out['v_w_xk'], 'v_w_xv': out['v_w_xv'], 'v_xq_norm_g': out['v_xq_norm_g'], 'v_xk_norm_g': out['v_xk_norm_g'], 'v_w_xo': out['v_w_xo'], 'v_ffn_norm_g': out['v_ffn_norm_g'], 'v_w_gate': out['v_w_gate'], 'v_w_up': out['v_w_up'], 'v_w_down': out['v_w_down']}


def _loss(weights, diff, rest, loss_target):
    with _jax.named_scope("forward"):
        args = {**rest, TWIN_DIFF_INPUT: diff, **{k: w.astype(_WEIGHT_DTYPES[k]) for k, w in weights.items()}}
        y = _forward(args)
    with _jax.named_scope("loss_head"):
        err = _jnp.square(y.astype(_jnp.float32) - loss_target)
        return 0.5 * _jnp.sum(_jnp.mean(err, axis=-1)) if err.ndim else 0.5 * err


def _adamw(w, g, m, v):
    m = ADAM_B1 * m + (1.0 - ADAM_B1) * g
    v = ADAM_B2 * v + (1.0 - ADAM_B2) * _jnp.square(g)
    m_hat = m / (1.0 - ADAM_B1 ** ADAM_STEP)
    v_hat = v / (1.0 - ADAM_B2 ** ADAM_STEP)
    delta = -ADAM_LR * (m_hat / (_jnp.sqrt(v_hat) + ADAM_EPS) + ADAM_WD * w)
    return delta, m, v


def reference(x, mem, positions, attn_norm_g, w_in, conv_w, conv_b, dt_bias, a_log, d_skip, ssd_norm_g, q_a_norm_g, w_q_b, kv_a_norm_g, w_kv_b, mla_q_norm_g, mla_k_norm_g, w_out, xattn_norm_g, mem_norm_g, w_xq, w_xk, w_xv, xq_norm_g, xk_norm_g, w_xo, ffn_norm_g, w_gate, w_up, w_down, loss_target, m_attn_norm_g, m_w_in, m_conv_w, m_conv_b, m_dt_bias, m_a_log, m_d_skip, m_ssd_norm_g, m_q_a_norm_g, m_w_q_b, m_kv_a_norm_g, m_w_kv_b, m_mla_q_norm_g, m_mla_k_norm_g, m_w_out, m_xattn_norm_g, m_mem_norm_g, m_w_xq, m_w_xk, m_w_xv, m_xq_norm_g, m_xk_norm_g, m_w_xo, m_ffn_norm_g, m_w_gate, m_w_up, m_w_down, v_attn_norm_g, v_w_in, v_conv_w, v_conv_b, v_dt_bias, v_a_log, v_d_skip, v_ssd_norm_g, v_q_a_norm_g, v_w_q_b, v_kv_a_norm_g, v_w_kv_b, v_mla_q_norm_g, v_mla_k_norm_g, v_w_out, v_xattn_norm_g, v_mem_norm_g, v_w_xq, v_w_xk, v_w_xv, v_xq_norm_g, v_xk_norm_g, v_w_xo, v_ffn_norm_g, v_w_gate, v_w_up, v_w_down):
    given = dict(x=x, mem=mem, positions=positions, attn_norm_g=attn_norm_g, w_in=w_in, conv_w=conv_w, conv_b=conv_b, dt_bias=dt_bias, a_log=a_log, d_skip=d_skip, ssd_norm_g=ssd_norm_g, q_a_norm_g=q_a_norm_g, w_q_b=w_q_b, kv_a_norm_g=kv_a_norm_g, w_kv_b=w_kv_b, mla_q_norm_g=mla_q_norm_g, mla_k_norm_g=mla_k_norm_g, w_out=w_out, xattn_norm_g=xattn_norm_g, mem_norm_g=mem_norm_g, w_xq=w_xq, w_xk=w_xk, w_xv=w_xv, xq_norm_g=xq_norm_g, xk_norm_g=xk_norm_g, w_xo=w_xo, ffn_norm_g=ffn_norm_g, w_gate=w_gate, w_up=w_up, w_down=w_down, loss_target=loss_target, m_attn_norm_g=m_attn_norm_g, m_w_in=m_w_in, m_conv_w=m_conv_w, m_conv_b=m_conv_b, m_dt_bias=m_dt_bias, m_a_log=m_a_log, m_d_skip=m_d_skip, m_ssd_norm_g=m_ssd_norm_g, m_q_a_norm_g=m_q_a_norm_g, m_w_q_b=m_w_q_b, m_kv_a_norm_g=m_kv_a_norm_g, m_w_kv_b=m_w_kv_b, m_mla_q_norm_g=m_mla_q_norm_g, m_mla_k_norm_g=m_mla_k_norm_g, m_w_out=m_w_out, m_xattn_norm_g=m_xattn_norm_g, m_mem_norm_g=m_mem_norm_g, m_w_xq=m_w_xq, m_w_xk=m_w_xk, m_w_xv=m_w_xv, m_xq_norm_g=m_xq_norm_g, m_xk_norm_g=m_xk_norm_g, m_w_xo=m_w_xo, m_ffn_norm_g=m_ffn_norm_g, m_w_gate=m_w_gate, m_w_up=m_w_up, m_w_down=m_w_down, v_attn_norm_g=v_attn_norm_g, v_w_in=v_w_in, v_conv_w=v_conv_w, v_conv_b=v_conv_b, v_dt_bias=v_dt_bias, v_a_log=v_a_log, v_d_skip=v_d_skip, v_ssd_norm_g=v_ssd_norm_g, v_q_a_norm_g=v_q_a_norm_g, v_w_q_b=v_w_q_b, v_kv_a_norm_g=v_kv_a_norm_g, v_w_kv_b=v_w_kv_b, v_mla_q_norm_g=v_mla_q_norm_g, v_mla_k_norm_g=v_mla_k_norm_g, v_w_out=v_w_out, v_xattn_norm_g=v_xattn_norm_g, v_mem_norm_g=v_mem_norm_g, v_w_xq=v_w_xq, v_w_xk=v_w_xk, v_w_xv=v_w_xv, v_xq_norm_g=v_xq_norm_g, v_xk_norm_g=v_xk_norm_g, v_w_xo=v_w_xo, v_ffn_norm_g=v_ffn_norm_g, v_w_gate=v_w_gate, v_w_up=v_w_up, v_w_down=v_w_down)
    weights = {n: given[n] for n in TWIN_WEIGHTS}
    shared = {n: given[n] for n in SHARED_INPUTS}
    per_example = {n: given[n] for n in ['x', 'mem', 'positions']}
    grad_fn = _jax.value_and_grad(_loss, argnums=(0, 1))

    def one_microbatch(ex, loss_target):
        ex = dict(ex)
        diff = ex.pop(TWIN_DIFF_INPUT)
        return grad_fn(weights, diff, {**shared, **ex}, loss_target)

    if N_MICROBATCH == 1:
        loss, (grad_w, grad_x) = one_microbatch(per_example, given["loss_target"])
    else:
        def body(carry, xs):
            loss_sum, grad_sum = carry
            l_k, (gw_k, gx_k) = one_microbatch(xs[0], xs[1])
            with _jax.named_scope("update"):
                return (loss_sum + l_k, _jax.tree.map(_jnp.add, grad_sum, gw_k)), gx_k

        init = (_jnp.zeros((), _jnp.float32), _jax.tree.map(_jnp.zeros_like, weights))
        (loss, grad_w), grad_x = _jax.lax.scan(body, init, (per_example, given["loss_target"]))
    with _jax.named_scope("update"):
        delta_w, new_m, new_v = {}, {}, {}
        for n in TWIN_WEIGHTS:
            delta_w[n], new_m[n], new_v[n] = _adamw(weights[n], grad_w[n], given["m_" + n], given["v_" + n])
    return (loss, grad_x, *[grad_w[n] for n in TWIN_WEIGHTS], *[delta_w[n] for n in TWIN_WEIGHTS],
            *[new_m[n] for n in TWIN_WEIGHTS], *[new_v[n] for n in TWIN_WEIGHTS])
```

```python
import functools

import jax
import jax.numpy as jnp
from jax import lax
from jax.experimental import pallas as pl
from jax.experimental.pallas import tpu as pltpu

F32 = jnp.float32
BF16 = jnp.bfloat16

D_MODEL = 2048
DEPTH = 4
N_DEV = 8
SSD_INNER = 1024
SSD_HEADS = 16
SSD_PAIRS = 8
SSD_STATE = 128
SSD_CHUNK = 128
CONV_DIM = 1536
MLA_HEADS = 8
MLA_QK = 192
MLA_QK_PAD = 256
MLA_V = 128
Q_LORA = 512
X_HEADS = 4
X_HEAD_DIM = 128
X_INNER = 512
FFN = 5632
IN_COLS = 3664
IN_PAD = 3840
TAIL_COL = 3584
ROPE_THETA = 10000.0
EPS = 1e-6
NEG = -1e30
VMEM_LIMIT = 56 * 1024 * 1024
HI = lax.Precision.HIGHEST

ADAM_LR = 0.001
ADAM_B1 = 0.9
ADAM_B2 = 0.999
ADAM_EPS = 1e-08
ADAM_WD = 0.01
ADAM_STEP = 10

SMALL_ROWS = 392
CONVW_ROWS = 192

FAM_A = (("w_in", 458, 464, True), ("w_out", 256, 256, False), ("w_gate", 704, 704, True),
         ("w_up", 704, 704, True), ("w_down", 704, 704, False))
FAM_B = (("w_q_b", 192, 192, True), ("w_kv_b", 256, 256, True), ("w_xq", 256, 256, False),
         ("w_xk", 256, 256, False), ("w_xv", 256, 256, False), ("w_xo", 256, 256, True))
FAMS = ((FAM_A, 2048), (FAM_B, 512))
SMALL = (
    ("attn_norm_g", 2048), ("conv_b", 1536), ("dt_bias", 16), ("a_log", 16), ("d_skip", 16),
    ("ssd_norm_g", 1024), ("q_a_norm_g", 512), ("kv_a_norm_g", 512), ("mla_q_norm_g", 192),
    ("mla_k_norm_g", 192), ("xattn_norm_g", 2048), ("mem_norm_g", 2048), ("xq_norm_g", 128),
    ("xk_norm_g", 128), ("ffn_norm_g", 2048),
)
WEIGHT_ORDER = ("attn_norm_g", "w_in", "conv_w", "conv_b", "dt_bias", "a_log", "d_skip", "ssd_norm_g",
                "q_a_norm_g", "w_q_b", "kv_a_norm_g", "w_kv_b", "mla_q_norm_g", "mla_k_norm_g", "w_out",
                "xattn_norm_g", "mem_norm_g", "w_xq", "w_xk", "w_xv", "xq_norm_g", "xk_norm_g", "w_xo",
                "ffn_norm_g", "w_gate", "w_up", "w_down")


def _cp(sem=None):
    if sem is None:
        return pltpu.CompilerParams(vmem_limit_bytes=VMEM_LIMIT)
    return pltpu.CompilerParams(vmem_limit_bytes=VMEM_LIMIT, dimension_semantics=sem)


def _tile(n, pref):
    if n <= pref:
        return n
    t = (pref // 128) * 128
    while t > 128 and n % t:
        t -= 128
    return t


def _sigmoid(v):
    return 1.0 / (1.0 + jnp.exp(-v))


def rms_fwd(x, col_blk, width, g, groups, n_valid, out_dtype, name, tr=512):
    rows = x.shape[0]
    tr = min(tr, rows)
    wg = width // groups

    def body(x_ref, g_ref, y_ref):
        for gi in range(groups):
            sl = slice(gi * wg, (gi + 1) * wg)
            xv = x_ref[:, sl].astype(F32)
            r = lax.rsqrt(jnp.sum(xv * xv, axis=-1, keepdims=True) * (1.0 / n_valid) + EPS)
            y_ref[:, sl] = ((xv * r) * g_ref[:, sl]).astype(out_dtype)

    return pl.pallas_call(
        body, name=name, grid=(rows // tr,),
        in_specs=[pl.BlockSpec((tr, width), lambda i: (i, col_blk)), pl.BlockSpec((1, width), lambda i: (0, 0))],
        out_specs=pl.BlockSpec((tr, width), lambda i: (i, 0)),
        out_shape=jax.ShapeDtypeStruct((rows, width), out_dtype), compiler_params=_cp(("parallel",)))(x, g)


def rms_bwd(x, col_blk, width, g, groups, n_valid, dy, add, name, tr=512):
    rows = x.shape[0]
    tr = min(tr, rows)
    wg = width // groups
    has_add = add is not None

    def body(*refs):
        if has_add:
            x_ref, g_ref, dy_ref, add_ref, dx_ref, dg_ref = refs
        else:
            x_ref, g_ref, dy_ref, dx_ref, dg_ref = refs

        @pl.when(pl.program_id(0) == 0)
        def _():
            dg_ref[...] = jnp.zeros(dg_ref.shape, F32)

        for gi in range(groups):
            sl = slice(gi * wg, (gi + 1) * wg)
            xv = x_ref[:, sl].astype(F32)
            dyv = dy_ref[:, sl].astype(F32)
            r = lax.rsqrt(jnp.sum(xv * xv, axis=-1, keepdims=True) * (1.0 / n_valid) + EPS)
            xh = xv * r
            dyg = dyv * g_ref[:, sl]
            c = jnp.sum(dyg * xh, axis=-1, keepdims=True) * (1.0 / n_valid)
            dx = r * (dyg - xh * c)
            if has_add:
                dx = dx + add_ref[:, sl]
            dx_ref[:, sl] = dx
            dg_ref[:, sl] += jnp.sum(dyv * xh, axis=0, keepdims=True)

    in_specs = [pl.BlockSpec((tr, width), lambda i: (i, col_blk)), pl.BlockSpec((1, width), lambda i: (0, 0)),
                pl.BlockSpec((tr, width), lambda i: (i, 0))]
    args = [x, g, dy]
    if has_add:
        in_specs.append(pl.BlockSpec((tr, width), lambda i: (i, 0)))
        args.append(add)
    return pl.pallas_call(
        body, name=name, grid=(rows // tr,), in_specs=in_specs,
        out_specs=[pl.BlockSpec((tr, width), lambda i: (i, 0)), pl.BlockSpec((1, width), lambda i: (0, 0))],
        out_shape=[jax.ShapeDtypeStruct((rows, width), F32), jax.ShapeDtypeStruct((1, width), F32)],
        compiler_params=_cp(("arbitrary",)))(*args)


def mm(pairs, mode, *, extras=(), epilogue=None, out_dtypes=(F32,), separate=False, tm=1024, tn=512, name):
    a0, b0 = pairs[0]
    if mode == "nn":
        m_dim, n_dim = a0.shape[0], b0.shape[1]
        dn = (((1,), (0,)), ((), ()))
    elif mode == "nt":
        m_dim, n_dim = a0.shape[0], b0.shape[0]
        dn = (((1,), (1,)), ((), ()))
    else:
        m_dim, n_dim = a0.shape[1], b0.shape[1]
        dn = (((0,), (0,)), ((), ()))
    tm = _tile(m_dim, tm)
    tn = _tile(n_dim, tn)
    n_pairs = len(pairs)
    n_extra = len(extras)

    def body(*refs):
        ins = refs[:2 * n_pairs]
        ex = refs[2 * n_pairs:2 * n_pairs + n_extra]
        outs = refs[2 * n_pairs + n_extra:]
        accs = []
        for k in range(n_pairs):
            a = ins[2 * k][...].astype(BF16)
            b = ins[2 * k + 1][...].astype(BF16)
            accs.append(lax.dot_general(a, b, dn, preferred_element_type=F32))
        if not separate:
            total = accs[0]
            for extra_acc in accs[1:]:
                total = total + extra_acc
            accs = [total]
        res = epilogue(*accs, *[e[...] for e in ex]) if epilogue is not None else tuple(accs)
        for o_ref, r in zip(outs, res):
            o_ref[...] = r.astype(o_ref.dtype)

    in_specs, args = [], []
    for a, b in pairs:
        if mode == "nn":
            in_specs += [pl.BlockSpec((tm, a.shape[1]), lambda i, j: (i, 0)),
                         pl.BlockSpec((b.shape[0], tn), lambda i, j: (0, j))]
        elif mode == "nt":
            in_specs += [pl.BlockSpec((tm, a.shape[1]), lambda i, j: (i, 0)),
                         pl.BlockSpec((tn, b.shape[1]), lambda i, j: (j, 0))]
        else:
            in_specs += [pl.BlockSpec((a.shape[0], tm), lambda i, j: (0, i)),
                         pl.BlockSpec((b.shape[0], tn), lambda i, j: (0, j))]
        args += [a, b]
    for e in extras:
        in_specs.append(pl.BlockSpec((tm, tn), lambda i, j: (i, j)))
        args.append(e)
    outs = pl.pallas_call(
        body, name=name, grid=(m_dim // tm, n_dim // tn), in_specs=in_specs,
        out_specs=[pl.BlockSpec((tm, tn), lambda i, j: (i, j)) for _ in out_dtypes],
        out_shape=[jax.ShapeDtypeStruct((m_dim, n_dim), dt) for dt in out_dtypes],
        compiler_params=_cp(("parallel", "parallel")))(*args)
    return outs[0] if len(out_dtypes) == 1 else outs


def _add_res(acc, res):
    return (acc + res,)


def _swiglu_fwd(gate, up):
    gb = gate.astype(BF16).astype(F32)
    ub = up.astype(BF16).astype(F32)
    return gate, up, gb * _sigmoid(gb) * ub


def _swiglu_bwd(dact, gate, up):
    gv = gate.astype(F32)
    uv = up.astype(F32)
    sg = _sigmoid(gv)
    return dact * uv * (sg * (1.0 + gv * (1.0 - sg))), dact * (gv * sg)


def ew(fn, ins, width, out_dtypes, name, tr=512):
    rows = ins[0][0].shape[0]
    tr = min(tr, rows)
    n_in = len(ins)

    def body(*refs):
        res = fn(*[r[...].astype(F32) for r in refs[:n_in]])
        for o_ref, r in zip(refs[n_in:], res):
            o_ref[...] = r.astype(o_ref.dtype)

    in_specs = [pl.BlockSpec((tr, width), functools.partial(lambda i, cb: (i, cb), cb=cb)) for _, cb in ins]
    outs = pl.pallas_call(
        body, name=name, grid=(rows // tr,), in_specs=in_specs,
        out_specs=[pl.BlockSpec((tr, width), lambda i: (i, 0)) for _ in out_dtypes],
        out_shape=[jax.ShapeDtypeStruct((rows, width), dt) for dt in out_dtypes],
        compiler_params=_cp(("parallel",)))(*[a for a, _ in ins])
    return outs[0] if len(out_dtypes) == 1 else outs


def _gate_fwd(y, z):
    return (y * (z * _sigmoid(z)),)


def _gate_bwd(dyg, y, z):
    sg = _sigmoid(z)
    return dyg * (z * sg), dyg * y * (sg * (1.0 + z * (1.0 - sg)))


CONV_CB = 512
CONV_OFF = 2


def _conv_pre(x, w_ref, row):
    acc = x * w_ref[pl.ds(3, 1), :] + w_ref[pl.ds(4, 1), :]
    shifted = []
    for j in (1, 2, 3):
        xs = jnp.where(row >= j, pltpu.roll(x, j, 0), 0.0)
        shifted.append(xs)
        acc = acc + xs * w_ref[pl.ds(3 - j, 1), :]
    return acc, shifted


def conv_fwd(proj3, w8, name):
    bsz, seq, _ = proj3.shape

    def body(x_ref, w_ref, o_ref):
        x = x_ref[0]
        row = lax.broadcasted_iota(jnp.int32, x.shape, 0)
        pre, _ = _conv_pre(x, w_ref, row)
        o_ref[0] = pre * _sigmoid(pre)

    return pl.pallas_call(
        body, name=name, grid=(bsz, CONV_DIM // CONV_CB),
        in_specs=[pl.BlockSpec((1, seq, CONV_CB), lambda b, c: (b, 0, c + CONV_OFF)),
                  pl.BlockSpec((8, CONV_CB), lambda b, c: (0, c))],
        out_specs=pl.BlockSpec((1, seq, CONV_CB), lambda b, c: (b, 0, c)),
        out_shape=jax.ShapeDtypeStruct((bsz, seq, CONV_DIM), F32),
        compiler_params=_cp(("parallel", "parallel")))(proj3, w8)


def conv_bwd(proj3, w8, dout, name):
    bsz, seq, _ = proj3.shape

    def body(x_ref, w_ref, do_ref, dx_ref, dw_ref):
        @pl.when(pl.program_id(1) == 0)
        def _():
            dw_ref[...] = jnp.zeros(dw_ref.shape, F32)

        x = x_ref[0]
        row = lax.broadcasted_iota(jnp.int32, x.shape, 0)
        pre, shifted = _conv_pre(x, w_ref, row)
        sg = _sigmoid(pre)
        dpre = do_ref[0] * (sg * (1.0 + pre * (1.0 - sg)))
        dx = dpre * w_ref[pl.ds(3, 1), :]
        for j in (1, 2, 3):
            fut = jnp.where(row < seq - j, pltpu.roll(dpre, seq - j, 0), 0.0)
            dx = dx + fut * w_ref[pl.ds(3 - j, 1), :]
            dw_ref[pl.ds(3 - j, 1), :] += jnp.sum(dpre * shifted[j - 1], axis=0, keepdims=True)
        dw_ref[pl.ds(3, 1), :] += jnp.sum(dpre * x, axis=0, keepdims=True)
        dw_ref[pl.ds(4, 1), :] += jnp.sum(dpre, axis=0, keepdims=True)
        dx_ref[0] = dx

    return pl.pallas_call(
        body, name=name, grid=(CONV_DIM // CONV_CB, bsz),
        in_specs=[pl.BlockSpec((1, seq, CONV_CB), lambda c, b: (b, 0, c + CONV_OFF)),
                  pl.BlockSpec((8, CONV_CB), lambda c, b: (0, c)),
                  pl.BlockSpec((1, seq, CONV_CB), lambda c, b: (b, 0, c))],
        out_specs=[pl.BlockSpec((1, seq, CONV_CB), lambda c, b: (b, 0, c)),
                   pl.BlockSpec((8, CONV_CB), lambda c, b: (0, c))],
        out_shape=[jax.ShapeDtypeStruct((bsz, seq, CONV_DIM), F32), jax.ShapeDtypeStruct((8, CONV_DIM), F32)],
        compiler_params=_cp(("parallel", "arbitrary")))(proj3, w8, dout)


def _softplus(v):
    t = jnp.exp(-jnp.abs(v))
    small = t * (1.0 - t * (0.5 - t * (1.0 / 3.0)))
    return jnp.maximum(v, 0.0) + jnp.where(t < 0.01, small, jnp.log(1.0 + t))


def _ssd_chunk_prelude(dt_ref, dtb_ref, alog_ref):
    L = SSD_CHUNK
    raw = dt_ref[0] + dtb_ref[...]
    dt = _softplus(raw)
    a_neg = -jnp.exp(alog_ref[...])
    a = dt * a_neg
    r_i = lax.broadcasted_iota(jnp.int32, (L, L), 0)
    c_i = lax.broadcasted_iota(jnp.int32, (L, L), 1)
    tri_low = (r_i >= c_i).astype(F32)
    tri_up = (r_i <= c_i).astype(F32)
    acs = jnp.dot(tri_low, a, precision=HI, preferred_element_type=F32)
    acs_t = lax.dot_general(a, tri_up, (((0,), (0,)), ((), ())), precision=HI,
                            preferred_element_type=F32)
    tot = jnp.sum(a, axis=0, keepdims=True)
    return raw, dt, a_neg, acs, acs_t, tot, r_i, c_i, tri_up


def _col(arr16, e):
    lane = lax.broadcasted_iota(jnp.int32, arr16.shape, 1)
    return jnp.sum(jnp.where(lane == e, arr16, 0.0), axis=1, keepdims=True)


def _rowvec(arr_t, e):
    sub = lax.broadcasted_iota(jnp.int32, arr_t.shape, 0)
    return jnp.sum(jnp.where(sub == e, arr_t, 0.0), axis=0, keepdims=True)


def _pair(lo_mask, v0, v1):
    return jnp.where(lo_mask, v0, v1)


def ssd_fwd(xbc, dt_raw, dtb, alog, dsk, name):
    bsz, seq, _ = xbc.shape
    L = SSD_CHUNK
    nc = seq // L

    def body(x_ref, b_ref, c_ref, dt_ref, dtb_ref, alog_ref, dsk_ref, y_ref, st_ref, h_sc):
        @pl.when(pl.program_id(1) == 0)
        def _():
            h_sc[...] = jnp.zeros(h_sc.shape, F32)

        _, dt, _, acs, acs_t, tot, r_i, c_i, _ = _ssd_chunk_prelude(dt_ref, dtb_ref, alog_ref)
        causal = r_i >= c_i
        lo = lax.broadcasted_iota(jnp.int32, (1, 128), 1) < 64
        dskv = dsk_ref[...]
        for grp in range(2):
            bg = b_ref[0, :, grp * 128:(grp + 1) * 128].astype(BF16)
            cg = c_ref[0, :, grp * 128:(grp + 1) * 128].astype(BF16)
            cb = lax.dot_general(cg, bg, (((1,), (1,)), ((), ())), preferred_element_type=F32)
            for kk in range(4):
                k = grp * 4 + kk
                e0, e1 = 2 * k, 2 * k + 1
                sl = slice(k * 128, (k + 1) * 128)
                xp = x_ref[0, :, sl]
                cols = [_col(acs, e0), _col(acs, e1)]
                dtp = _pair(lo, _col(dt, e0), _col(dt, e1))
                xdt = xp * dtp
                ydiag = jnp.zeros((L, 128), F32)
                for sub, e in enumerate((e0, e1)):
                    diff = cols[sub] - _rowvec(acs_t, e)
                    lam = jnp.exp(jnp.where(causal, diff, NEG))
                    m_e = (cb * lam).astype(BF16)
                    xm = jnp.where(lo if sub == 0 else jnp.logical_not(lo), xdt, 0.0).astype(BF16)
                    ydiag = ydiag + jnp.dot(m_e, xm, preferred_element_type=F32)
                h = h_sc[k]
                st_ref[0, 0, k] = h
                wp = _pair(lo, jnp.exp(cols[0]), jnp.exp(cols[1]))
                yoff = jnp.dot(cg, h.astype(BF16), preferred_element_type=F32) * wp
                tot0, tot1 = _col(tot, e0), _col(tot, e1)
                up = _pair(lo, jnp.exp(tot0 - cols[0]), jnp.exp(tot1 - cols[1]))
                etot = _pair(lo, jnp.exp(tot0), jnp.exp(tot1))
                h_sc[k] = h * etot + lax.dot_general(bg, (xdt * up).astype(BF16), (((0,), (0,)), ((), ())),
                                                     preferred_element_type=F32)
                dskp = _pair(lo, _col(dskv, e0), _col(dskv, e1))
                y_ref[0, :, sl] = ydiag + yoff + xp * dskp

    p16 = pl.BlockSpec((1, 16), lambda b, c: (0, 0))
    return pl.pallas_call(
        body, name=name, grid=(bsz, nc),
        in_specs=[pl.BlockSpec((1, L, 1024), lambda b, c: (b, c, 0)),
                  pl.BlockSpec((1, L, 256), lambda b, c: (b, c, 4)),
                  pl.BlockSpec((1, L, 256), lambda b, c: (b, c, 5)),
                  pl.BlockSpec((1, L, 16), lambda b, c: (b, c, 0)), p16, p16, p16],
        out_specs=[pl.BlockSpec((1, L, 1024), lambda b, c: (b, c, 0)),
                   pl.BlockSpec((1, 1, SSD_PAIRS, 128, 128), lambda b, c: (b, c, 0, 0, 0))],
        out_shape=[jax.ShapeDtypeStruct((bsz, seq, SSD_INNER), F32),
                   jax.ShapeDtypeStruct((bsz, nc, SSD_PAIRS, 128, 128), F32)],
        scratch_shapes=[pltpu.VMEM((SSD_PAIRS, 128, 128), F32)],
        compiler_params=_cp(("parallel", "arbitrary")))(xbc, xbc, xbc, dt_raw, dtb, alog, dsk)


def ssd_bwd(xbc, dt_raw, dtb, alog, dsk, states, dy, name):
    bsz, seq, _ = xbc.shape
    L = SSD_CHUNK
    nc = seq // L

    def body(x_ref, b_ref, c_ref, dt_ref, dtb_ref, alog_ref, dsk_ref, st_ref, dy_ref,
             dxbc_ref, ddt_ref, gdtb_ref, galog_ref, gdsk_ref, dh_sc):
        first = jnp.logical_and(pl.program_id(0) == 0, pl.program_id(1) == 0)

        @pl.when(first)
        def _():
            gdtb_ref[...] = jnp.zeros(gdtb_ref.shape, F32)
            galog_ref[...] = jnp.zeros(galog_ref.shape, F32)
            gdsk_ref[...] = jnp.zeros(gdsk_ref.shape, F32)

        @pl.when(pl.program_id(1) == 0)
        def _():
            dh_sc[...] = jnp.zeros(dh_sc.shape, F32)

        raw, dt, a_neg, acs, acs_t, tot, r_i, c_i, tri_up = _ssd_chunk_prelude(dt_ref, dtb_ref, alog_ref)
        causal = r_i >= c_i
        causal_t = c_i >= r_i
        lo = lax.broadcasted_iota(jnp.int32, (1, 128), 1) < 64
        hi = jnp.logical_not(lo)
        lane16 = lax.broadcasted_iota(jnp.int32, (L, 16), 1)
        lane16r = lax.broadcasted_iota(jnp.int32, (1, 16), 1)
        last_row = lax.broadcasted_iota(jnp.int32, (L, 1), 0) == L - 1
        dskv = dsk_ref[...]
        ds16 = jnp.zeros((L, 16), F32)
        ddt16 = jnp.zeros((L, 16), F32)
        gdsk = jnp.zeros((1, 16), F32)

        def hsum(t, mask):
            return jnp.sum(jnp.where(mask, t, 0.0), axis=1, keepdims=True)

        for grp in range(2):
            gs = slice(grp * 128, (grp + 1) * 128)
            bg = b_ref[0, :, gs].astype(BF16)
            cg = c_ref[0, :, gs].astype(BF16)
            cb = lax.dot_general(cg, bg, (((1,), (1,)), ((), ())), preferred_element_type=F32)
            cbt = lax.dot_general(bg, cg, (((1,), (1,)), ((), ())), preferred_element_type=F32)
            dcb = jnp.zeros((L, L), F32)
            dcbt = jnp.zeros((L, L), F32)
            dc_g = jnp.zeros((L, 128), F32)
            db_g = jnp.zeros((L, 128), F32)
            for kk in range(4):
                k = grp * 4 + kk
                e0, e1 = 2 * k, 2 * k + 1
                sl = slice(k * 128, (k + 1) * 128)
                xp = x_ref[0, :, sl]
                dyp = dy_ref[0, :, sl]
                h = st_ref[0, 0, k]
                dh = dh_sc[k]
                cols = [_col(acs, e0), _col(acs, e1)]
                tots = [_col(tot, e0), _col(tot, e1)]
                dtp = _pair(lo, _col(dt, e0), _col(dt, e1))
                xdt = xp * dtp
                wp = _pair(lo, jnp.exp(cols[0]), jnp.exp(cols[1]))
                up = _pair(lo, jnp.exp(tots[0] - cols[0]), jnp.exp(tots[1] - cols[1]))
                etot = _pair(lo, jnp.exp(tots[0]), jnp.exp(tots[1]))
                hb = h.astype(BF16)
                dhb = dh.astype(BF16)
                yoff = jnp.dot(cg, hb, preferred_element_type=F32) * wp
                dyw = (dyp * wp).astype(BF16)
                dc_g = dc_g + lax.dot_general(dyw, hb, (((1,), (1,)), ((), ())), preferred_element_type=F32)
                dh_in = lax.dot_general(cg, dyw, (((0,), (0,)), ((), ())), preferred_element_type=F32) + dh * etot
                q_mat = jnp.dot(bg, dhb, preferred_element_type=F32)
                db_g = db_g + lax.dot_general((xdt * up).astype(BF16), dhb, (((1,), (1,)), ((), ())),
                                              preferred_element_type=F32)
                dxdt = up * q_mat
                t_u = q_mat * xdt * up
                t_w = dyp * yoff
                t_h = jnp.sum(dh * h, axis=0, keepdims=True) * etot
                for sub, e in enumerate((e0, e1)):
                    msk = lo if sub == 0 else hi
                    rowv = _rowvec(acs_t, e)
                    lam = jnp.exp(jnp.where(causal, cols[sub] - rowv, NEG))
                    lam_t = jnp.exp(jnp.where(causal_t, rowv - cols[sub], NEG))
                    xm = jnp.where(msk, xdt, 0.0).astype(BF16)
                    dym = jnp.where(msk, dyp, 0.0).astype(BF16)
                    g_mat = lax.dot_general(dym, xm, (((1,), (1,)), ((), ())), preferred_element_type=F32)
                    g_t = lax.dot_general(xm, dym, (((1,), (1,)), ((), ())), preferred_element_type=F32)
                    gl = g_mat * lam
                    glt = g_t * lam_t
                    dcb = dcb + gl
                    dcbt = dcbt + glt
                    e_r = jnp.sum(gl * cb, axis=1, keepdims=True)
                    e_c = jnp.sum(glt * cbt, axis=1, keepdims=True)
                    dxdt = dxdt + jnp.dot((cbt * lam_t).astype(BF16), dym, preferred_element_type=F32)
                    u_rows = hsum(t_u, msk)
                    ds_tot = jnp.sum(u_rows, axis=0, keepdims=True) + hsum(t_h, msk)
                    ds_col = e_r - e_c + hsum(t_w, msk) - u_rows + jnp.where(last_row, ds_tot, 0.0)
                    ds16 = ds16 + jnp.where(lane16 == e, ds_col, 0.0)
                dxp = dxdt * xp
                dyx = dyp * xp
                for sub, e in enumerate((e0, e1)):
                    msk = lo if sub == 0 else hi
                    ddt16 = ddt16 + jnp.where(lane16 == e, hsum(dxp, msk), 0.0)
                    gdsk = gdsk + jnp.where(lane16r == e, jnp.sum(hsum(dyx, msk), axis=0, keepdims=True), 0.0)
                dskp = _pair(lo, _col(dskv, e0), _col(dskv, e1))
                dxbc_ref[0, :, sl] = dxdt * dtp + dyp * dskp
                dh_sc[k] = dh_in
            dc_g = dc_g + jnp.dot(dcb.astype(BF16), bg, preferred_element_type=F32)
            db_g = db_g + jnp.dot(dcbt.astype(BF16), cg, preferred_element_type=F32)
            dxbc_ref[0, :, 1024 + grp * 128:1024 + (grp + 1) * 128] = db_g
            dxbc_ref[0, :, 1280 + grp * 128:1280 + (grp + 1) * 128] = dc_g
        da16 = jnp.dot(tri_up, ds16, precision=HI, preferred_element_type=F32)
        ddt16 = ddt16 + da16 * a_neg
        d_aneg = jnp.sum(da16 * dt, axis=0, keepdims=True)
        ddt_raw = ddt16 * _sigmoid(raw)
        ddt_ref[0] = ddt_raw
        gdtb_ref[...] += jnp.sum(ddt_raw, axis=0, keepdims=True)
        galog_ref[...] += d_aneg * a_neg
        gdsk_ref[...] += gdsk

    p16 = pl.BlockSpec((1, 16), lambda b, c: (0, 0))
    rev = lambda b, c: (b, nc - 1 - c, 0)
    return pl.pallas_call(
        body, name=name, grid=(bsz, nc),
        in_specs=[pl.BlockSpec((1, L, 1024), rev),
                  pl.BlockSpec((1, L, 256), lambda b, c: (b, nc - 1 - c, 4)),
                  pl.BlockSpec((1, L, 256), lambda b, c: (b, nc - 1 - c, 5)),
                  pl.BlockSpec((1, L, 16), rev), p16, p16, p16,
                  pl.BlockSpec((1, 1, SSD_PAIRS, 128, 128), lambda b, c: (b, nc - 1 - c, 0, 0, 0)),
                  pl.BlockSpec((1, L, 1024), rev)],
        out_specs=[pl.BlockSpec((1, L, CONV_DIM), rev), pl.BlockSpec((1, L, 16), rev), p16, p16, p16],
        out_shape=[jax.ShapeDtypeStruct((bsz, seq, CONV_DIM), F32), jax.ShapeDtypeStruct((bsz, seq, 16), F32),
                   jax.ShapeDtypeStruct((1, 16), F32), jax.ShapeDtypeStruct((1, 16), F32),
                   jax.ShapeDtypeStruct((1, 16), F32)],
        scratch_shapes=[pltpu.VMEM((SSD_PAIRS, 128, 128), F32)],
        compiler_params=_cp(("arbitrary", "arbitrary")))(xbc, xbc, xbc, dt_raw, dtb, alog, dsk, states, dy)


def rope(x3, cos_t, sin_a, sin_b, transpose, out_dtype, name, ts=512):
    bsz, seq, width = x3.shape
    heads = width // MLA_QK_PAD
    ts = min(ts, seq)

    def body(x_ref, c_ref, sa_ref, sb_ref, o_ref):
        c, sa, sb = c_ref[0], sa_ref[0], sb_ref[0]
        for h in range(heads):
            base = h * MLA_QK_PAD
            o_ref[0, :, base:base + 128] = x_ref[0, :, base:base + 128].astype(out_dtype)
            v = x_ref[0, :, base + 128:base + 256]
            if transpose:
                r = v * c + pltpu.roll(v * sa, 32, 1) + pltpu.roll(v * sb, 96, 1)
            else:
                r = v * c + pltpu.roll(v, 96, 1) * sa + pltpu.roll(v, 32, 1) * sb
            o_ref[0, :, base + 128:base + 256] = r.astype(out_dtype)

    tab = pl.BlockSpec((1, ts, 128), lambda b, i: (b, i, 0))
    return pl.pallas_call(
        body, name=name, grid=(bsz, seq // ts),
        in_specs=[pl.BlockSpec((1, ts, width), lambda b, i: (b, i, 0)), tab, tab, tab],
        out_specs=pl.BlockSpec((1, ts, width), lambda b, i: (b, i, 0)),
        out_shape=jax.ShapeDtypeStruct(x3.shape, out_dtype),
        compiler_params=_cp(("parallel", "parallel")))(x3, cos_t, sin_a, sin_b)


def _scores(q_ref, k_ref, scale, causal, i, j, tq, tk):
    s = lax.dot_general(q_ref[0].astype(BF16), k_ref[0].astype(BF16), (((1,), (1,)), ((), ())),
                        preferred_element_type=F32) * scale
    if not causal:
        return s, None
    r = i * tq + lax.broadcasted_iota(jnp.int32, (tq, tk), 0)
    c = j * tk + lax.broadcasted_iota(jnp.int32, (tq, tk), 1)
    mask = r >= c
    return jnp.where(mask, s, NEG), mask


def attn_fwd(q, k, v, *, heads, dk, dv, v_mul, v_off, causal, scale, tq, tk, name):
    bsz, sq, _ = q.shape
    sk = k.shape[1]
    tq, tk = min(tq, sq), min(tk, sk)
    nq, nk = sq // tq, sk // tk

    def body(q_ref, k_ref, v_ref, o_ref, lse_ref, m_sc, l_sc, acc_sc):
        i, j = pl.program_id(2), pl.program_id(3)

        @pl.when(j == 0)
        def _():
            m_sc[...] = jnp.full(m_sc.shape, NEG, F32)
            l_sc[...] = jnp.zeros(l_sc.shape, F32)
            acc_sc[...] = jnp.zeros(acc_sc.shape, F32)

        def compute():
            s, _ = _scores(q_ref, k_ref, scale, causal, i, j, tq, tk)
            m_prev = m_sc[...]
            m_new = jnp.maximum(m_prev, jnp.max(s, axis=1, keepdims=True))
            p = jnp.exp(s - m_new)
            alpha = jnp.exp(m_prev - m_new)
            l_sc[...] = alpha * l_sc[...] + jnp.sum(p, axis=1, keepdims=True)
            acc_sc[...] = alpha * acc_sc[...] + jnp.dot(p.astype(BF16), v_ref[0].astype(BF16),
                                                        preferred_element_type=F32)
            m_sc[...] = m_new

        if causal:
            pl.when(j <= i)(compute)
        else:
            compute()

        @pl.when(j == nk - 1)
        def _():
            o_ref[0] = acc_sc[...] / l_sc[...]
            lse_ref[0, 0] = m_sc[...] + jnp.log(l_sc[...])

    kvj = (lambda i, j: jnp.minimum(i, j)) if causal else (lambda i, j: j)
    return pl.pallas_call(
        body, name=name, grid=(bsz, heads, nq, nk),
        in_specs=[pl.BlockSpec((1, tq, dk), lambda b, h, i, j: (b, i, h)),
                  pl.BlockSpec((1, tk, dk), lambda b, h, i, j: (b, kvj(i, j), h)),
                  pl.BlockSpec((1, tk, dv), lambda b, h, i, j: (b, kvj(i, j), v_mul * h + v_off))],
        out_specs=[pl.BlockSpec((1, tq, dv), lambda b, h, i, j: (b, i, h)),
                   pl.BlockSpec((1, 1, tq, 1), lambda b, h, i, j: (b, h, i, 0))],
        out_shape=[jax.ShapeDtypeStruct((bsz, sq, heads * dv), F32),
                   jax.ShapeDtypeStruct((bsz, heads, sq, 1), F32)],
        scratch_shapes=[pltpu.VMEM((tq, 1), F32), pltpu.VMEM((tq, 1), F32), pltpu.VMEM((tq, dv), F32)],
        compiler_params=_cp(("parallel", "parallel", "parallel", "arbitrary")))(q, k, v)


def _attn_grads(q_ref, k_ref, v_ref, do_ref, o_ref, lse_ref, scale, causal, i, j, tq, tk):
    s, mask = _scores(q_ref, k_ref, scale, causal, i, j, tq, tk)
    p = jnp.exp(s - lse_ref[0, 0])
    if mask is not None:
        p = jnp.where(mask, p, 0.0)
    do = do_ref[0]
    dp = lax.dot_general(do.astype(BF16), v_ref[0].astype(BF16), (((1,), (1,)), ((), ())),
                         preferred_element_type=F32)
    delta = jnp.sum(do * o_ref[0], axis=1, keepdims=True)
    ds = p * (dp - delta) * scale
    return p, ds


def attn_bwd(q, k, v, do, do_off, o, lse, *, heads, dk, dv, v_mul, v_off, causal, scale, tq, tk, name):
    bsz, sq, _ = q.shape
    sk = k.shape[1]
    tq, tk = min(tq, sq), min(tk, sk)
    nq, nk = sq // tq, sk // tk

    def dq_body(q_ref, k_ref, v_ref, do_ref, o_ref, lse_ref, dq_ref, acc_sc):
        i, j = pl.program_id(2), pl.program_id(3)

        @pl.when(j == 0)
        def _():
            acc_sc[...] = jnp.zeros(acc_sc.shape, F32)

        def compute():
            _, ds = _attn_grads(q_ref, k_ref, v_ref, do_ref, o_ref, lse_ref, scale, causal, i, j, tq, tk)
            acc_sc[...] += jnp.dot(ds.astype(BF16), k_ref[0].astype(BF16), preferred_element_type=F32)

        if causal:
            pl.when(j <= i)(compute)
        else:
            compute()

        @pl.when(j == nk - 1)
        def _():
            dq_ref[0] = acc_sc[...]

    kvj = (lambda i, j: jnp.minimum(i, j)) if causal else (lambda i, j: j)
    dq = pl.pallas_call(
        dq_body, name=name + "_dq", grid=(bsz, heads, nq, nk),
        in_specs=[pl.BlockSpec((1, tq, dk), lambda b, h, i, j: (b, i, h)),
                  pl.BlockSpec((1, tk, dk), lambda b, h, i, j: (b, kvj(i, j), h)),
                  pl.BlockSpec((1, tk, dv), lambda b, h, i, j: (b, kvj(i, j), v_mul * h + v_off)),
                  pl.BlockSpec((1, tq, dv), lambda b, h, i, j: (b, i, h + do_off)),
                  pl.BlockSpec((1, tq, dv), lambda b, h, i, j: (b, i, h)),
                  pl.BlockSpec((1, 1, tq, 1), lambda b, h, i, j: (b, h, i, 0))],
        out_specs=pl.BlockSpec((1, tq, dk), lambda b, h, i, j: (b, i, h)),
        out_shape=jax.ShapeDtypeStruct((bsz, sq, heads * dk), F32),
        scratch_shapes=[pltpu.VMEM((tq, dk), F32)],
        compiler_params=_cp(("parallel", "parallel", "parallel", "arbitrary")))(q, k, v, do, o, lse)

    def dkv_body(q_ref, k_ref, v_ref, do_ref, o_ref, lse_ref, dk_ref, dv_ref, dk_sc, dv_sc):
        j, i = pl.program_id(2), pl.program_id(3)

        @pl.when(i == 0)
        def _():
            dk_sc[...] = jnp.zeros(dk_sc.shape, F32)
            dv_sc[...] = jnp.zeros(dv_sc.shape, F32)

        def compute():
            p, ds = _attn_grads(q_ref, k_ref, v_ref, do_ref, o_ref, lse_ref, scale, causal, i, j, tq, tk)
            dv_sc[...] += lax.dot_general(p.astype(BF16), do_ref[0].astype(BF16), (((0,), (0,)), ((), ())),
                                          preferred_element_type=F32)
            dk_sc[...] += lax.dot_general(ds.astype(BF16), q_ref[0].astype(BF16), (((0,), (0,)), ((), ())),
                                          preferred_element_type=F32)

        if causal:
            pl.when(i >= j)(compute)
        else:
            compute()

        @pl.when(i == nq - 1)
        def _():
            dk_ref[0] = dk_sc[...]
            dv_ref[0] = dv_sc[...]

    qi = (lambda j, i: jnp.maximum(i, j)) if causal else (lambda j, i: i)
    dk_out, dv_out = pl.pallas_call(
        dkv_body, name=name + "_dkv", grid=(bsz, heads, nk, nq),
        in_specs=[pl.BlockSpec((1, tq, dk), lambda b, h, j, i: (b, qi(j, i), h)),
                  pl.BlockSpec((1, tk, dk), lambda b, h, j, i: (b, j, h)),
                  pl.BlockSpec((1, tk, dv), lambda b, h, j, i: (b, j, v_mul * h + v_off)),
                  pl.BlockSpec((1, tq, dv), lambda b, h, j, i: (b, qi(j, i), h + do_off)),
                  pl.BlockSpec((1, tq, dv), lambda b, h, j, i: (b, qi(j, i), h)),
                  pl.BlockSpec((1, 1, tq, 1), lambda b, h, j, i: (b, h, qi(j, i), 0))],
        out_specs=[pl.BlockSpec((1, tk, dk), lambda b, h, j, i: (b, j, h)),
                   pl.BlockSpec((1, tk, dv), lambda b, h, j, i: (b, j, h))],
        out_shape=[jax.ShapeDtypeStruct((bsz, sk, heads * dk), F32),
                   jax.ShapeDtypeStruct((bsz, sk, heads * dv), F32)],
        scratch_shapes=[pltpu.VMEM((tk, dk), F32), pltpu.VMEM((tk, dv), F32)],
        compiler_params=_cp(("parallel", "parallel", "parallel", "arbitrary")))(q, k, v, do, o, lse)
    return dq, dk_out, dv_out


def loss_head(y, target, name, tr=512):
    rows, width = y.shape
    tr = min(tr, rows)

    def body(y_ref, t_ref, dy_ref, l_ref):
        @pl.when(pl.program_id(0) == 0)
        def _():
            l_ref[...] = jnp.zeros(l_ref.shape, F32)

        err = y_ref[...] - t_ref[...]
        dy_ref[...] = err * (1.0 / width)
        l_ref[...] += jnp.sum(jnp.sum(err * err, axis=1, keepdims=True), axis=0, keepdims=True) * (0.5 / width)

    return pl.pallas_call(
        body, name=name, grid=(rows // tr,),
        in_specs=[pl.BlockSpec((tr, width), lambda i: (i, 0)), pl.BlockSpec((tr, width), lambda i: (i, 0))],
        out_specs=[pl.BlockSpec((tr, width), lambda i: (i, 0)), pl.BlockSpec((1, 1), lambda i: (0, 0))],
        out_shape=[jax.ShapeDtypeStruct((rows, width), F32), jax.ShapeDtypeStruct((1, 1), F32)],
        compiler_params=_cp(("arbitrary",)))(y, target)


def _mesh_pos():
    return lax.axis_index("x"), lax.axis_index("y"), lax.axis_index("c")


def all_gather(blocks, name):
    n_arr = len(blocks)

    def body(*refs):
        x_refs, out_refs = refs[:n_arr], refs[n_arr:2 * n_arr]
        send_sems, recv_sems, local_sems = refs[2 * n_arr:]
        x, y, c = _mesh_pos()
        me, sibling = (x, y, c), (x, y, 1 - c)
        chips = [(1 - x, y), (x, 1 - y), (1 - x, 1 - y)]

        def slot(a, px, py, pc):
            return out_refs[a].at[4 * px + 2 * py + pc]

        def copy(a, k, blk, to, src=None):
            return pltpu.make_async_remote_copy(
                src_ref=slot(a, *blk) if src is None else src, dst_ref=slot(a, *blk),
                send_sem=send_sems.at[7 * a + k], recv_sem=recv_sems.at[7 * a + k],
                device_id=to, device_id_type=pl.DeviceIdType.MESH)

        mine = [pltpu.make_async_copy(x_refs[a], slot(a, *me), local_sems.at[a]) for a in range(n_arr)]
        for cp in mine:
            cp.start()
        first = []
        for a in range(n_arr):
            first.append(copy(a, 0, me, sibling, src=x_refs[a]))
            first += [copy(a, 1 + j, me, (*chip, c), src=x_refs[a]) for j, chip in enumerate(chips)]
        for cp in first:
            cp.start()
        passed = []
        for j, chip in enumerate(chips):
            for a in range(n_arr):
                copy(a, 1 + j, (*chip, c), me).wait_recv()
                passed.append(copy(a, 4 + j, (*chip, c), sibling))
                passed[-1].start()
        for a in range(n_arr):
            copy(a, 0, sibling, me).wait_recv()
            for j, chip in enumerate(chips):
                copy(a, 4 + j, (*chip, 1 - c), me).wait_recv()
        for cp in first + passed:
            cp.wait_send()
        for cp in mine:
            cp.wait()

    return pl.pallas_call(
        body, name=name,
        out_shape=[jax.ShapeDtypeStruct((N_DEV,) + b.shape, b.dtype) for b in blocks],
        in_specs=[pl.BlockSpec(memory_space=pl.ANY)] * n_arr,
        out_specs=[pl.BlockSpec(memory_space=pl.ANY)] * n_arr,
        scratch_shapes=[pltpu.SemaphoreType.DMA((7 * n_arr,)), pltpu.SemaphoreType.DMA((7 * n_arr,)),
                        pltpu.SemaphoreType.DMA((n_arr,))],
    )(*blocks)


def exchange(blobs, name):
    n_arr = len(blobs)

    def body(*refs):
        b_refs, r_refs = refs[:n_arr], refs[n_arr:2 * n_arr]
        send_sems, recv_sems, local_sems = refs[2 * n_arr:]
        x, y, c = _mesh_pos()
        me_idx = 4 * x + 2 * y + c

        def peer(k):
            px = (1 - x) if (k >> 2) & 1 else x
            py = (1 - y) if (k >> 1) & 1 else y
            pc = (1 - c) if k & 1 else c
            return (px, py, pc), 4 * px + 2 * py + pc

        def copy(a, k, dst_slot):
            to, peer_idx = peer(k)
            return pltpu.make_async_remote_copy(
                src_ref=b_refs[a].at[peer_idx], dst_ref=r_refs[a].at[dst_slot],
                send_sem=send_sems.at[7 * a + k - 1], recv_sem=recv_sems.at[7 * a + k - 1],
                device_id=to, device_id_type=pl.DeviceIdType.MESH)

        mine = [pltpu.make_async_copy(b_refs[a].at[me_idx], r_refs[a].at[me_idx], local_sems.at[a])
                for a in range(n_arr)]
        for cp in mine:
            cp.start()
        sends = [copy(a, k, me_idx) for a in range(n_arr) for k in range(1, N_DEV)]
        for cp in sends:
            cp.start()
        for a in range(n_arr):
            for k in range(1, N_DEV):
                copy(a, k, peer(k)[1]).wait_recv()
        for cp in sends:
            cp.wait_send()
        for cp in mine:
            cp.wait()

    return pl.pallas_call(
        body, name=name,
        out_shape=[jax.ShapeDtypeStruct(b.shape, b.dtype) for b in blobs],
        in_specs=[pl.BlockSpec(memory_space=pl.ANY)] * n_arr,
        out_specs=[pl.BlockSpec(memory_space=pl.ANY)] * n_arr,
        scratch_shapes=[pltpu.SemaphoreType.DMA((7 * n_arr,)), pltpu.SemaphoreType.DMA((7 * n_arr,)),
                        pltpu.SemaphoreType.DMA((n_arr,))],
    )(*blobs)


def adamw(parts, w, m, v, name, tr=128):
    rows, cols = w.shape
    tr = min(tr, rows)
    n_parts = parts.shape[0]
    c1 = 1.0 - ADAM_B1 ** ADAM_STEP
    c2 = 1.0 - ADAM_B2 ** ADAM_STEP

    def body(p_ref, w_ref, m_ref, v_ref, g_ref, d_ref, nm_ref, nv_ref):
        g = p_ref[0].astype(F32)
        for s in range(1, n_parts):
            g = g + p_ref[s].astype(F32)
        nm = ADAM_B1 * m_ref[...] + (1.0 - ADAM_B1) * g
        nv = ADAM_B2 * v_ref[...] + (1.0 - ADAM_B2) * (g * g)
        m_hat = nm / c1
        v_hat = nv / c2
        g_ref[...] = g
        d_ref[...] = -ADAM_LR * (m_hat / (jnp.sqrt(v_hat) + ADAM_EPS) + ADAM_WD * w_ref[...])
        nm_ref[...] = nm
        nv_ref[...] = nv

    blk = pl.BlockSpec((tr, cols), lambda i: (i, 0))
    return pl.pallas_call(
        body, name=name, grid=(rows // tr,),
        in_specs=[pl.BlockSpec((n_parts, tr, cols), lambda i: (0, i, 0)), blk, blk, blk],
        out_specs=[blk, blk, blk, blk],
        out_shape=[jax.ShapeDtypeStruct((rows, cols), F32)] * 4,
        compiler_params=_cp(("parallel",)))(parts, w, m, v)


def _pack_local(fam, shards, dtype):
    parts = []
    for n, rows, prows, transposed in fam:
        a = shards[n].T if transposed else shards[n]
        parts.append(jnp.pad(a.astype(dtype), ((0, prows - rows), (0, 0))))
    return jnp.concatenate(parts, axis=0)


def _unpack_gathered(fam, gathered):
    out, off = {}, 0
    for n, rows, prows, _ in fam:
        out[n] = gathered[:, off:off + rows].reshape(N_DEV * rows, gathered.shape[2])
        off += prows
    return out


def _pack_full_grads(fam, grads):
    segs = []
    for n, rows, prows, _ in fam:
        g = grads[n].astype(BF16)
        g = g.reshape(N_DEV, rows, g.shape[1])
        segs.append(jnp.pad(g, ((0, 0), (0, prows - rows), (0, 0))))
    return jnp.concatenate(segs, axis=1)


def _unpack_local(fam, blob):
    out, off = {}, 0
    for n, rows, prows, transposed in fam:
        a = blob[off:off + rows]
        out[n] = a.T if transposed else a
        off += prows
    return out


def _pack_small(vals):
    flat = jnp.concatenate([vals[n].reshape(-1).astype(F32) for n, _ in SMALL])
    flat = jnp.pad(flat, (0, SMALL_ROWS * 128 - flat.shape[0]))
    return flat.reshape(SMALL_ROWS, 128)


def _unpack_small(flat2d):
    flat = flat2d.reshape(-1)
    out, off = {}, 0
    for n, k in SMALL:
        out[n] = flat[off:off + DEPTH * k].reshape(DEPTH, k)
        off += DEPTH * k
    return out


def _pad_w_in_t(w_t):
    return jnp.concatenate([w_t[:2560], w_t[2576:IN_COLS], w_t[2560:2576],
                            jnp.zeros((IN_PAD - IN_COLS, w_t.shape[1]), w_t.dtype)], axis=0)


def _unpad_w_in_t(g_t):
    return jnp.concatenate([g_t[:2560], g_t[3648:3664], g_t[2560:3648]], axis=0)


def _pad_head_rows(w_t):
    k = w_t.shape[1]
    v = jnp.pad(w_t.reshape(MLA_HEADS, MLA_QK, k), ((0, 0), (0, MLA_QK_PAD - MLA_QK), (0, 0)))
    return v.reshape(MLA_HEADS * MLA_QK_PAD, k)


def _unpad_head_rows(g_t):
    k = g_t.shape[1]
    return g_t.reshape(MLA_HEADS, MLA_QK_PAD, k)[:, :MLA_QK].reshape(MLA_HEADS * MLA_QK, k)


def _row(v):
    return v.reshape(1, -1).astype(F32)


def layer_fwd(x, memf, w, p, tabs, bsz):
    t_rows = x.shape[0]
    seq = t_rows // bsz
    r = {}
    r["x"] = x
    h1 = rms_fwd(x, 0, D_MODEL, _row(p["attn_norm_g"]), 1, D_MODEL, BF16, "norm1_fwd")
    proj = mm([(h1, w["w_in"])], "nt", tn=768, name="in_proj")
    r["h1"], r["proj"] = h1, proj
    proj3 = proj.reshape(bsz, seq, IN_PAD)

    w8 = jnp.concatenate([p["conv_w_full"], p["conv_b"].reshape(1, -1), jnp.zeros((3, CONV_DIM), F32)], 0)
    xbc = conv_fwd(proj3, w8, "conv_fwd")
    tail = proj[:, TAIL_COL:TAIL_COL + 128]
    dt_raw = tail[:, 64:80].reshape(bsz, seq, 16)
    y, states = ssd_fwd(xbc, dt_raw, _row(p["dt_bias"]), _row(p["a_log"]), _row(p["d_skip"]), "ssd_fwd")
    y2 = y.reshape(t_rows, SSD_INNER)
    yg = ew(_gate_fwd, [(y2, 0), (proj, 0)], SSD_INNER, (F32,), "ssd_gate_fwd")
    y_ssd = rms_fwd(yg, 0, SSD_INNER, _row(p["ssd_norm_g"]), 2, 512, BF16, "ssd_norm_fwd")
    r.update(w8=w8, xbc=xbc, dt_raw=dt_raw, states=states, y=y2, yg=yg)

    qn = rms_fwd(proj, 5, Q_LORA, _row(p["q_a_norm_g"]), 1, Q_LORA, BF16, "qa_norm_fwd")
    kvn = rms_fwd(proj, 6, Q_LORA, _row(p["kv_a_norm_g"]), 1, Q_LORA, BF16, "kva_norm_fwd")
    q_raw = mm([(qn, w["w_q_b"])], "nt", name="q_b_proj")
    kv = mm([(kvn, w["w_kv_b"])], "nt", name="kv_b_proj")
    lane = jnp.arange(128) < 64
    krope = jnp.where(lane[None, :], tail, 0.0)
    k_raw = jnp.concatenate([kv.reshape(t_rows, MLA_HEADS, 256)[:, :, :128],
                             jnp.broadcast_to(krope[:, None, :], (t_rows, MLA_HEADS, 128))], axis=-1)
    k_raw = k_raw.reshape(t_rows, MLA_HEADS * MLA_QK_PAD)
    gq = _row(jnp.tile(jnp.pad(p["mla_q_norm_g"], (0, 64)), MLA_HEADS))
    gk = _row(jnp.tile(jnp.pad(p["mla_k_norm_g"], (0, 64)), MLA_HEADS))
    qh = rms_fwd(q_raw, 0, 2048, gq, MLA_HEADS, MLA_QK, F32, "q_head_norm_fwd")
    kh = rms_fwd(k_raw, 0, 2048, gk, MLA_HEADS, MLA_QK, F32, "k_head_norm_fwd")
    q_fin = rope(qh.reshape(bsz, seq, 2048), *tabs, False, BF16, "rope_fwd")
    k_fin = rope(kh.reshape(bsz, seq, 2048), *tabs, False, BF16, "rope_fwd")
    kv3 = kv.reshape(bsz, seq, 2048)
    o, lse = attn_fwd(q_fin, k_fin, kv3, heads=MLA_HEADS, dk=MLA_QK_PAD, dv=MLA_V, v_mul=2, v_off=1,
                      causal=True, scale=MLA_QK ** -0.5, tq=512, tk=512, name="mla_attn_fwd")
    r.update(qn=qn, kvn=kvn, q_raw=q_raw, k_raw=k_raw, gq=gq, gk=gk, q_fin=q_fin, k_fin=k_fin, kv3=kv3, o=o, lse=lse)
    mixed_in = jnp.concatenate([y_ssd, o.reshape(t_rows, SSD_INNER).astype(BF16)], axis=1)
    x2 = mm([(mixed_in, w["w_out"])], "nn", extras=(x,), epilogue=_add_res, name="out_proj")
    r.update(mixed_in=mixed_in, x2=x2)

    h2 = rms_fwd(x2, 0, D_MODEL, _row(p["xattn_norm_g"]), 1, D_MODEL, BF16, "norm2_fwd")
    mn = rms_fwd(memf, 0, D_MODEL, _row(p["mem_norm_g"]), 1, D_MODEL, BF16, "mem_norm_fwd")
    xq_raw = mm([(h2, w["w_xq"])], "nn", name="xq_proj")
    xk_raw = mm([(mn, w["w_xk"])], "nn", name="xk_proj")
    xv = mm([(mn, w["w_xv"])], "nn", name="xv_proj")
    gxq = _row(jnp.tile(p["xq_norm_g"], X_HEADS))
    gxk = _row(jnp.tile(p["xk_norm_g"], X_HEADS))
    xq = rms_fwd(xq_raw, 0, X_INNER, gxq, X_HEADS, X_HEAD_DIM, BF16, "xq_norm_fwd")
    xk = rms_fwd(xk_raw, 0, X_INNER, gxk, X_HEADS, X_HEAD_DIM, BF16, "xk_norm_fwd")
    mlen = memf.shape[0] // bsz
    xq3, xk3, xv3 = xq.reshape(bsz, seq, X_INNER), xk.reshape(bsz, mlen, X_INNER), xv.reshape(bsz, mlen, X_INNER)
    xo, xlse = attn_fwd(xq3, xk3, xv3, heads=X_HEADS, dk=X_HEAD_DIM, dv=X_HEAD_DIM, v_mul=1, v_off=0,
                        causal=False, scale=X_HEAD_DIM ** -0.5, tq=512, tk=256, name="xattn_fwd")
    xo2 = xo.reshape(t_rows, X_INNER)
    x3 = mm([(xo2, w["w_xo"])], "nt", extras=(x2,), epilogue=_add_res, name="xo_proj")
    r.update(h2=h2, mn=mn, xq_raw=xq_raw, xk_raw=xk_raw, gxq=gxq, gxk=gxk, xq3=xq3, xk3=xk3, xv3=xv3,
             xo=xo, xlse=xlse, x3=x3)

    h3 = rms_fwd(x3, 0, D_MODEL, _row(p["ffn_norm_g"]), 1, D_MODEL, BF16, "norm3_fwd")
    gate, up, act = mm([(h3, w["w_gate"]), (h3, w["w_up"])], "nt", epilogue=_swiglu_fwd, separate=True,
                       out_dtypes=(BF16, BF16, BF16), name="gate_up_proj")
    x4 = mm([(act, w["w_down"])], "nn", extras=(x3,), epilogue=_add_res, tm=512, name="down_proj")
    r.update(h3=h3, gate=gate, up=up, act=act)
    return x4, r


def layer_bwd(dx4, r, memf, w, p, tabs, bsz):
    t_rows = dx4.shape[0]
    seq = t_rows // bsz
    big, small = {}, {}

    dgate, dup = mm([(dx4, w["w_down"])], "nt", extras=(r["gate"], r["up"]), epilogue=_swiglu_bwd,
                    out_dtypes=(BF16, BF16), name="down_proj_bwd")
    big["w_down"] = mm([(r["act"], dx4)], "tn", out_dtypes=(BF16,), name="down_proj_wgrad")
    big["w_gate"] = mm([(dgate, r["h3"])], "tn", out_dtypes=(BF16,), tm=512, tn=1024, name="gate_proj_wgrad")
    big["w_up"] = mm([(dup, r["h3"])], "tn", out_dtypes=(BF16,), tm=512, tn=1024, name="gate_proj_wgrad")
    dh3 = mm([(dgate, w["w_gate"]), (dup, w["w_up"])], "nn", tm=512, tn=256, name="gate_up_proj_bwd")
    dx3, g = rms_bwd(r["x3"], 0, D_MODEL, _row(p["ffn_norm_g"]), 1, D_MODEL, dh3, dx4, "norm3_bwd")
    small["ffn_norm_g"] = g[0]

    xo2 = r["xo"].reshape(t_rows, X_INNER)
    dxo = mm([(dx3, w["w_xo"])], "nn", name="xo_proj_bwd")
    big["w_xo"] = mm([(dx3, xo2)], "tn", out_dtypes=(BF16,), tm=512, name="xo_proj_wgrad")
    dxq, dxk, dxv = attn_bwd(r["xq3"], r["xk3"], r["xv3"], dxo.reshape(bsz, seq, X_INNER), 0, r["xo"], r["xlse"],
                             heads=X_HEADS, dk=X_HEAD_DIM, dv=X_HEAD_DIM, v_mul=1, v_off=0, causal=False,
                             scale=X_HEAD_DIM ** -0.5, tq=512, tk=256, name="xattn_bwd")
    mrows = memf.shape[0]
    dxq_raw, g = rms_bwd(r["xq_raw"], 0, X_INNER, r["gxq"], X_HEADS, X_HEAD_DIM, dxq.reshape(t_rows, X_INNER),
                         None, "xq_norm_bwd")
    small["xq_norm_g"] = g.reshape(X_HEADS, X_HEAD_DIM).sum(0)
    dxk_raw, g = rms_bwd(r["xk_raw"], 0, X_INNER, r["gxk"], X_HEADS, X_HEAD_DIM, dxk.reshape(mrows, X_INNER),
                         None, "xk_norm_bwd")
    small["xk_norm_g"] = g.reshape(X_HEADS, X_HEAD_DIM).sum(0)
    dxv2 = dxv.reshape(mrows, X_INNER)
    big["w_xq"] = mm([(r["h2"], dxq_raw)], "tn", out_dtypes=(BF16,), name="xq_proj_wgrad")
    big["w_xk"] = mm([(r["mn"], dxk_raw)], "tn", out_dtypes=(BF16,), name="xkv_proj_wgrad")
    big["w_xv"] = mm([(r["mn"], dxv2)], "tn", out_dtypes=(BF16,), name="xkv_proj_wgrad")
    dh2 = mm([(dxq_raw, w["w_xq"])], "nt", name="xq_proj_bwd")
    dmn = mm([(dxk_raw, w["w_xk"]), (dxv2, w["w_xv"])], "nt", name="xkv_proj_bwd")
    _, g = rms_bwd(memf, 0, D_MODEL, _row(p["mem_norm_g"]), 1, D_MODEL, dmn, None, "mem_norm_bwd")
    small["mem_norm_g"] = g[0]
    dx2, g = rms_bwd(r["x2"], 0, D_MODEL, _row(p["xattn_norm_g"]), 1, D_MODEL, dh2, dx3, "norm2_bwd")
    small["xattn_norm_g"] = g[0]

    dmixed = mm([(dx2, w["w_out"])], "nt", name="out_proj_bwd")
    big["w_out"] = mm([(r["mixed_in"], dx2)], "tn", out_dtypes=(BF16,), name="out_proj_wgrad")
    dmixed3 = dmixed.reshape(bsz, seq, D_MODEL)

    dq_fin, dk_fin, dv = attn_bwd(r["q_fin"], r["k_fin"], r["kv3"], dmixed3, MLA_HEADS, r["o"], r["lse"],
                                  heads=MLA_HEADS, dk=MLA_QK_PAD, dv=MLA_V, v_mul=2, v_off=1, causal=True,
                                  scale=MLA_QK ** -0.5, tq=512, tk=512, name="mla_attn_bwd")
    dqh = rope(dq_fin, *tabs, True, F32, "rope_bwd").reshape(t_rows, 2048)
    dkh = rope(dk_fin, *tabs, True, F32, "rope_bwd").reshape(t_rows, 2048)
    dq_raw, g = rms_bwd(r["q_raw"], 0, 2048, r["gq"], MLA_HEADS, MLA_QK, dqh, None, "q_head_norm_bwd")
    small["mla_q_norm_g"] = g.reshape(MLA_HEADS, MLA_QK_PAD)[:, :MLA_QK].sum(0)
    dk_raw, g = rms_bwd(r["k_raw"], 0, 2048, r["gk"], MLA_HEADS, MLA_QK, dkh, None, "q_head_norm_bwd")
    small["mla_k_norm_g"] = g.reshape(MLA_HEADS, MLA_QK_PAD)[:, :MLA_QK].sum(0)
    big["w_q_b"] = _unpad_head_rows(mm([(dq_raw, r["qn"])], "tn", out_dtypes=(BF16,), tm=512,
                                       name="qkv_b_proj_wgrad"))
    dqn = mm([(dq_raw, w["w_q_b"])], "nn", name="qkv_b_proj_bwd")
    dk_raw3 = dk_raw.reshape(t_rows, MLA_HEADS, MLA_QK_PAD)
    dkv = jnp.concatenate([dk_raw3[:, :, :128], dv.reshape(t_rows, MLA_HEADS, MLA_V)], axis=-1)
    dkv = dkv.reshape(t_rows, 2048)
    dkrope = dk_raw3[:, :, 128:192].sum(1)
    big["w_kv_b"] = mm([(dkv, r["kvn"])], "tn", out_dtypes=(BF16,), tm=512, name="qkv_b_proj_wgrad")
    dkvn = mm([(dkv, w["w_kv_b"])], "nn", name="qkv_b_proj_bwd")
    dq_a, g = rms_bwd(r["proj"], 5, Q_LORA, _row(p["q_a_norm_g"]), 1, Q_LORA, dqn, None, "qa_norm_bwd")
    small["q_a_norm_g"] = g[0]
    dkv_a, g = rms_bwd(r["proj"], 6, Q_LORA, _row(p["kv_a_norm_g"]), 1, Q_LORA, dkvn, None, "qa_norm_bwd")
    small["kv_a_norm_g"] = g[0]

    dyg, g = rms_bwd(r["yg"], 0, SSD_INNER, _row(p["ssd_norm_g"]), 2, 512, dmixed, None, "ssd_norm_bwd")
    small["ssd_norm_g"] = g[0]
    dy, dz = ew(_gate_bwd, [(dyg, 0), (r["y"], 0), (r["proj"], 0)], SSD_INNER, (F32, F32), "ssd_gate_bwd")
    dxbc_c, ddt_raw, g_dtb, g_alog, g_dsk = ssd_bwd(
        r["xbc"], r["dt_raw"], _row(p["dt_bias"]), _row(p["a_log"]), _row(p["d_skip"]), r["states"],
        dy.reshape(bsz, seq, SSD_INNER), "ssd_bwd")
    small["dt_bias"], small["a_log"], small["d_skip"] = g_dtb[0], g_alog[0], g_dsk[0]
    dxbc, dw8 = conv_bwd(r["proj"].reshape(bsz, seq, IN_PAD), r["w8"], dxbc_c, "conv_bwd")
    small["conv_w"] = dw8[:4]
    small["conv_b"] = dw8[4]

    dtail = jnp.concatenate([dkrope, ddt_raw.reshape(t_rows, 16), jnp.zeros((t_rows, 48 + 128), F32)], axis=1)
    dproj = jnp.concatenate([dz.astype(BF16), dxbc.reshape(t_rows, CONV_DIM).astype(BF16), dq_a.astype(BF16),
                             dkv_a.astype(BF16), dtail.astype(BF16)], axis=1)
    big["w_in"] = _unpad_w_in_t(mm([(dproj, r["h1"])], "tn", out_dtypes=(BF16,), tm=768, tn=1024,
                                   name="in_proj_wgrad"))
    dh1 = mm([(dproj, w["w_in"])], "nn", name="in_proj_bwd")
    dx, g = rms_bwd(r["x"], 0, D_MODEL, _row(p["attn_norm_g"]), 1, D_MODEL, dh1, dx2, "norm1_bwd")
    small["attn_norm_g"] = g[0]
    return dx, big, small


def _rope_tables(positions):
    inv_freq = 1.0 / (ROPE_THETA ** (jnp.arange(0, 64, 2, dtype=F32) / 64))
    ang = positions.astype(F32)[..., None] * inv_freq
    c, s = jnp.cos(ang), jnp.sin(ang)
    z32, z64 = jnp.zeros_like(c), jnp.zeros(c.shape[:-1] + (64,), F32)
    return (jnp.concatenate([c, c, z64], -1), jnp.concatenate([-s, z32, z64], -1),
            jnp.concatenate([z32, s, z64], -1))


def kernel(x, mem, positions, attn_norm_g, w_in, conv_w, conv_b, dt_bias, a_log, d_skip, ssd_norm_g, q_a_norm_g, w_q_b, kv_a_norm_g, w_kv_b, mla_q_norm_g, mla_k_norm_g, w_out, xattn_norm_g, mem_norm_g, w_xq, w_xk, w_xv, xq_norm_g, xk_norm_g, w_xo, ffn_norm_g, w_gate, w_up, w_down, loss_target, m_attn_norm_g, m_w_in, m_conv_w, m_conv_b, m_dt_bias, m_a_log, m_d_skip, m_ssd_norm_g, m_q_a_norm_g, m_w_q_b, m_kv_a_norm_g, m_w_kv_b, m_mla_q_norm_g, m_mla_k_norm_g, m_w_out, m_xattn_norm_g, m_mem_norm_g, m_w_xq, m_w_xk, m_w_xv, m_xq_norm_g, m_xk_norm_g, m_w_xo, m_ffn_norm_g, m_w_gate, m_w_up, m_w_down, v_attn_norm_g, v_w_in, v_conv_w, v_conv_b, v_dt_bias, v_a_log, v_d_skip, v_ssd_norm_g, v_q_a_norm_g, v_w_q_b, v_kv_a_norm_g, v_w_kv_b, v_mla_q_norm_g, v_mla_k_norm_g, v_w_out, v_xattn_norm_g, v_mem_norm_g, v_w_xq, v_w_xk, v_w_xv, v_xq_norm_g, v_xk_norm_g, v_w_xo, v_ffn_norm_g, v_w_gate, v_w_up, v_w_down):
    args = dict(locals())
    weights = {n: args[n] for n in WEIGHT_ORDER}
    mom_m = {n: args["m_" + n] for n in WEIGHT_ORDER}
    mom_v = {n: args["v_" + n] for n in WEIGHT_ORDER}
    bsz, seq, _ = x.shape
    t_rows = bsz * seq
    xf = x.reshape(t_rows, D_MODEL)
    memf = mem.reshape(-1, D_MODEL)
    tabs = _rope_tables(positions)

    full = []
    for l in range(DEPTH):
        blobs = [_pack_local(fam, {n: weights[n][l] for n, _, _, _ in fam}, BF16) for fam, _ in FAMS]
        gathered = all_gather(blobs, "weight_all_gather")
        wl = {}
        for (fam, _), g in zip(FAMS, gathered):
            wl.update(_unpack_gathered(fam, g))
        wl["w_in"] = _pad_w_in_t(wl["w_in"])
        wl["w_q_b"] = _pad_head_rows(wl["w_q_b"])
        full.append(wl)
    small_p = [{n: weights[n][l] for n, _ in SMALL} for l in range(DEPTH)]
    cw = all_gather([conv_w.reshape(-1, 128)], "conv_w_all_gather")[0].reshape(N_DEV, DEPTH, 4, -1)
    cw = cw.transpose(1, 2, 0, 3).reshape(DEPTH, 4, CONV_DIM)
    for l in range(DEPTH):
        small_p[l]["conv_w_full"] = cw[l]

    saved = []
    h = xf
    for l in range(DEPTH):
        h, res = layer_fwd(h, memf, full[l], small_p[l], tabs, bsz)
        saved.append(res)
    dy, loss_part = loss_head(h, loss_target.reshape(t_rows, D_MODEL), "loss_head")
    loss = lax.psum(loss_part[0, 0], ("x", "y", "c"))

    small_g = [None] * DEPTH
    outs = {k: {} for k in ("g", "d", "m", "v")}
    per_layer = []
    dh = dy
    for l in reversed(range(DEPTH)):
        dh, big_g, small_g[l] = layer_bwd(dh, saved[l], memf, full[l], small_p[l], tabs, bsz)
        recv = exchange([_pack_full_grads(fam, big_g) for fam, _ in FAMS], "grad_exchange")
        layer_out = [{}, {}, {}, {}]
        for (fam, width), rcv in zip(FAMS, recv):
            loc = [_pack_local(fam, {n: src[n][l] for n, _, _, _ in fam}, F32) for src in (weights, mom_m, mom_v)]
            res = adamw(rcv, *loc, "adamw_big", tr=48 if width == 2048 else 64)
            for ki in range(4):
                layer_out[ki].update(_unpack_local(fam, res[ki]))
        per_layer.append(layer_out)
    per_layer = per_layer[::-1]
    for ki, key in enumerate(("g", "d", "m", "v")):
        for fam, _ in FAMS:
            for n, _, _, _ in fam:
                outs[key][n] = jnp.stack([per_layer[l][ki][n] for l in range(DEPTH)])

    sg = _pack_small({n: jnp.stack([small_g[l][n] for l in range(DEPTH)]) for n, _ in SMALL})
    cg = jnp.stack([small_g[l]["conv_w"] for l in range(DEPTH)]).reshape(CONVW_ROWS, 128)
    parts = all_gather([jnp.concatenate([sg, cg], axis=0)], "small_grad_all_gather")[0]
    sm = adamw(parts[:, :SMALL_ROWS], _pack_small({n: weights[n] for n, _ in SMALL}),
               _pack_small({n: mom_m[n] for n, _ in SMALL}), _pack_small({n: mom_v[n] for n, _ in SMALL}),
               "adamw_small", tr=SMALL_ROWS)
    for ki, key in enumerate(("g", "d", "m", "v")):
        outs[key].update(_unpack_small(sm[ki]))
    my_idx = 4 * lax.axis_index("x") + 2 * lax.axis_index("y") + lax.axis_index("c")
    cparts = parts[:, SMALL_ROWS:].reshape(N_DEV, DEPTH, 4, N_DEV, -1)
    cparts = lax.dynamic_index_in_dim(cparts, my_idx, axis=3, keepdims=False).reshape(N_DEV, -1, 128)
    cres = adamw(cparts, conv_w.reshape(-1, 128), m_conv_w.reshape(-1, 128), v_conv_w.reshape(-1, 128),
                 "adamw_conv_w")
    for ki, key in enumerate(("g", "d", "m", "v")):
        outs[key]["conv_w"] = cres[ki].reshape(conv_w.shape)

    grad_x = dh.reshape(bsz, seq, D_MODEL)
    return (loss, grad_x, *[outs["g"][n] for n in WEIGHT_ORDER], *[outs["d"][n] for n in WEIGHT_ORDER],
            *[outs["m"][n] for n in WEIGHT_ORDER], *[outs["v"][n] for n in WEIGHT_ORDER])
```

```python
import functools

import jax
import jax.numpy as jnp
from jax import lax
from jax.experimental import pallas as pl
from jax.experimental.pallas import tpu as pltpu

F32 = jnp.float32
BF16 = jnp.bfloat16

D_MODEL = 2048
DEPTH = 4
N_DEV = 8
SSD_INNER = 1024
SSD_HEADS = 16
SSD_PAIRS = 8
SSD_STATE = 128
SSD_CHUNK = 128
CONV_DIM = 1536
MLA_HEADS = 8
MLA_QK = 192
MLA_QK_PAD = 256
MLA_V = 128
Q_LORA = 512
X_HEADS = 4
X_HEAD_DIM = 128
X_INNER = 512
FFN = 5632
IN_COLS = 3664
IN_PAD = 3840
TAIL_COL = 3584
ROPE_THETA = 10000.0
EPS = 1e-6
NEG = -1e30
VMEM_LIMIT = 56 * 1024 * 1024
HI = lax.Precision.HIGHEST

ADAM_LR = 0.001
ADAM_B1 = 0.9
ADAM_B2 = 0.999
ADAM_EPS = 1e-08
ADAM_WD = 0.01
ADAM_STEP = 10

SMALL_ROWS = 392
CONVW_ROWS = 192

FAM_A = (("w_in", 458, 464, True), ("w_out", 256, 256, False), ("w_gate", 704, 704, True),
         ("w_up", 704, 704, True), ("w_down", 704, 704, False))
FAM_B = (("w_q_b", 192, 192, True), ("w_kv_b", 256, 256, True), ("w_xq", 256, 256, False),
         ("w_xk", 256, 256, False), ("w_xv", 256, 256, False), ("w_xo", 256, 256, True))
FAMS = ((FAM_A, 2048), (FAM_B, 512))
PARTIAL_TR = (944, 1472)
SMALL = (
    ("attn_norm_g", 2048), ("conv_b", 1536), ("dt_bias", 16), ("a_log", 16), ("d_skip", 16),
    ("ssd_norm_g", 1024), ("q_a_norm_g", 512), ("kv_a_norm_g", 512), ("mla_q_norm_g", 192),
    ("mla_k_norm_g", 192), ("xattn_norm_g", 2048), ("mem_norm_g", 2048), ("xq_norm_g", 128),
    ("xk_norm_g", 128), ("ffn_norm_g", 2048),
)
WEIGHT_ORDER = ("attn_norm_g", "w_in", "conv_w", "conv_b", "dt_bias", "a_log", "d_skip", "ssd_norm_g",
                "q_a_norm_g", "w_q_b", "kv_a_norm_g", "w_kv_b", "mla_q_norm_g", "mla_k_norm_g", "w_out",
                "xattn_norm_g", "mem_norm_g", "w_xq", "w_xk", "w_xv", "xq_norm_g", "xk_norm_g", "w_xo",
                "ffn_norm_g", "w_gate", "w_up", "w_down")


def _cp(sem=None):
    if sem is None:
        return pltpu.CompilerParams(vmem_limit_bytes=VMEM_LIMIT)
    return pltpu.CompilerParams(vmem_limit_bytes=VMEM_LIMIT, dimension_semantics=sem)


def _tile(n, pref):
    if n <= pref:
        return n
    t = (pref // 128) * 128
    while t > 128 and n % t:
        t -= 128
    return t


def _sigmoid(v):
    return 1.0 / (1.0 + jnp.exp(-v))


class Rider:
    def __init__(self, ins, out_shapes, sems, start, finish, aliases=None):
        self.ins, self.out_shapes, self.sems = list(ins), list(out_shapes), list(sems)
        self.start, self.finish = start, finish
        self.aliases = dict(aliases or {})


def _attach(rider, body, grid, in_specs, out_specs, out_shape, scratch, args):
    n_in, n_out, n_scr = len(in_specs), len(out_specs), len(scratch)
    r_in, r_out = len(rider.ins), len(rider.out_shapes)

    def wrapped(*refs):
        ins, rins = refs[:n_in], refs[n_in:n_in + r_in]
        o0 = n_in + r_in
        outs, routs = refs[o0:o0 + n_out], refs[o0 + n_out:o0 + n_out + r_out]
        s0 = o0 + n_out + r_out
        scr, rsems = refs[s0:s0 + n_scr], refs[s0 + n_scr:]
        first = last = None
        for d, g in enumerate(grid):
            i = pl.program_id(d)
            first = (i == 0) if first is None else jnp.logical_and(first, i == 0)
            last = (i == g - 1) if last is None else jnp.logical_and(last, i == g - 1)
        if first is None:
            rider.start(rins, routs, rsems)
            body(*ins, *outs, *scr)
            rider.finish(rins, routs, rsems)
            return
        pl.when(first)(lambda: rider.start(rins, routs, rsems))
        body(*ins, *outs, *scr)
        pl.when(last)(lambda: rider.finish(rins, routs, rsems))

    hbm = pl.BlockSpec(memory_space=pl.ANY)
    io_aliases = {n_in + ri: n_out + ro for ri, ro in rider.aliases.items()}
    return (wrapped, list(in_specs) + [hbm] * r_in, list(out_specs) + [hbm] * r_out,
            list(out_shape) + rider.out_shapes, list(scratch) + rider.sems, list(args) + rider.ins, io_aliases)


def comm_call(riders, name):
    def start(rins, routs, sems):
        for r, (i0, i1, o0, o1, s0, s1) in zip(riders, spans):
            r.start(rins[i0:i1], routs[o0:o1], sems[s0:s1])

    def finish(rins, routs, sems):
        for r, (i0, i1, o0, o1, s0, s1) in zip(riders, spans):
            r.finish(rins[i0:i1], routs[o0:o1], sems[s0:s1])

    spans, i, o, s = [], 0, 0, 0
    for r in riders:
        spans.append((i, i + len(r.ins), o, o + len(r.out_shapes), s, s + len(r.sems)))
        i, o, s = i + len(r.ins), o + len(r.out_shapes), s + len(r.sems)
    aliases = {i0 + ri: o0 + ro for r, (i0, _, o0, _, _, _) in zip(riders, spans) for ri, ro in r.aliases.items()}
    both = Rider([a for r in riders for a in r.ins], [a for r in riders for a in r.out_shapes],
                 [a for r in riders for a in r.sems], start, finish, aliases)
    body, in_specs, out_specs, out_shape, scratch, args, io_aliases = _attach(
        both, lambda: None, (), [], [], [], [], [])
    outs = pl.pallas_call(body, name=name, in_specs=in_specs, out_specs=out_specs, out_shape=out_shape,
                          scratch_shapes=scratch, input_output_aliases=io_aliases)(*args)
    return [outs[a:b] for (_, _, a, b, _, _) in spans]


def rms_fwd(x, col_blk, width, g, groups, n_valid, out_dtype, name, tr=512):
    rows = x.shape[0]
    tr = min(tr, rows)
    wg = width // groups

    def body(x_ref, g_ref, y_ref):
        for gi in range(groups):
            sl = slice(gi * wg, (gi + 1) * wg)
            xv = x_ref[:, sl].astype(F32)
            r = lax.rsqrt(jnp.sum(xv * xv, axis=-1, keepdims=True) * (1.0 / n_valid) + EPS)
            y_ref[:, sl] = ((xv * r) * g_ref[:, sl]).astype(out_dtype)

    return pl.pallas_call(
        body, name=name, grid=(rows // tr,),
        in_specs=[pl.BlockSpec((tr, width), lambda i: (i, col_blk)), pl.BlockSpec((1, width), lambda i: (0, 0))],
        out_specs=pl.BlockSpec((tr, width), lambda i: (i, 0)),
        out_shape=jax.ShapeDtypeStruct((rows, width), out_dtype), compiler_params=_cp(("parallel",)))(x, g)


def rms_bwd(x, col_blk, width, g, groups, n_valid, dy, add, name, tr=512):
    rows = x.shape[0]
    tr = min(tr, rows)
    wg = width // groups
    has_add = add is not None

    def body(*refs):
        if has_add:
            x_ref, g_ref, dy_ref, add_ref, dx_ref, dg_ref = refs
        else:
            x_ref, g_ref, dy_ref, dx_ref, dg_ref = refs

        @pl.when(pl.program_id(0) == 0)
        def _():
            dg_ref[...] = jnp.zeros(dg_ref.shape, F32)

        for gi in range(groups):
            sl = slice(gi * wg, (gi + 1) * wg)
            xv = x_ref[:, sl].astype(F32)
            dyv = dy_ref[:, sl].astype(F32)
            r = lax.rsqrt(jnp.sum(xv * xv, axis=-1, keepdims=True) * (1.0 / n_valid) + EPS)
            xh = xv * r
            dyg = dyv * g_ref[:, sl]
            c = jnp.sum(dyg * xh, axis=-1, keepdims=True) * (1.0 / n_valid)
            dx = r * (dyg - xh * c)
            if has_add:
                dx = dx + add_ref[:, sl]
            dx_ref[:, sl] = dx
            dg_ref[:, sl] += jnp.sum(dyv * xh, axis=0, keepdims=True)

    in_specs = [pl.BlockSpec((tr, width), lambda i: (i, col_blk)), pl.BlockSpec((1, width), lambda i: (0, 0)),
                pl.BlockSpec((tr, width), lambda i: (i, 0))]
    args = [x, g, dy]
    if has_add:
        in_specs.append(pl.BlockSpec((tr, width), lambda i: (i, 0)))
        args.append(add)
    return pl.pallas_call(
        body, name=name, grid=(rows // tr,), in_specs=in_specs,
        out_specs=[pl.BlockSpec((tr, width), lambda i: (i, 0)), pl.BlockSpec((1, width), lambda i: (0, 0))],
        out_shape=[jax.ShapeDtypeStruct((rows, width), F32), jax.ShapeDtypeStruct((1, width), F32)],
        compiler_params=_cp(("arbitrary",)))(*args)


def mm(pairs, mode, *, extras=(), epilogue=None, out_dtypes=(F32,), separate=False, tm=1024, tn=512, name,
       rider=None):
    a0, b0 = pairs[0]
    if mode == "nn":
        m_dim, n_dim = a0.shape[0], b0.shape[1]
        dn = (((1,), (0,)), ((), ()))
    elif mode == "nt":
        m_dim, n_dim = a0.shape[0], b0.shape[0]
        dn = (((1,), (1,)), ((), ()))
    else:
        m_dim, n_dim = a0.shape[1], b0.shape[1]
        dn = (((0,), (0,)), ((), ()))
    tm = _tile(m_dim, tm)
    tn = _tile(n_dim, tn)
    n_pairs = len(pairs)
    n_extra = len(extras)

    def body(*refs):
        ins = refs[:2 * n_pairs]
        ex = refs[2 * n_pairs:2 * n_pairs + n_extra]
        outs = refs[2 * n_pairs + n_extra:]
        accs = []
        for k in range(n_pairs):
            a = ins[2 * k][...].astype(BF16)
            b = ins[2 * k + 1][...].astype(BF16)
            accs.append(lax.dot_general(a, b, dn, preferred_element_type=F32))
        if not separate:
            total = accs[0]
            for extra_acc in accs[1:]:
                total = total + extra_acc
            accs = [total]
        res = epilogue(*accs, *[e[...] for e in ex]) if epilogue is not None else tuple(accs)
        for o_ref, r in zip(outs, res):
            o_ref[...] = r.astype(o_ref.dtype)

    in_specs, args = [], []
    for a, b in pairs:
        if mode == "nn":
            in_specs += [pl.BlockSpec((tm, a.shape[1]), lambda i, j: (i, 0)),
                         pl.BlockSpec((b.shape[0], tn), lambda i, j: (0, j))]
        elif mode == "nt":
            in_specs += [pl.BlockSpec((tm, a.shape[1]), lambda i, j: (i, 0)),
                         pl.BlockSpec((tn, b.shape[1]), lambda i, j: (j, 0))]
        else:
            in_specs += [pl.BlockSpec((a.shape[0], tm), lambda i, j: (0, i)),
                         pl.BlockSpec((b.shape[0], tn), lambda i, j: (0, j))]
        args += [a, b]
    for e in extras:
        in_specs.append(pl.BlockSpec((tm, tn), lambda i, j: (i, j)))
        args.append(e)
    grid = (m_dim // tm, n_dim // tn)
    out_specs = [pl.BlockSpec((tm, tn), lambda i, j: (i, j)) for _ in out_dtypes]
    out_shape = [jax.ShapeDtypeStruct((m_dim, n_dim), dt) for dt in out_dtypes]
    scratch, sem, io_aliases = [], ("parallel", "parallel"), {}
    if rider is not None:
        body, in_specs, out_specs, out_shape, scratch, args, io_aliases = _attach(
            rider, body, grid, in_specs, out_specs, out_shape, scratch, args)
        sem = ("arbitrary", "arbitrary")
    outs = pl.pallas_call(body, name=name, grid=grid, in_specs=in_specs, out_specs=out_specs, out_shape=out_shape,
                          scratch_shapes=scratch, input_output_aliases=io_aliases, compiler_params=_cp(sem))(*args)
    main = outs[0] if len(out_dtypes) == 1 else outs[:len(out_dtypes)]
    return main if rider is None else (main, outs[len(out_dtypes):])


def _add_res(acc, res):
    return (acc + res,)


def _swiglu_fwd(gate, up):
    gb = gate.astype(BF16).astype(F32)
    ub = up.astype(BF16).astype(F32)
    return gate, up, gb * _sigmoid(gb) * ub


def _swiglu_bwd(dact, gate, up):
    gv = gate.astype(F32)
    uv = up.astype(F32)
    sg = _sigmoid(gv)
    return dact * uv * (sg * (1.0 + gv * (1.0 - sg))), dact * (gv * sg)


def ew(fn, ins, width, out_dtypes, name, tr=512):
    rows = ins[0][0].shape[0]
    tr = min(tr, rows)
    n_in = len(ins)

    def body(*refs):
        res = fn(*[r[...].astype(F32) for r in refs[:n_in]])
        for o_ref, r in zip(refs[n_in:], res):
            o_ref[...] = r.astype(o_ref.dtype)

    in_specs = [pl.BlockSpec((tr, width), functools.partial(lambda i, cb: (i, cb), cb=cb)) for _, cb in ins]
    outs = pl.pallas_call(
        body, name=name, grid=(rows // tr,), in_specs=in_specs,
        out_specs=[pl.BlockSpec((tr, width), lambda i: (i, 0)) for _ in out_dtypes],
        out_shape=[jax.ShapeDtypeStruct((rows, width), dt) for dt in out_dtypes],
        compiler_params=_cp(("parallel",)))(*[a for a, _ in ins])
    return outs[0] if len(out_dtypes) == 1 else outs


def _gate_fwd(y, z):
    return (y * (z * _sigmoid(z)),)


def _gate_bwd(dyg, y, z):
    sg = _sigmoid(z)
    return dyg * (z * sg), dyg * y * (sg * (1.0 + z * (1.0 - sg)))


CONV_CB = 512
CONV_OFF = 2


def _conv_pre(x, w_ref, row):
    acc = x * w_ref[pl.ds(3, 1), :] + w_ref[pl.ds(4, 1), :]
    shifted = []
    for j in (1, 2, 3):
        xs = jnp.where(row >= j, pltpu.roll(x, j, 0), 0.0)
        shifted.append(xs)
        acc = acc + xs * w_ref[pl.ds(3 - j, 1), :]
    return acc, shifted


def conv_fwd(proj3, w8, name):
    bsz, seq, _ = proj3.shape

    def body(x_ref, w_ref, o_ref):
        x = x_ref[0]
        row = lax.broadcasted_iota(jnp.int32, x.shape, 0)
        pre, _ = _conv_pre(x, w_ref, row)
        o_ref[0] = pre * _sigmoid(pre)

    return pl.pallas_call(
        body, name=name, grid=(bsz, CONV_DIM // CONV_CB),
        in_specs=[pl.BlockSpec((1, seq, CONV_CB), lambda b, c: (b, 0, c + CONV_OFF)),
                  pl.BlockSpec((8, CONV_CB), lambda b, c: (0, c))],
        out_specs=pl.BlockSpec((1, seq, CONV_CB), lambda b, c: (b, 0, c)),
        out_shape=jax.ShapeDtypeStruct((bsz, seq, CONV_DIM), F32),
        compiler_params=_cp(("parallel", "parallel")))(proj3, w8)


def conv_bwd(proj3, w8, dout, name):
    bsz, seq, _ = proj3.shape

    def body(x_ref, w_ref, do_ref, dx_ref, dw_ref):
        @pl.when(pl.program_id(1) == 0)
        def _():
            dw_ref[...] = jnp.zeros(dw_ref.shape, F32)

        x = x_ref[0]
        row = lax.broadcasted_iota(jnp.int32, x.shape, 0)
        pre, shifted = _conv_pre(x, w_ref, row)
        sg = _sigmoid(pre)
        dpre = do_ref[0] * (sg * (1.0 + pre * (1.0 - sg)))
        dx = dpre * w_ref[pl.ds(3, 1), :]
        for j in (1, 2, 3):
            fut = jnp.where(row < seq - j, pltpu.roll(dpre, seq - j, 0), 0.0)
            dx = dx + fut * w_ref[pl.ds(3 - j, 1), :]
            dw_ref[pl.ds(3 - j, 1), :] += jnp.sum(dpre * shifted[j - 1], axis=0, keepdims=True)
        dw_ref[pl.ds(3, 1), :] += jnp.sum(dpre * x, axis=0, keepdims=True)
        dw_ref[pl.ds(4, 1), :] += jnp.sum(dpre, axis=0, keepdims=True)
        dx_ref[0] = dx

    return pl.pallas_call(
        body, name=name, grid=(CONV_DIM // CONV_CB, bsz),
        in_specs=[pl.BlockSpec((1, seq, CONV_CB), lambda c, b: (b, 0, c + CONV_OFF)),
                  pl.BlockSpec((8, CONV_CB), lambda c, b: (0, c)),
                  pl.BlockSpec((1, seq, CONV_CB), lambda c, b: (b, 0, c))],
        out_specs=[pl.BlockSpec((1, seq, CONV_CB), lambda c, b: (b, 0, c)),
                   pl.BlockSpec((8, CONV_CB), lambda c, b: (0, c))],
        out_shape=[jax.ShapeDtypeStruct((bsz, seq, CONV_DIM), F32), jax.ShapeDtypeStruct((8, CONV_DIM), F32)],
        compiler_params=_cp(("parallel", "arbitrary")))(proj3, w8, dout)


def _softplus(v):
    t = jnp.exp(-jnp.abs(v))
    small = t * (1.0 - t * (0.5 - t * (1.0 / 3.0)))
    return jnp.maximum(v, 0.0) + jnp.where(t < 0.01, small, jnp.log(1.0 + t))


def _ssd_chunk_prelude(dt_ref, dtb_ref, alog_ref):
    L = SSD_CHUNK
    raw = dt_ref[0] + dtb_ref[...]
    dt = _softplus(raw)
    a_neg = -jnp.exp(alog_ref[...])
    a = dt * a_neg
    r_i = lax.broadcasted_iota(jnp.int32, (L, L), 0)
    c_i = lax.broadcasted_iota(jnp.int32, (L, L), 1)
    tri_low = (r_i >= c_i).astype(F32)
    tri_up = (r_i <= c_i).astype(F32)
    acs = jnp.dot(tri_low, a, precision=HI, preferred_element_type=F32)
    acs_t = lax.dot_general(a, tri_up, (((0,), (0,)), ((), ())), precision=HI,
                            preferred_element_type=F32)
    tot = jnp.sum(a, axis=0, keepdims=True)
    return raw, dt, a_neg, acs, acs_t, tot, r_i, c_i, tri_up


def _col(arr16, e):
    lane = lax.broadcasted_iota(jnp.int32, arr16.shape, 1)
    return jnp.sum(jnp.where(lane == e, arr16, 0.0), axis=1, keepdims=True)


def _rowvec(arr_t, e):
    sub = lax.broadcasted_iota(jnp.int32, arr_t.shape, 0)
    return jnp.sum(jnp.where(sub == e, arr_t, 0.0), axis=0, keepdims=True)


def _pair(lo_mask, v0, v1):
    return jnp.where(lo_mask, v0, v1)


def ssd_fwd(xbc, dt_raw, dtb, alog, dsk, name):
    bsz, seq, _ = xbc.shape
    L = SSD_CHUNK
    nc = seq // L

    def body(x_ref, b_ref, c_ref, dt_ref, dtb_ref, alog_ref, dsk_ref, y_ref, st_ref, h_sc):
        @pl.when(pl.program_id(1) == 0)
        def _():
            h_sc[...] = jnp.zeros(h_sc.shape, F32)

        _, dt, _, acs, acs_t, tot, r_i, c_i, _ = _ssd_chunk_prelude(dt_ref, dtb_ref, alog_ref)
        causal = r_i >= c_i
        lo = lax.broadcasted_iota(jnp.int32, (1, 128), 1) < 64
        dskv = dsk_ref[...]
        for grp in range(2):
            bg = b_ref[0, :, grp * 128:(grp + 1) * 128].astype(BF16)
            cg = c_ref[0, :, grp * 128:(grp + 1) * 128].astype(BF16)
            cb = lax.dot_general(cg, bg, (((1,), (1,)), ((), ())), preferred_element_type=F32)
            for kk in range(4):
                k = grp * 4 + kk
                e0, e1 = 2 * k, 2 * k + 1
                sl = slice(k * 128, (k + 1) * 128)
                xp = x_ref[0, :, sl]
                cols = [_col(acs, e0), _col(acs, e1)]
                dtp = _pair(lo, _col(dt, e0), _col(dt, e1))
                xdt = xp * dtp
                ydiag = jnp.zeros((L, 128), F32)
                for sub, e in enumerate((e0, e1)):
                    diff = cols[sub] - _rowvec(acs_t, e)
                    lam = jnp.exp(jnp.where(causal, diff, NEG))
                    m_e = (cb * lam).astype(BF16)
                    xm = jnp.where(lo if sub == 0 else jnp.logical_not(lo), xdt, 0.0).astype(BF16)
                    ydiag = ydiag + jnp.dot(m_e, xm, preferred_element_type=F32)
                h = h_sc[k]
                st_ref[0, 0, k] = h
                wp = _pair(lo, jnp.exp(cols[0]), jnp.exp(cols[1]))
                yoff = jnp.dot(cg, h.astype(BF16), preferred_element_type=F32) * wp
                tot0, tot1 = _col(tot, e0), _col(tot, e1)
                up = _pair(lo, jnp.exp(tot0 - cols[0]), jnp.exp(tot1 - cols[1]))
                etot = _pair(lo, jnp.exp(tot0), jnp.exp(tot1))
                h_sc[k] = h * etot + lax.dot_general(bg, (xdt * up).astype(BF16), (((0,), (0,)), ((), ())),
                                                     preferred_element_type=F32)
                dskp = _pair(lo, _col(dskv, e0), _col(dskv, e1))
                y_ref[0, :, sl] = ydiag + yoff + xp * dskp

    p16 = pl.BlockSpec((1, 16), lambda b, c: (0, 0))
    return pl.pallas_call(
        body, name=name, grid=(bsz, nc),
        in_specs=[pl.BlockSpec((1, L, 1024), lambda b, c: (b, c, 0)),
                  pl.BlockSpec((1, L, 256), lambda b, c: (b, c, 4)),
                  pl.BlockSpec((1, L, 256), lambda b, c: (b, c, 5)),
                  pl.BlockSpec((1, L, 16), lambda b, c: (b, c, 0)), p16, p16, p16],
        out_specs=[pl.BlockSpec((1, L, 1024), lambda b, c: (b, c, 0)),
                   pl.BlockSpec((1, 1, SSD_PAIRS, 128, 128), lambda b, c: (b, c, 0, 0, 0))],
        out_shape=[jax.ShapeDtypeStruct((bsz, seq, SSD_INNER), F32),
                   jax.ShapeDtypeStruct((bsz, nc, SSD_PAIRS, 128, 128), F32)],
        scratch_shapes=[pltpu.VMEM((SSD_PAIRS, 128, 128), F32)],
        compiler_params=_cp(("parallel", "arbitrary")))(xbc, xbc, xbc, dt_raw, dtb, alog, dsk)


def ssd_bwd(xbc, dt_raw, dtb, alog, dsk, states, dy, name):
    bsz, seq, _ = xbc.shape
    L = SSD_CHUNK
    nc = seq // L

    def body(x_ref, b_ref, c_ref, dt_ref, dtb_ref, alog_ref, dsk_ref, st_ref, dy_ref,
             dxbc_ref, ddt_ref, gdtb_ref, galog_ref, gdsk_ref, dh_sc):
        first = jnp.logical_and(pl.program_id(0) == 0, pl.program_id(1) == 0)

        @pl.when(first)
        def _():
            gdtb_ref[...] = jnp.zeros(gdtb_ref.shape, F32)
            galog_ref[...] = jnp.zeros(galog_ref.shape, F32)
            gdsk_ref[...] = jnp.zeros(gdsk_ref.shape, F32)

        @pl.when(pl.program_id(1) == 0)
        def _():
            dh_sc[...] = jnp.zeros(dh_sc.shape, F32)

        raw, dt, a_neg, acs, acs_t, tot, r_i, c_i, tri_up = _ssd_chunk_prelude(dt_ref, dtb_ref, alog_ref)
        causal = r_i >= c_i
        causal_t = c_i >= r_i
        lo = lax.broadcasted_iota(jnp.int32, (1, 128), 1) < 64
        hi = jnp.logical_not(lo)
        lane16 = lax.broadcasted_iota(jnp.int32, (L, 16), 1)
        lane16r = lax.broadcasted_iota(jnp.int32, (1, 16), 1)
        last_row = lax.broadcasted_iota(jnp.int32, (L, 1), 0) == L - 1
        dskv = dsk_ref[...]
        ds16 = jnp.zeros((L, 16), F32)
        ddt16 = jnp.zeros((L, 16), F32)
        gdsk = jnp.zeros((1, 16), F32)

        def hsum(t, mask):
            return jnp.sum(jnp.where(mask, t, 0.0), axis=1, keepdims=True)

        for grp in range(2):
            gs = slice(grp * 128, (grp + 1) * 128)
            bg = b_ref[0, :, gs].astype(BF16)
            cg = c_ref[0, :, gs].astype(BF16)
            cb = lax.dot_general(cg, bg, (((1,), (1,)), ((), ())), preferred_element_type=F32)
            cbt = lax.dot_general(bg, cg, (((1,), (1,)), ((), ())), preferred_element_type=F32)
            dcb = jnp.zeros((L, L), F32)
            dcbt = jnp.zeros((L, L), F32)
            dc_g = jnp.zeros((L, 128), F32)
            db_g = jnp.zeros((L, 128), F32)
            for kk in range(4):
                k = grp * 4 + kk
                e0, e1 = 2 * k, 2 * k + 1
                sl = slice(k * 128, (k + 1) * 128)
                xp = x_ref[0, :, sl]
                dyp = dy_ref[0, :, sl]
                h = st_ref[0, 0, k]
                dh = dh_sc[k]
                cols = [_col(acs, e0), _col(acs, e1)]
                tots = [_col(tot, e0), _col(tot, e1)]
                dtp = _pair(lo, _col(dt, e0), _col(dt, e1))
                xdt = xp * dtp
                wp = _pair(lo, jnp.exp(cols[0]), jnp.exp(cols[1]))
                up = _pair(lo, jnp.exp(tots[0] - cols[0]), jnp.exp(tots[1] - cols[1]))
                etot = _pair(lo, jnp.exp(tots[0]), jnp.exp(tots[1]))
                hb = h.astype(BF16)
                dhb = dh.astype(BF16)
                yoff = jnp.dot(cg, hb, preferred_element_type=F32) * wp
                dyw = (dyp * wp).astype(BF16)
                dc_g = dc_g + lax.dot_general(dyw, hb, (((1,), (1,)), ((), ())), preferred_element_type=F32)
                dh_in = lax.dot_general(cg, dyw, (((0,), (0,)), ((), ())), preferred_element_type=F32) + dh * etot
                q_mat = jnp.dot(bg, dhb, preferred_element_type=F32)
                db_g = db_g + lax.dot_general((xdt * up).astype(BF16), dhb, (((1,), (1,)), ((), ())),
                                              preferred_element_type=F32)
                dxdt = up * q_mat
                t_u = q_mat * xdt * up
                t_w = dyp * yoff
                t_h = jnp.sum(dh * h, axis=0, keepdims=True) * etot
                for sub, e in enumerate((e0, e1)):
                    msk = lo if sub == 0 else hi
                    rowv = _rowvec(acs_t, e)
                    lam = jnp.exp(jnp.where(causal, cols[sub] - rowv, NEG))
                    lam_t = jnp.exp(jnp.where(causal_t, rowv - cols[sub], NEG))
                    xm = jnp.where(msk, xdt, 0.0).astype(BF16)
                    dym = jnp.where(msk, dyp, 0.0).astype(BF16)
                    g_mat = lax.dot_general(dym, xm, (((1,), (1,)), ((), ())), preferred_element_type=F32)
                    g_t = lax.dot_general(xm, dym, (((1,), (1,)), ((), ())), preferred_element_type=F32)
                    gl = g_mat * lam
                    glt = g_t * lam_t
                    dcb = dcb + gl
                    dcbt = dcbt + glt
                    e_r = jnp.sum(gl * cb, axis=1, keepdims=True)
                    e_c = jnp.sum(glt * cbt, axis=1, keepdims=True)
                    dxdt = dxdt + jnp.dot((cbt * lam_t).astype(BF16), dym, preferred_element_type=F32)
                    u_rows = hsum(t_u, msk)
                    ds_tot = jnp.sum(u_rows, axis=0, keepdims=True) + hsum(t_h, msk)
                    ds_col = e_r - e_c + hsum(t_w, msk) - u_rows + jnp.where(last_row, ds_tot, 0.0)
                    ds16 = ds16 + jnp.where(lane16 == e, ds_col, 0.0)
                dxp = dxdt * xp
                dyx = dyp * xp
                for sub, e in enumerate((e0, e1)):
                    msk = lo if sub == 0 else hi
                    ddt16 = ddt16 + jnp.where(lane16 == e, hsum(dxp, msk), 0.0)
                    gdsk = gdsk + jnp.where(lane16r == e, jnp.sum(hsum(dyx, msk), axis=0, keepdims=True), 0.0)
                dskp = _pair(lo, _col(dskv, e0), _col(dskv, e1))
                dxbc_ref[0, :, sl] = dxdt * dtp + dyp * dskp
                dh_sc[k] = dh_in
            dc_g = dc_g + jnp.dot(dcb.astype(BF16), bg, preferred_element_type=F32)
            db_g = db_g + jnp.dot(dcbt.astype(BF16), cg, preferred_element_type=F32)
            dxbc_ref[0, :, 1024 + grp * 128:1024 + (grp + 1) * 128] = db_g
            dxbc_ref[0, :, 1280 + grp * 128:1280 + (grp + 1) * 128] = dc_g
        da16 = jnp.dot(tri_up, ds16, precision=HI, preferred_element_type=F32)
        ddt16 = ddt16 + da16 * a_neg
        d_aneg = jnp.sum(da16 * dt, axis=0, keepdims=True)
        ddt_raw = ddt16 * _sigmoid(raw)
        ddt_ref[0] = ddt_raw
        gdtb_ref[...] += jnp.sum(ddt_raw, axis=0, keepdims=True)
        galog_ref[...] += d_aneg * a_neg
        gdsk_ref[...] += gdsk

    p16 = pl.BlockSpec((1, 16), lambda b, c: (0, 0))
    rev = lambda b, c: (b, nc - 1 - c, 0)
    return pl.pallas_call(
        body, name=name, grid=(bsz, nc),
        in_specs=[pl.BlockSpec((1, L, 1024), rev),
                  pl.BlockSpec((1, L, 256), lambda b, c: (b, nc - 1 - c, 4)),
                  pl.BlockSpec((1, L, 256), lambda b, c: (b, nc - 1 - c, 5)),
                  pl.BlockSpec((1, L, 16), rev), p16, p16, p16,
                  pl.BlockSpec((1, 1, SSD_PAIRS, 128, 128), lambda b, c: (b, nc - 1 - c, 0, 0, 0)),
                  pl.BlockSpec((1, L, 1024), rev)],
        out_specs=[pl.BlockSpec((1, L, CONV_DIM), rev), pl.BlockSpec((1, L, 16), rev), p16, p16, p16],
        out_shape=[jax.ShapeDtypeStruct((bsz, seq, CONV_DIM), F32), jax.ShapeDtypeStruct((bsz, seq, 16), F32),
                   jax.ShapeDtypeStruct((1, 16), F32), jax.ShapeDtypeStruct((1, 16), F32),
                   jax.ShapeDtypeStruct((1, 16), F32)],
        scratch_shapes=[pltpu.VMEM((SSD_PAIRS, 128, 128), F32)],
        compiler_params=_cp(("arbitrary", "arbitrary")))(xbc, xbc, xbc, dt_raw, dtb, alog, dsk, states, dy)


def rope(x3, cos_t, sin_a, sin_b, transpose, out_dtype, name, ts=512):
    bsz, seq, width = x3.shape
    heads = width // MLA_QK_PAD
    ts = min(ts, seq)

    def body(x_ref, c_ref, sa_ref, sb_ref, o_ref):
        c, sa, sb = c_ref[0], sa_ref[0], sb_ref[0]
        for h in range(heads):
            base = h * MLA_QK_PAD
            o_ref[0, :, base:base + 128] = x_ref[0, :, base:base + 128].astype(out_dtype)
            v = x_ref[0, :, base + 128:base + 256]
            if transpose:
                r = v * c + pltpu.roll(v * sa, 32, 1) + pltpu.roll(v * sb, 96, 1)
            else:
                r = v * c + pltpu.roll(v, 96, 1) * sa + pltpu.roll(v, 32, 1) * sb
            o_ref[0, :, base + 128:base + 256] = r.astype(out_dtype)

    tab = pl.BlockSpec((1, ts, 128), lambda b, i: (b, i, 0))
    return pl.pallas_call(
        body, name=name, grid=(bsz, seq // ts),
        in_specs=[pl.BlockSpec((1, ts, width), lambda b, i: (b, i, 0)), tab, tab, tab],
        out_specs=pl.BlockSpec((1, ts, width), lambda b, i: (b, i, 0)),
        out_shape=jax.ShapeDtypeStruct(x3.shape, out_dtype),
        compiler_params=_cp(("parallel", "parallel")))(x3, cos_t, sin_a, sin_b)


def _scores(q_ref, k_ref, scale, causal, i, j, tq, tk):
    s = lax.dot_general(q_ref[0].astype(BF16), k_ref[0].astype(BF16), (((1,), (1,)), ((), ())),
                        preferred_element_type=F32) * scale
    if not causal:
        return s, None
    r = i * tq + lax.broadcasted_iota(jnp.int32, (tq, tk), 0)
    c = j * tk + lax.broadcasted_iota(jnp.int32, (tq, tk), 1)
    mask = r >= c
    return jnp.where(mask, s, NEG), mask


def attn_fwd(q, k, v, *, heads, dk, dv, v_mul, v_off, causal, scale, tq, tk, name, rider=None):
    bsz, sq, _ = q.shape
    sk = k.shape[1]
    tq, tk = min(tq, sq), min(tk, sk)
    nq, nk = sq // tq, sk // tk

    def body(q_ref, k_ref, v_ref, o_ref, lse_ref, m_sc, l_sc, acc_sc):
        i, j = pl.program_id(2), pl.program_id(3)

        @pl.when(j == 0)
        def _():
            m_sc[...] = jnp.full(m_sc.shape, NEG, F32)
            l_sc[...] = jnp.zeros(l_sc.shape, F32)
            acc_sc[...] = jnp.zeros(acc_sc.shape, F32)

        def compute(masked=False):
            s, _ = _scores(q_ref, k_ref, scale, masked, i, j, tq, tk)
            m_prev = m_sc[...]
            m_new = jnp.maximum(m_prev, jnp.max(s, axis=1, keepdims=True))
            p = jnp.exp(s - m_new)
            alpha = jnp.exp(m_prev - m_new)
            l_sc[...] = alpha * l_sc[...] + jnp.sum(p, axis=1, keepdims=True)
            acc_sc[...] = alpha * acc_sc[...] + jnp.dot(p.astype(BF16), v_ref[0].astype(BF16),
                                                        preferred_element_type=F32)
            m_sc[...] = m_new

        if causal:
            pl.when(j < i)(compute)
            pl.when(j == i)(functools.partial(compute, True))
        else:
            compute()

        @pl.when(j == nk - 1)
        def _():
            o_ref[0] = acc_sc[...] / l_sc[...]
            lse_ref[0, 0] = m_sc[...] + jnp.log(l_sc[...])

    kvj = (lambda i, j: jnp.minimum(i, j)) if causal else (lambda i, j: j)
    grid = (bsz, heads, nq, nk)
    in_specs = [pl.BlockSpec((1, tq, dk), lambda b, h, i, j: (b, i, h)),
                pl.BlockSpec((1, tk, dk), lambda b, h, i, j: (b, kvj(i, j), h)),
                pl.BlockSpec((1, tk, dv), lambda b, h, i, j: (b, kvj(i, j), v_mul * h + v_off))]
    out_specs = [pl.BlockSpec((1, tq, dv), lambda b, h, i, j: (b, i, h)),
                 pl.BlockSpec((1, 1, tq, 1), lambda b, h, i, j: (b, h, i, 0))]
    out_shape = [jax.ShapeDtypeStruct((bsz, sq, heads * dv), F32), jax.ShapeDtypeStruct((bsz, heads, sq, 1), F32)]
    scratch = [pltpu.VMEM((tq, 1), F32), pltpu.VMEM((tq, 1), F32), pltpu.VMEM((tq, dv), F32)]
    args, sem, io_aliases = [q, k, v], ("parallel", "parallel", "parallel", "arbitrary"), {}
    if rider is not None:
        body, in_specs, out_specs, out_shape, scratch, args, io_aliases = _attach(
            rider, body, grid, in_specs, out_specs, out_shape, scratch, args)
        sem = ("arbitrary",) * 4
    outs = pl.pallas_call(body, name=name, grid=grid, in_specs=in_specs, out_specs=out_specs, out_shape=out_shape,
                          scratch_shapes=scratch, input_output_aliases=io_aliases, compiler_params=_cp(sem))(*args)
    return outs[0], outs[1], outs[2:]


def _attn_grads(q_ref, k_ref, v_ref, do_ref, o_ref, lse_ref, scale, causal, i, j, tq, tk):
    s, mask = _scores(q_ref, k_ref, scale, causal, i, j, tq, tk)
    p = jnp.exp(s - lse_ref[0, 0])
    if mask is not None:
        p = jnp.where(mask, p, 0.0)
    do = do_ref[0]
    dp = lax.dot_general(do.astype(BF16), v_ref[0].astype(BF16), (((1,), (1,)), ((), ())),
                         preferred_element_type=F32)
    delta = jnp.sum(do * o_ref[0], axis=1, keepdims=True)
    ds = p * (dp - delta) * scale
    return p, ds


def attn_bwd(q, k, v, do, do_off, o, lse, *, heads, dk, dv, v_mul, v_off, causal, scale, tq, tk, name,
             rider_dq=None, rider_dkv=None):
    bsz, sq, _ = q.shape
    sk = k.shape[1]
    tq, tk = min(tq, sq), min(tk, sk)
    nq, nk = sq // tq, sk // tk

    def dq_body(q_ref, k_ref, v_ref, do_ref, o_ref, lse_ref, dq_ref, acc_sc):
        i, j = pl.program_id(2), pl.program_id(3)

        @pl.when(j == 0)
        def _():
            acc_sc[...] = jnp.zeros(acc_sc.shape, F32)

        def compute(masked=False):
            _, ds = _attn_grads(q_ref, k_ref, v_ref, do_ref, o_ref, lse_ref, scale, masked, i, j, tq, tk)
            acc_sc[...] += jnp.dot(ds.astype(BF16), k_ref[0].astype(BF16), preferred_element_type=F32)

        if causal:
            pl.when(j < i)(compute)
            pl.when(j == i)(functools.partial(compute, True))
        else:
            compute()

        @pl.when(j == nk - 1)
        def _():
            dq_ref[0] = acc_sc[...]

    kvj = (lambda i, j: jnp.minimum(i, j)) if causal else (lambda i, j: j)
    grid = (bsz, heads, nq, nk)
    in_specs = [pl.BlockSpec((1, tq, dk), lambda b, h, i, j: (b, i, h)),
                pl.BlockSpec((1, tk, dk), lambda b, h, i, j: (b, kvj(i, j), h)),
                pl.BlockSpec((1, tk, dv), lambda b, h, i, j: (b, kvj(i, j), v_mul * h + v_off)),
                pl.BlockSpec((1, tq, dv), lambda b, h, i, j: (b, i, h + do_off)),
                pl.BlockSpec((1, tq, dv), lambda b, h, i, j: (b, i, h)),
                pl.BlockSpec((1, 1, tq, 1), lambda b, h, i, j: (b, h, i, 0))]
    out_specs = [pl.BlockSpec((1, tq, dk), lambda b, h, i, j: (b, i, h))]
    out_shape = [jax.ShapeDtypeStruct((bsz, sq, heads * dk), F32)]
    scratch, args = [pltpu.VMEM((tq, dk), F32)], [q, k, v, do, o, lse]
    sem, io_aliases = ("parallel", "parallel", "parallel", "arbitrary"), {}
    if rider_dq is not None:
        dq_body, in_specs, out_specs, out_shape, scratch, args, io_aliases = _attach(
            rider_dq, dq_body, grid, in_specs, out_specs, out_shape, scratch, args)
        sem = ("arbitrary",) * 4
    dq_outs = pl.pallas_call(dq_body, name=name + "_dq", grid=grid, in_specs=in_specs, out_specs=out_specs,
                             out_shape=out_shape, scratch_shapes=scratch, input_output_aliases=io_aliases,
                             compiler_params=_cp(sem))(*args)
    dq = dq_outs[0]

    def dkv_body(q_ref, k_ref, v_ref, do_ref, o_ref, lse_ref, dk_ref, dv_ref, dk_sc, dv_sc):
        j, i = pl.program_id(2), pl.program_id(3)

        @pl.when(i == 0)
        def _():
            dk_sc[...] = jnp.zeros(dk_sc.shape, F32)
            dv_sc[...] = jnp.zeros(dv_sc.shape, F32)

        def compute(masked=False):
            p, ds = _attn_grads(q_ref, k_ref, v_ref, do_ref, o_ref, lse_ref, scale, masked, i, j, tq, tk)
            dv_sc[...] += lax.dot_general(p.astype(BF16), do_ref[0].astype(BF16), (((0,), (0,)), ((), ())),
                                          preferred_element_type=F32)
            dk_sc[...] += lax.dot_general(ds.astype(BF16), q_ref[0].astype(BF16), (((0,), (0,)), ((), ())),
                                          preferred_element_type=F32)

        if causal:
            pl.when(i > j)(compute)
            pl.when(i == j)(functools.partial(compute, True))
        else:
            compute()

        @pl.when(i == nq - 1)
        def _():
            dk_ref[0] = dk_sc[...]
            dv_ref[0] = dv_sc[...]

    qi = (lambda j, i: jnp.maximum(i, j)) if causal else (lambda j, i: i)
    grid = (bsz, heads, nk, nq)
    in_specs = [pl.BlockSpec((1, tq, dk), lambda b, h, j, i: (b, qi(j, i), h)),
                pl.BlockSpec((1, tk, dk), lambda b, h, j, i: (b, j, h)),
                pl.BlockSpec((1, tk, dv), lambda b, h, j, i: (b, j, v_mul * h + v_off)),
                pl.BlockSpec((1, tq, dv), lambda b, h, j, i: (b, qi(j, i), h + do_off)),
                pl.BlockSpec((1, tq, dv), lambda b, h, j, i: (b, qi(j, i), h)),
                pl.BlockSpec((1, 1, tq, 1), lambda b, h, j, i: (b, h, qi(j, i), 0))]
    out_specs = [pl.BlockSpec((1, tk, dk), lambda b, h, j, i: (b, j, h)),
                 pl.BlockSpec((1, tk, dv), lambda b, h, j, i: (b, j, h))]
    out_shape = [jax.ShapeDtypeStruct((bsz, sk, heads * dk), F32), jax.ShapeDtypeStruct((bsz, sk, heads * dv), F32)]
    scratch, args = [pltpu.VMEM((tk, dk), F32), pltpu.VMEM((tk, dv), F32)], [q, k, v, do, o, lse]
    sem, io_aliases = ("parallel", "parallel", "parallel", "arbitrary"), {}
    if rider_dkv is not None:
        dkv_body, in_specs, out_specs, out_shape, scratch, args, io_aliases = _attach(
            rider_dkv, dkv_body, grid, in_specs, out_specs, out_shape, scratch, args)
        sem = ("arbitrary",) * 4
    dkv_outs = pl.pallas_call(dkv_body, name=name + "_dkv", grid=grid, in_specs=in_specs, out_specs=out_specs,
                              out_shape=out_shape, scratch_shapes=scratch, input_output_aliases=io_aliases,
                              compiler_params=_cp(sem))(*args)
    return dq, dkv_outs[0], dkv_outs[1], dq_outs[1:], dkv_outs[2:]


def loss_head(y, target, name, tr=512):
    rows, width = y.shape
    tr = min(tr, rows)

    def body(y_ref, t_ref, dy_ref, l_ref):
        @pl.when(pl.program_id(0) == 0)
        def _():
            l_ref[...] = jnp.zeros(l_ref.shape, F32)

        err = y_ref[...] - t_ref[...]
        dy_ref[...] = err * (1.0 / width)
        l_ref[...] += jnp.sum(jnp.sum(err * err, axis=1, keepdims=True), axis=0, keepdims=True) * (0.5 / width)

    return pl.pallas_call(
        body, name=name, grid=(rows // tr,),
        in_specs=[pl.BlockSpec((tr, width), lambda i: (i, 0)), pl.BlockSpec((tr, width), lambda i: (i, 0))],
        out_specs=[pl.BlockSpec((tr, width), lambda i: (i, 0)), pl.BlockSpec((1, 1), lambda i: (0, 0))],
        out_shape=[jax.ShapeDtypeStruct((rows, width), F32), jax.ShapeDtypeStruct((1, 1), F32)],
        compiler_params=_cp(("arbitrary",)))(y, target)


def _mesh_pos():
    return lax.axis_index("x"), lax.axis_index("y"), lax.axis_index("c")


def all_gather(blocks, name):
    n_arr = len(blocks)

    def body(*refs):
        x_refs, out_refs = refs[:n_arr], refs[n_arr:2 * n_arr]
        send_sems, recv_sems, local_sems = refs[2 * n_arr:]
        x, y, c = _mesh_pos()
        me, sibling = (x, y, c), (x, y, 1 - c)
        chips = [(1 - x, y), (x, 1 - y), (1 - x, 1 - y)]

        def slot(a, px, py, pc):
            return out_refs[a].at[4 * px + 2 * py + pc]

        def copy(a, k, blk, to, src=None):
            return pltpu.make_async_remote_copy(
                src_ref=slot(a, *blk) if src is None else src, dst_ref=slot(a, *blk),
                send_sem=send_sems.at[7 * a + k], recv_sem=recv_sems.at[7 * a + k],
                device_id=to, device_id_type=pl.DeviceIdType.MESH)

        mine = [pltpu.make_async_copy(x_refs[a], slot(a, *me), local_sems.at[a]) for a in range(n_arr)]
        for cp in mine:
            cp.start()
        first = []
        for a in range(n_arr):
            first.append(copy(a, 0, me, sibling, src=x_refs[a]))
            first += [copy(a, 1 + j, me, (*chip, c), src=x_refs[a]) for j, chip in enumerate(chips)]
        for cp in first:
            cp.start()
        passed = []
        for j, chip in enumerate(chips):
            for a in range(n_arr):
                copy(a, 1 + j, (*chip, c), me).wait_recv()
                passed.append(copy(a, 4 + j, (*chip, c), sibling))
                passed[-1].start()
        for a in range(n_arr):
            copy(a, 0, sibling, me).wait_recv()
            for j, chip in enumerate(chips):
                copy(a, 4 + j, (*chip, 1 - c), me).wait_recv()
        for cp in first + passed:
            cp.wait_send()
        for cp in mine:
            cp.wait()

    return pl.pallas_call(
        body, name=name,
        out_shape=[jax.ShapeDtypeStruct((N_DEV,) + b.shape, b.dtype) for b in blocks],
        in_specs=[pl.BlockSpec(memory_space=pl.ANY)] * n_arr,
        out_specs=[pl.BlockSpec(memory_space=pl.ANY)] * n_arr,
        scratch_shapes=[pltpu.SemaphoreType.DMA((7 * n_arr,)), pltpu.SemaphoreType.DMA((7 * n_arr,)),
                        pltpu.SemaphoreType.DMA((n_arr,))],
    )(*blocks)


def _slot(px, py, pc):
    return 4 * px + 2 * py + pc


def gather_near_rider(blocks):
    n_arr = len(blocks)

    def parts(rins, routs, sems):
        send_sems, recv_sems, local_sems = sems
        x, y, c = _mesh_pos()
        peers = [(x, y, 1 - c), (1 - x, y, c), (x, 1 - y, c)]
        local = [pltpu.make_async_copy(rins[a], routs[a].at[_slot(x, y, c)], local_sems.at[a]) for a in range(n_arr)]
        sends = [pltpu.make_async_remote_copy(
            src_ref=rins[a], dst_ref=routs[a].at[_slot(x, y, c)], send_sem=send_sems.at[3 * a + k],
            recv_sem=recv_sems.at[3 * a + k], device_id=p, device_id_type=pl.DeviceIdType.MESH)
            for a in range(n_arr) for k, p in enumerate(peers)]
        recvs = [pltpu.make_async_remote_copy(
            src_ref=rins[a], dst_ref=routs[a].at[_slot(*p)], send_sem=send_sems.at[3 * a + k],
            recv_sem=recv_sems.at[3 * a + k], device_id=p, device_id_type=pl.DeviceIdType.MESH)
            for a in range(n_arr) for k, p in enumerate(peers)]
        return local, sends, recvs

    def start(rins, routs, sems):
        local, sends, _ = parts(rins, routs, sems)
        for cp in local + sends:
            cp.start()

    def finish(rins, routs, sems):
        local, sends, recvs = parts(rins, routs, sems)
        for cp in recvs:
            cp.wait_recv()
        for cp in sends:
            cp.wait_send()
        for cp in local:
            cp.wait()

    return Rider(blocks, [jax.ShapeDtypeStruct((N_DEV,) + b.shape, b.dtype) for b in blocks],
                 [pltpu.SemaphoreType.DMA((3 * n_arr,)), pltpu.SemaphoreType.DMA((3 * n_arr,)),
                  pltpu.SemaphoreType.DMA((n_arr,))], start, finish)


def gather_diagonal_rider(blocks, gathered):
    n_arr = len(blocks)

    def parts(rins, routs, sems):
        send_sems, recv_sems = sems
        x, y, c = _mesh_pos()
        p = (1 - x, 1 - y, c)
        sends = [pltpu.make_async_remote_copy(
            src_ref=rins[a], dst_ref=routs[a].at[_slot(x, y, c)], send_sem=send_sems.at[a],
            recv_sem=recv_sems.at[a], device_id=p, device_id_type=pl.DeviceIdType.MESH) for a in range(n_arr)]
        recvs = [pltpu.make_async_remote_copy(
            src_ref=rins[a], dst_ref=routs[a].at[_slot(*p)], send_sem=send_sems.at[a],
            recv_sem=recv_sems.at[a], device_id=p, device_id_type=pl.DeviceIdType.MESH) for a in range(n_arr)]
        return sends, recvs

    def start(rins, routs, sems):
        for cp in parts(rins, routs, sems)[0]:
            cp.start()

    def finish(rins, routs, sems):
        sends, recvs = parts(rins, routs, sems)
        for cp in recvs:
            cp.wait_recv()
        for cp in sends:
            cp.wait_send()

    return Rider(list(blocks) + list(gathered), [jax.ShapeDtypeStruct(g.shape, g.dtype) for g in gathered],
                 [pltpu.SemaphoreType.DMA((n_arr,)), pltpu.SemaphoreType.DMA((n_arr,))], start, finish,
                 aliases={n_arr + a: a for a in range(n_arr)})


def gather_forward_rider(gathered):
    n_arr = len(gathered)

    def parts(rins, routs, sems):
        send_sems, recv_sems = sems
        x, y, c = _mesh_pos()
        chips = [(1 - x, y), (x, 1 - y), (1 - x, 1 - y)]
        sends = [pltpu.make_async_remote_copy(
            src_ref=routs[a].at[_slot(px, py, c)], dst_ref=routs[a].at[_slot(px, py, c)],
            send_sem=send_sems.at[3 * a + k], recv_sem=recv_sems.at[3 * a + k],
            device_id=(x, y, 1 - c), device_id_type=pl.DeviceIdType.MESH)
            for a in range(n_arr) for k, (px, py) in enumerate(chips)]
        recvs = [pltpu.make_async_remote_copy(
            src_ref=routs[a].at[_slot(px, py, c)], dst_ref=routs[a].at[_slot(px, py, 1 - c)],
            send_sem=send_sems.at[3 * a + k], recv_sem=recv_sems.at[3 * a + k],
            device_id=(x, y, 1 - c), device_id_type=pl.DeviceIdType.MESH)
            for a in range(n_arr) for k, (px, py) in enumerate(chips)]
        return sends, recvs

    def start(rins, routs, sems):
        for cp in parts(rins, routs, sems)[0]:
            cp.start()

    def finish(rins, routs, sems):
        sends, recvs = parts(rins, routs, sems)
        for cp in recvs:
            cp.wait_recv()
        for cp in sends:
            cp.wait_send()

    return Rider(gathered, [jax.ShapeDtypeStruct(g.shape, g.dtype) for g in gathered],
                 [pltpu.SemaphoreType.DMA((3 * n_arr,)), pltpu.SemaphoreType.DMA((3 * n_arr,))], start, finish,
                 aliases={a: a for a in range(n_arr)})


def sibling_stage_rider(blobs):
    n_arr = len(blobs)

    def copies(rins, routs, sems):
        send_sems, recv_sems = sems
        x, y, c = _mesh_pos()
        return [pltpu.make_async_remote_copy(
            src_ref=rins[a].at[2 * j + (1 - c)], dst_ref=routs[a].at[j],
            send_sem=send_sems.at[4 * a + j], recv_sem=recv_sems.at[4 * a + j],
            device_id=(x, y, 1 - c), device_id_type=pl.DeviceIdType.MESH)
            for a in range(n_arr) for j in range(4)]

    def start(rins, routs, sems):
        for cp in copies(rins, routs, sems):
            cp.start()

    def finish(rins, routs, sems):
        for cp in copies(rins, routs, sems):
            cp.wait()

    return Rider(blobs, [jax.ShapeDtypeStruct((4,) + b.shape[1:], b.dtype) for b in blobs],
                 [pltpu.SemaphoreType.DMA((4 * n_arr,)), pltpu.SemaphoreType.DMA((4 * n_arr,))], start, finish)


def chip_partial(blob, from_sibling, name, tr):
    _, rows, cols = blob.shape
    c_idx = lax.axis_index("c").astype(jnp.int32).reshape(1)

    def body(c_ref, b_ref, s_ref, o_ref):
        o_ref[...] = (b_ref[...].astype(F32) + s_ref[...].astype(F32)).astype(o_ref.dtype)

    return pl.pallas_call(
        body, name=name,
        grid_spec=pltpu.PrefetchScalarGridSpec(
            num_scalar_prefetch=1, grid=(4, rows // tr),
            in_specs=[pl.BlockSpec((1, tr, cols), lambda j, i, c_ref: (2 * j + c_ref[0], i, 0)),
                      pl.BlockSpec((1, tr, cols), lambda j, i, c_ref: (j, i, 0))],
            out_specs=pl.BlockSpec((1, tr, cols), lambda j, i, c_ref: (j, i, 0))),
        out_shape=jax.ShapeDtypeStruct((4, rows, cols), blob.dtype),
        compiler_params=_cp(("parallel", "parallel")))(c_idx, blob, from_sibling)


def chip_stage_rider(partials, diagonal):
    n_arr = len(partials)
    n_remote = 1 if diagonal else 2

    def copies(rins, routs, sems):
        send_sems, recv_sems = sems[0], sems[1]
        x, y, c = _mesh_pos()
        chips = [(1 - x, 1 - y)] if diagonal else [(1 - x, y), (x, 1 - y)]
        return [pltpu.make_async_remote_copy(
            src_ref=rins[a].at[2 * px + py], dst_ref=routs[a].at[k if diagonal else 1 + k],
            send_sem=send_sems.at[n_remote * a + k], recv_sem=recv_sems.at[n_remote * a + k],
            device_id=(px, py, c), device_id_type=pl.DeviceIdType.MESH)
            for a in range(n_arr) for k, (px, py) in enumerate(chips)]

    def local(rins, routs, sems):
        x, y, _ = _mesh_pos()
        return [pltpu.make_async_copy(rins[a].at[2 * x + y], routs[a].at[0], sems[2].at[a]) for a in range(n_arr)]

    def start(rins, routs, sems):
        for cp in copies(rins, routs, sems) + ([] if diagonal else local(rins, routs, sems)):
            cp.start()

    def finish(rins, routs, sems):
        for cp in copies(rins, routs, sems) + ([] if diagonal else local(rins, routs, sems)):
            cp.wait()

    sem_list = [pltpu.SemaphoreType.DMA((n_remote * n_arr,)), pltpu.SemaphoreType.DMA((n_remote * n_arr,))]
    if not diagonal:
        sem_list.append(pltpu.SemaphoreType.DMA((n_arr,)))
    return Rider(partials, [jax.ShapeDtypeStruct((1 if diagonal else 3,) + p.shape[1:], p.dtype) for p in partials],
                 sem_list, start, finish)


def adamw(parts, w, m, v, name, tr=128):
    rows, cols = w.shape
    tr = min(tr, rows)
    n_arr = len(parts)
    c1 = 1.0 - ADAM_B1 ** ADAM_STEP
    c2 = 1.0 - ADAM_B2 ** ADAM_STEP

    def body(*refs):
        p_refs = refs[:n_arr]
        w_ref, m_ref, v_ref, g_ref, d_ref, nm_ref, nv_ref = refs[n_arr:]
        g = None
        for p_ref in p_refs:
            for s in range(p_ref.shape[0]):
                term = p_ref[s].astype(F32)
                g = term if g is None else g + term
        nm = ADAM_B1 * m_ref[...] + (1.0 - ADAM_B1) * g
        nv = ADAM_B2 * v_ref[...] + (1.0 - ADAM_B2) * (g * g)
        m_hat = nm / c1
        v_hat = nv / c2
        g_ref[...] = g
        d_ref[...] = -ADAM_LR * (m_hat / (jnp.sqrt(v_hat) + ADAM_EPS) + ADAM_WD * w_ref[...])
        nm_ref[...] = nm
        nv_ref[...] = nv

    blk = pl.BlockSpec((tr, cols), lambda i: (i, 0))
    return pl.pallas_call(
        body, name=name, grid=(rows // tr,),
        in_specs=[pl.BlockSpec((p.shape[0], tr, cols), lambda i: (0, i, 0)) for p in parts] + [blk, blk, blk],
        out_specs=[blk, blk, blk, blk],
        out_shape=[jax.ShapeDtypeStruct((rows, cols), F32)] * 4,
        compiler_params=_cp(("parallel",)))(*parts, w, m, v)


def _pack_local(fam, shards, dtype):
    parts = []
    for n, rows, prows, transposed in fam:
        a = shards[n].T if transposed else shards[n]
        parts.append(jnp.pad(a.astype(dtype), ((0, prows - rows), (0, 0))))
    return jnp.concatenate(parts, axis=0)


def _unpack_gathered(fam, gathered):
    out, off = {}, 0
    for n, rows, prows, _ in fam:
        out[n] = gathered[:, off:off + rows].reshape(N_DEV * rows, gathered.shape[2])
        off += prows
    return out


def _pack_full_grads(fam, grads):
    segs = []
    for n, rows, prows, _ in fam:
        g = grads[n].astype(BF16)
        g = g.reshape(N_DEV, rows, g.shape[1])
        segs.append(jnp.pad(g, ((0, 0), (0, prows - rows), (0, 0))))
    return jnp.concatenate(segs, axis=1)


def _unpack_local(fam, blob):
    out, off = {}, 0
    for n, rows, prows, transposed in fam:
        a = blob[off:off + rows]
        out[n] = a.T if transposed else a
        off += prows
    return out


def _pack_small(vals):
    flat = jnp.concatenate([vals[n].reshape(-1).astype(F32) for n, _ in SMALL])
    flat = jnp.pad(flat, (0, SMALL_ROWS * 128 - flat.shape[0]))
    return flat.reshape(SMALL_ROWS, 128)


def _unpack_small(flat2d):
    flat = flat2d.reshape(-1)
    out, off = {}, 0
    for n, k in SMALL:
        out[n] = flat[off:off + DEPTH * k].reshape(DEPTH, k)
        off += DEPTH * k
    return out


def _pad_w_in_t(w_t):
    return jnp.concatenate([w_t[:2560], w_t[2576:IN_COLS], w_t[2560:2576],
                            jnp.zeros((IN_PAD - IN_COLS, w_t.shape[1]), w_t.dtype)], axis=0)


def _unpad_w_in_t(g_t):
    return jnp.concatenate([g_t[:2560], g_t[3648:3664], g_t[2560:3648]], axis=0)


def _pad_head_rows(w_t):
    k = w_t.shape[1]
    v = jnp.pad(w_t.reshape(MLA_HEADS, MLA_QK, k), ((0, 0), (0, MLA_QK_PAD - MLA_QK), (0, 0)))
    return v.reshape(MLA_HEADS * MLA_QK_PAD, k)


def _unpad_head_rows(g_t):
    k = g_t.shape[1]
    return g_t.reshape(MLA_HEADS, MLA_QK_PAD, k)[:, :MLA_QK].reshape(MLA_HEADS * MLA_QK, k)


def _row(v):
    return v.reshape(1, -1).astype(F32)


def layer_fwd(x, memf, w, p, tabs, bsz, next_blocks):
    t_rows = x.shape[0]
    seq = t_rows // bsz
    r = {}
    r["x"] = x
    h1 = rms_fwd(x, 0, D_MODEL, _row(p["attn_norm_g"]), 1, D_MODEL, BF16, "norm1_fwd")
    proj = mm([(h1, w["w_in"])], "nt", tn=768, name="in_proj")
    r["h1"], r["proj"] = h1, proj
    proj3 = proj.reshape(bsz, seq, IN_PAD)

    w8 = jnp.concatenate([p["conv_w_full"], p["conv_b"].reshape(1, -1), jnp.zeros((3, CONV_DIM), F32)], 0)
    xbc = conv_fwd(proj3, w8, "conv_fwd")
    tail = proj[:, TAIL_COL:TAIL_COL + 128]
    dt_raw = tail[:, 64:80].reshape(bsz, seq, 16)
    y, states = ssd_fwd(xbc, dt_raw, _row(p["dt_bias"]), _row(p["a_log"]), _row(p["d_skip"]), "ssd_fwd")
    y2 = y.reshape(t_rows, SSD_INNER)
    yg = ew(_gate_fwd, [(y2, 0), (proj, 0)], SSD_INNER, (F32,), "ssd_gate_fwd")
    y_ssd = rms_fwd(yg, 0, SSD_INNER, _row(p["ssd_norm_g"]), 2, 512, BF16, "ssd_norm_fwd")
    r.update(w8=w8, xbc=xbc, dt_raw=dt_raw, states=states, y=y2, yg=yg)

    qn = rms_fwd(proj, 5, Q_LORA, _row(p["q_a_norm_g"]), 1, Q_LORA, BF16, "qa_norm_fwd")
    kvn = rms_fwd(proj, 6, Q_LORA, _row(p["kv_a_norm_g"]), 1, Q_LORA, BF16, "kva_norm_fwd")
    q_raw = mm([(qn, w["w_q_b"])], "nt", name="q_b_proj")
    kv = mm([(kvn, w["w_kv_b"])], "nt", name="kv_b_proj")
    lane = jnp.arange(128) < 64
    krope = jnp.where(lane[None, :], tail, 0.0)
    k_raw = jnp.concatenate([kv.reshape(t_rows, MLA_HEADS, 256)[:, :, :128],
                             jnp.broadcast_to(krope[:, None, :], (t_rows, MLA_HEADS, 128))], axis=-1)
    k_raw = k_raw.reshape(t_rows, MLA_HEADS * MLA_QK_PAD)
    gq = _row(jnp.tile(jnp.pad(p["mla_q_norm_g"], (0, 64)), MLA_HEADS))
    gk = _row(jnp.tile(jnp.pad(p["mla_k_norm_g"], (0, 64)), MLA_HEADS))
    qh = rms_fwd(q_raw, 0, 2048, gq, MLA_HEADS, MLA_QK, F32, "q_head_norm_fwd")
    kh = rms_fwd(k_raw, 0, 2048, gk, MLA_HEADS, MLA_QK, F32, "k_head_norm_fwd")
    q_fin = rope(qh.reshape(bsz, seq, 2048), *tabs, False, BF16, "rope_fwd")
    k_fin = rope(kh.reshape(bsz, seq, 2048), *tabs, False, BF16, "rope_fwd")
    kv3 = kv.reshape(bsz, seq, 2048)
    riding = next_blocks is not None
    o, lse, gathered = attn_fwd(q_fin, k_fin, kv3, heads=MLA_HEADS, dk=MLA_QK_PAD, dv=MLA_V, v_mul=2, v_off=1,
                                causal=True, scale=MLA_QK ** -0.5, tq=512, tk=512,
                                name="mla_attn_fwd_gather_near" if riding else "mla_attn_fwd",
                                rider=gather_near_rider(next_blocks) if riding else None)
    r.update(qn=qn, kvn=kvn, q_raw=q_raw, k_raw=k_raw, gq=gq, gk=gk, q_fin=q_fin, k_fin=k_fin, kv3=kv3, o=o, lse=lse)
    mixed_in = jnp.concatenate([y_ssd, o.reshape(t_rows, SSD_INNER).astype(BF16)], axis=1)
    x2 = mm([(mixed_in, w["w_out"])], "nn", extras=(x,), epilogue=_add_res, name="out_proj")
    r.update(mixed_in=mixed_in, x2=x2)

    h2 = rms_fwd(x2, 0, D_MODEL, _row(p["xattn_norm_g"]), 1, D_MODEL, BF16, "norm2_fwd")
    mn = rms_fwd(memf, 0, D_MODEL, _row(p["mem_norm_g"]), 1, D_MODEL, BF16, "mem_norm_fwd")
    xq_raw = mm([(h2, w["w_xq"])], "nn", name="xq_proj")
    xk_raw = mm([(mn, w["w_xk"])], "nn", name="xk_proj")
    xv = mm([(mn, w["w_xv"])], "nn", name="xv_proj")
    gxq = _row(jnp.tile(p["xq_norm_g"], X_HEADS))
    gxk = _row(jnp.tile(p["xk_norm_g"], X_HEADS))
    xq = rms_fwd(xq_raw, 0, X_INNER, gxq, X_HEADS, X_HEAD_DIM, BF16, "xq_norm_fwd")
    xk = rms_fwd(xk_raw, 0, X_INNER, gxk, X_HEADS, X_HEAD_DIM, BF16, "xk_norm_fwd")
    mlen = memf.shape[0] // bsz
    xq3, xk3, xv3 = xq.reshape(bsz, seq, X_INNER), xk.reshape(bsz, mlen, X_INNER), xv.reshape(bsz, mlen, X_INNER)
    xo, xlse, _ = attn_fwd(xq3, xk3, xv3, heads=X_HEADS, dk=X_HEAD_DIM, dv=X_HEAD_DIM, v_mul=1, v_off=0,
                           causal=False, scale=X_HEAD_DIM ** -0.5, tq=512, tk=256, name="xattn_fwd")
    xo2 = xo.reshape(t_rows, X_INNER)
    x3 = mm([(xo2, w["w_xo"])], "nt", extras=(x2,), epilogue=_add_res, name="xo_proj")
    r.update(h2=h2, mn=mn, xq_raw=xq_raw, xk_raw=xk_raw, gxq=gxq, gxk=gxk, xq3=xq3, xk3=xk3, xv3=xv3,
             xo=xo, xlse=xlse, x3=x3)

    h3 = rms_fwd(x3, 0, D_MODEL, _row(p["ffn_norm_g"]), 1, D_MODEL, BF16, "norm3_fwd")
    if riding:
        (gate, up, act), gathered = mm(
            [(h3, w["w_gate"]), (h3, w["w_up"])], "nt", epilogue=_swiglu_fwd, separate=True,
            out_dtypes=(BF16, BF16, BF16), name="gate_up_proj_gather_diagonal",
            rider=gather_diagonal_rider(next_blocks, gathered))
        x4, gathered = mm([(act, w["w_down"])], "nn", extras=(x3,), epilogue=_add_res, tm=512,
                          name="down_proj_gather_forward", rider=gather_forward_rider(gathered))
    else:
        gate, up, act = mm([(h3, w["w_gate"]), (h3, w["w_up"])], "nt", epilogue=_swiglu_fwd, separate=True,
                           out_dtypes=(BF16, BF16, BF16), name="gate_up_proj")
        x4 = mm([(act, w["w_down"])], "nn", extras=(x3,), epilogue=_add_res, tm=512, name="down_proj")
        gathered = None
    r.update(h3=h3, gate=gate, up=up, act=act)
    return x4, r, gathered


def layer_bwd(dx4, r, memf, w, p, tabs, bsz, pending):
    t_rows = dx4.shape[0]
    seq = t_rows // bsz
    big, small = {}, {}
    rider_xy = rider_diag = None

    if pending is None:
        dgate, dup = mm([(dx4, w["w_down"])], "nt", extras=(r["gate"], r["up"]), epilogue=_swiglu_bwd,
                        out_dtypes=(BF16, BF16), name="down_proj_bwd")
    else:
        (dgate, dup), from_sib = mm([(dx4, w["w_down"])], "nt", extras=(r["gate"], r["up"]), epilogue=_swiglu_bwd,
                                    out_dtypes=(BF16, BF16), name="down_proj_bwd_sibling_exchange",
                                    rider=sibling_stage_rider(pending))
        partials = [chip_partial(b, s, "grad_chip_partial", tr) for b, s, tr in zip(pending, from_sib, PARTIAL_TR)]
        rider_xy, rider_diag = chip_stage_rider(partials, False), chip_stage_rider(partials, True)
    big["w_down"] = mm([(r["act"], dx4)], "tn", out_dtypes=(BF16,), name="down_proj_wgrad")
    big["w_gate"] = mm([(dgate, r["h3"])], "tn", out_dtypes=(BF16,), tm=512, tn=1024, name="gate_proj_wgrad")
    big["w_up"] = mm([(dup, r["h3"])], "tn", out_dtypes=(BF16,), tm=512, tn=1024, name="gate_proj_wgrad")
    dh3 = mm([(dgate, w["w_gate"]), (dup, w["w_up"])], "nn", tm=512, tn=256, name="gate_up_proj_bwd")
    dx3, g = rms_bwd(r["x3"], 0, D_MODEL, _row(p["ffn_norm_g"]), 1, D_MODEL, dh3, dx4, "norm3_bwd")
    small["ffn_norm_g"] = g[0]

    xo2 = r["xo"].reshape(t_rows, X_INNER)
    dxo = mm([(dx3, w["w_xo"])], "nn", name="xo_proj_bwd")
    big["w_xo"] = mm([(dx3, xo2)], "tn", out_dtypes=(BF16,), tm=512, name="xo_proj_wgrad")
    dxq, dxk, dxv, _, _ = attn_bwd(r["xq3"], r["xk3"], r["xv3"], dxo.reshape(bsz, seq, X_INNER), 0, r["xo"],
                                   r["xlse"], heads=X_HEADS, dk=X_HEAD_DIM, dv=X_HEAD_DIM, v_mul=1, v_off=0,
                                   causal=False, scale=X_HEAD_DIM ** -0.5, tq=512, tk=256, name="xattn_bwd")
    mrows = memf.shape[0]
    dxq_raw, g = rms_bwd(r["xq_raw"], 0, X_INNER, r["gxq"], X_HEADS, X_HEAD_DIM, dxq.reshape(t_rows, X_INNER),
                         None, "xq_norm_bwd")
    small["xq_norm_g"] = g.reshape(X_HEADS, X_HEAD_DIM).sum(0)
    dxk_raw, g = rms_bwd(r["xk_raw"], 0, X_INNER, r["gxk"], X_HEADS, X_HEAD_DIM, dxk.reshape(mrows, X_INNER),
                         None, "xk_norm_bwd")
    small["xk_norm_g"] = g.reshape(X_HEADS, X_HEAD_DIM).sum(0)
    dxv2 = dxv.reshape(mrows, X_INNER)
    big["w_xq"] = mm([(r["h2"], dxq_raw)], "tn", out_dtypes=(BF16,), name="xq_proj_wgrad")
    big["w_xk"] = mm([(r["mn"], dxk_raw)], "tn", out_dtypes=(BF16,), name="xkv_proj_wgrad")
    big["w_xv"] = mm([(r["mn"], dxv2)], "tn", out_dtypes=(BF16,), name="xkv_proj_wgrad")
    dh2 = mm([(dxq_raw, w["w_xq"])], "nt", name="xq_proj_bwd")
    dmn = mm([(dxk_raw, w["w_xk"]), (dxv2, w["w_xv"])], "nt", name="xkv_proj_bwd")
    _, g = rms_bwd(memf, 0, D_MODEL, _row(p["mem_norm_g"]), 1, D_MODEL, dmn, None, "mem_norm_bwd")
    small["mem_norm_g"] = g[0]
    dx2, g = rms_bwd(r["x2"], 0, D_MODEL, _row(p["xattn_norm_g"]), 1, D_MODEL, dh2, dx3, "norm2_bwd")
    small["xattn_norm_g"] = g[0]

    dmixed = mm([(dx2, w["w_out"])], "nt", name="out_proj_bwd")
    big["w_out"] = mm([(r["mixed_in"], dx2)], "tn", out_dtypes=(BF16,), name="out_proj_wgrad")
    dmixed3 = dmixed.reshape(bsz, seq, D_MODEL)

    dq_fin, dk_fin, dv, recv_xy, recv_diag = attn_bwd(
        r["q_fin"], r["k_fin"], r["kv3"], dmixed3, MLA_HEADS, r["o"], r["lse"], heads=MLA_HEADS, dk=MLA_QK_PAD,
        dv=MLA_V, v_mul=2, v_off=1, causal=True, scale=MLA_QK ** -0.5, tq=512, tk=512,
        name="mla_attn_bwd" if pending is None else "mla_attn_bwd_chip_exchange",
        rider_dq=rider_xy, rider_dkv=rider_diag)
    dqh = rope(dq_fin, *tabs, True, F32, "rope_bwd").reshape(t_rows, 2048)
    dkh = rope(dk_fin, *tabs, True, F32, "rope_bwd").reshape(t_rows, 2048)
    dq_raw, g = rms_bwd(r["q_raw"], 0, 2048, r["gq"], MLA_HEADS, MLA_QK, dqh, None, "q_head_norm_bwd")
    small["mla_q_norm_g"] = g.reshape(MLA_HEADS, MLA_QK_PAD)[:, :MLA_QK].sum(0)
    dk_raw, g = rms_bwd(r["k_raw"], 0, 2048, r["gk"], MLA_HEADS, MLA_QK, dkh, None, "q_head_norm_bwd")
    small["mla_k_norm_g"] = g.reshape(MLA_HEADS, MLA_QK_PAD)[:, :MLA_QK].sum(0)
    big["w_q_b"] = _unpad_head_rows(mm([(dq_raw, r["qn"])], "tn", out_dtypes=(BF16,), tm=512,
                                       name="qkv_b_proj_wgrad"))
    dqn = mm([(dq_raw, w["w_q_b"])], "nn", name="qkv_b_proj_bwd")
    dk_raw3 = dk_raw.reshape(t_rows, MLA_HEADS, MLA_QK_PAD)
    dkv = jnp.concatenate([dk_raw3[:, :, :128], dv.reshape(t_rows, MLA_HEADS, MLA_V)], axis=-1)
    dkv = dkv.reshape(t_rows, 2048)
    dkrope = dk_raw3[:, :, 128:192].sum(1)
    big["w_kv_b"] = mm([(dkv, r["kvn"])], "tn", out_dtypes=(BF16,), tm=512, name="qkv_b_proj_wgrad")
    dkvn = mm([(dkv, w["w_kv_b"])], "nn", name="qkv_b_proj_bwd")
    dq_a, g = rms_bwd(r["proj"], 5, Q_LORA, _row(p["q_a_norm_g"]), 1, Q_LORA, dqn, None, "qa_norm_bwd")
    small["q_a_norm_g"] = g[0]
    dkv_a, g = rms_bwd(r["proj"], 6, Q_LORA, _row(p["kv_a_norm_g"]), 1, Q_LORA, dkvn, None, "qa_norm_bwd")
    small["kv_a_norm_g"] = g[0]

    dyg, g = rms_bwd(r["yg"], 0, SSD_INNER, _row(p["ssd_norm_g"]), 2, 512, dmixed, None, "ssd_norm_bwd")
    small["ssd_norm_g"] = g[0]
    dy, dz = ew(_gate_bwd, [(dyg, 0), (r["y"], 0), (r["proj"], 0)], SSD_INNER, (F32, F32), "ssd_gate_bwd")
    dxbc_c, ddt_raw, g_dtb, g_alog, g_dsk = ssd_bwd(
        r["xbc"], r["dt_raw"], _row(p["dt_bias"]), _row(p["a_log"]), _row(p["d_skip"]), r["states"],
        dy.reshape(bsz, seq, SSD_INNER), "ssd_bwd")
    small["dt_bias"], small["a_log"], small["d_skip"] = g_dtb[0], g_alog[0], g_dsk[0]
    dxbc, dw8 = conv_bwd(r["proj"].reshape(bsz, seq, IN_PAD), r["w8"], dxbc_c, "conv_bwd")
    small["conv_w"] = dw8[:4]
    small["conv_b"] = dw8[4]

    dtail = jnp.concatenate([dkrope, ddt_raw.reshape(t_rows, 16), jnp.zeros((t_rows, 48 + 128), F32)], axis=1)
    dproj = jnp.concatenate([dz.astype(BF16), dxbc.reshape(t_rows, CONV_DIM).astype(BF16), dq_a.astype(BF16),
                             dkv_a.astype(BF16), dtail.astype(BF16)], axis=1)
    big["w_in"] = _unpad_w_in_t(mm([(dproj, r["h1"])], "tn", out_dtypes=(BF16,), tm=768, tn=1024,
                                   name="in_proj_wgrad"))
    dh1 = mm([(dproj, w["w_in"])], "nn", name="in_proj_bwd")
    dx, g = rms_bwd(r["x"], 0, D_MODEL, _row(p["attn_norm_g"]), 1, D_MODEL, dh1, dx2, "norm1_bwd")
    small["attn_norm_g"] = g[0]
    if pending is None:
        recv_xy = recv_diag = None
    return dx, big, small, recv_xy, recv_diag


def _rope_tables(positions):
    inv_freq = 1.0 / (ROPE_THETA ** (jnp.arange(0, 64, 2, dtype=F32) / 64))
    ang = positions.astype(F32)[..., None] * inv_freq
    c, s = jnp.cos(ang), jnp.sin(ang)
    z32, z64 = jnp.zeros_like(c), jnp.zeros(c.shape[:-1] + (64,), F32)
    return (jnp.concatenate([c, c, z64], -1), jnp.concatenate([-s, z32, z64], -1),
            jnp.concatenate([z32, s, z64], -1))


def kernel(x, mem, positions, attn_norm_g, w_in, conv_w, conv_b, dt_bias, a_log, d_skip, ssd_norm_g, q_a_norm_g, w_q_b, kv_a_norm_g, w_kv_b, mla_q_norm_g, mla_k_norm_g, w_out, xattn_norm_g, mem_norm_g, w_xq, w_xk, w_xv, xq_norm_g, xk_norm_g, w_xo, ffn_norm_g, w_gate, w_up, w_down, loss_target, m_attn_norm_g, m_w_in, m_conv_w, m_conv_b, m_dt_bias, m_a_log, m_d_skip, m_ssd_norm_g, m_q_a_norm_g, m_w_q_b, m_kv_a_norm_g, m_w_kv_b, m_mla_q_norm_g, m_mla_k_norm_g, m_w_out, m_xattn_norm_g, m_mem_norm_g, m_w_xq, m_w_xk, m_w_xv, m_xq_norm_g, m_xk_norm_g, m_w_xo, m_ffn_norm_g, m_w_gate, m_w_up, m_w_down, v_attn_norm_g, v_w_in, v_conv_w, v_conv_b, v_dt_bias, v_a_log, v_d_skip, v_ssd_norm_g, v_q_a_norm_g, v_w_q_b, v_kv_a_norm_g, v_w_kv_b, v_mla_q_norm_g, v_mla_k_norm_g, v_w_out, v_xattn_norm_g, v_mem_norm_g, v_w_xq, v_w_xk, v_w_xv, v_xq_norm_g, v_xk_norm_g, v_w_xo, v_ffn_norm_g, v_w_gate, v_w_up, v_w_down):
    args = dict(locals())
    weights = {n: args[n] for n in WEIGHT_ORDER}
    mom_m = {n: args["m_" + n] for n in WEIGHT_ORDER}
    mom_v = {n: args["v_" + n] for n in WEIGHT_ORDER}
    bsz, seq, _ = x.shape
    t_rows = bsz * seq
    xf = x.reshape(t_rows, D_MODEL)
    memf = mem.reshape(-1, D_MODEL)
    tabs = _rope_tables(positions)

    full = []
    def local_blocks(l):
        return [_pack_local(fam, {n: weights[n][l] for n, _, _, _ in fam}, BF16) for fam, _ in FAMS]

    def full_weights(gathered):
        wl = {}
        for (fam, _), g in zip(FAMS, gathered):
            wl.update(_unpack_gathered(fam, g))
        wl["w_in"] = _pad_w_in_t(wl["w_in"])
        wl["w_q_b"] = _pad_head_rows(wl["w_q_b"])
        return wl

    full.append(full_weights(all_gather(local_blocks(0), "weight_all_gather")))
    small_p = [{n: weights[n][l] for n, _ in SMALL} for l in range(DEPTH)]
    cw = all_gather([conv_w.reshape(-1, 128)], "conv_w_all_gather")[0].reshape(N_DEV, DEPTH, 4, -1)
    cw = cw.transpose(1, 2, 0, 3).reshape(DEPTH, 4, CONV_DIM)
    for l in range(DEPTH):
        small_p[l]["conv_w_full"] = cw[l]

    saved = []
    h = xf
    for l in range(DEPTH):
        h, res, gathered = layer_fwd(h, memf, full[l], small_p[l], tabs, bsz,
                                     local_blocks(l + 1) if l + 1 < DEPTH else None)
        saved.append(res)
        if gathered is not None:
            full.append(full_weights(gathered))
    dy, loss_part = loss_head(h, loss_target.reshape(t_rows, D_MODEL), "loss_head")
    loss = lax.psum(loss_part[0, 0], ("x", "y", "c"))

    small_g = [None] * DEPTH
    outs = {k: {} for k in ("g", "d", "m", "v")}
    per_layer = [None] * DEPTH

    def update_layer(l, recv_xy, recv_diag):
        layer_out = [{}, {}, {}, {}]
        for a, (fam, width) in enumerate(FAMS):
            loc = [_pack_local(fam, {n: src[n][l] for n, _, _, _ in fam}, F32) for src in (weights, mom_m, mom_v)]
            res = adamw([recv_xy[a], recv_diag[a]], *loc, "adamw_big", tr=48 if width == 2048 else 64)
            for ki in range(4):
                layer_out[ki].update(_unpack_local(fam, res[ki]))
        per_layer[l] = layer_out

    dh, pending = dy, None
    for l in reversed(range(DEPTH)):
        dh, big_g, small_g[l], recv_xy, recv_diag = layer_bwd(dh, saved[l], memf, full[l], small_p[l], tabs, bsz,
                                                             pending)
        if pending is not None:
            update_layer(l + 1, recv_xy, recv_diag)
        pending = [_pack_full_grads(fam, big_g) for fam, _ in FAMS]
    from_sib = comm_call([sibling_stage_rider(pending)], "grad_sibling_exchange")[0]
    partials = [chip_partial(b, s, "grad_chip_partial", tr) for b, s, tr in zip(pending, from_sib, PARTIAL_TR)]
    recv_xy, recv_diag = comm_call([chip_stage_rider(partials, False), chip_stage_rider(partials, True)],
                                   "grad_chip_exchange")
    update_layer(0, recv_xy, recv_diag)
    for ki, key in enumerate(("g", "d", "m", "v")):
        for fam, _ in FAMS:
            for n, _, _, _ in fam:
                outs[key][n] = jnp.stack([per_layer[l][ki][n] for l in range(DEPTH)])

    sg = _pack_small({n: jnp.stack([small_g[l][n] for l in range(DEPTH)]) for n, _ in SMALL})
    cg = jnp.stack([small_g[l]["conv_w"] for l in range(DEPTH)]).reshape(CONVW_ROWS, 128)
    parts = all_gather([jnp.concatenate([sg, cg], axis=0)], "small_grad_all_gather")[0]
    sm = adamw([parts[:, :SMALL_ROWS]], _pack_small({n: weights[n] for n, _ in SMALL}),
               _pack_small({n: mom_m[n] for n, _ in SMALL}), _pack_small({n: mom_v[n] for n, _ in SMALL}),
               "adamw_small", tr=SMALL_ROWS)
    for ki, key in enumerate(("g", "d", "m", "v")):
        outs[key].update(_unpack_small(sm[ki]))
    my_idx = 4 * lax.axis_index("x") + 2 * lax.axis_index("y") + lax.axis_index("c")
    cparts = parts[:, SMALL_ROWS:].reshape(N_DEV, DEPTH, 4, N_DEV, -1)
    cparts = lax.dynamic_index_in_dim(cparts, my_idx, axis=3, keepdims=False).reshape(N_DEV, -1, 128)
    cres = adamw([cparts], conv_w.reshape(-1, 128), m_conv_w.reshape(-1, 128), v_conv_w.reshape(-1, 128),
                 "adamw_conv_w")
    for ki, key in enumerate(("g", "d", "m", "v")):
        outs[key]["conv_w"] = cres[ki].reshape(conv_w.shape)

    grad_x = dh.reshape(bsz, seq, D_MODEL)
    return (loss, grad_x, *[outs["g"][n] for n in WEIGHT_ORDER], *[outs["d"][n] for n in WEIGHT_ORDER],
            *[outs["m"][n] for n in WEIGHT_ORDER], *[outs["v"][n] for n in WEIGHT_ORDER])
```

```python
import functools

import jax
import jax.numpy as jnp
from jax import lax
from jax.experimental import pallas as pl
from jax.experimental.pallas import tpu as pltpu

F32 = jnp.float32
BF16 = jnp.bfloat16

D_MODEL = 2048
DEPTH = 4
N_DEV = 8
SSD_INNER = 1024
SSD_HEADS = 16
SSD_PAIRS = 8
SSD_STATE = 128
SSD_CHUNK = 128
CONV_DIM = 1536
MLA_HEADS = 8
MLA_QK = 192
MLA_QK_PAD = 256
MLA_V = 128
Q_LORA = 512
X_HEADS = 4
X_HEAD_DIM = 128
X_INNER = 512
FFN = 5632
IN_COLS = 3664
IN_PAD = 3840
TAIL_COL = 3584
ROPE_THETA = 10000.0
EPS = 1e-6
NEG = -1e30
VMEM_LIMIT = 56 * 1024 * 1024
HI = lax.Precision.HIGHEST

ADAM_LR = 0.001
ADAM_B1 = 0.9
ADAM_B2 = 0.999
ADAM_EPS = 1e-08
ADAM_WD = 0.01
ADAM_STEP = 10

SMALL_ROWS = 392
CONVW_ROWS = 192

FAM_A = (("w_in", 458, 464, True), ("w_out", 256, 256, False), ("w_gate", 704, 704, True),
         ("w_up", 704, 704, True), ("w_down", 704, 704, False))
FAM_B = (("w_q_b", 192, 192, True), ("w_kv_b", 256, 256, True), ("w_xq", 256, 256, False),
         ("w_xk", 256, 256, False), ("w_xv", 256, 256, False), ("w_xo", 256, 256, True))
FAMS = ((FAM_A, 2048), (FAM_B, 512))
PARTIAL_TR = (944, 1472)
SMALL = (
    ("attn_norm_g", 2048), ("conv_b", 1536), ("dt_bias", 16), ("a_log", 16), ("d_skip", 16),
    ("ssd_norm_g", 1024), ("q_a_norm_g", 512), ("kv_a_norm_g", 512), ("mla_q_norm_g", 192),
    ("mla_k_norm_g", 192), ("xattn_norm_g", 2048), ("mem_norm_g", 2048), ("xq_norm_g", 128),
    ("xk_norm_g", 128), ("ffn_norm_g", 2048),
)
WEIGHT_ORDER = ("attn_norm_g", "w_in", "conv_w", "conv_b", "dt_bias", "a_log", "d_skip", "ssd_norm_g",
                "q_a_norm_g", "w_q_b", "kv_a_norm_g", "w_kv_b", "mla_q_norm_g", "mla_k_norm_g", "w_out",
                "xattn_norm_g", "mem_norm_g", "w_xq", "w_xk", "w_xv", "xq_norm_g", "xk_norm_g", "w_xo",
                "ffn_norm_g", "w_gate", "w_up", "w_down")


def _cp(sem=None):
    if sem is None:
        return pltpu.CompilerParams(vmem_limit_bytes=VMEM_LIMIT)
    return pltpu.CompilerParams(vmem_limit_bytes=VMEM_LIMIT, dimension_semantics=sem)


def _tile(n, pref):
    if n <= pref:
        return n
    t = (pref // 128) * 128
    while t > 128 and n % t:
        t -= 128
    return t


def _sigmoid(v):
    return 1.0 / (1.0 + jnp.exp(-v))


class Rider:
    def __init__(self, ins, out_shapes, sems, start, finish, aliases=None):
        self.ins, self.out_shapes, self.sems = list(ins), list(out_shapes), list(sems)
        self.start, self.finish = start, finish
        self.aliases = dict(aliases or {})


def _attach(rider, body, grid, in_specs, out_specs, out_shape, scratch, args):
    n_in, n_out, n_scr = len(in_specs), len(out_specs), len(scratch)
    r_in, r_out = len(rider.ins), len(rider.out_shapes)

    def wrapped(*refs):
        ins, rins = refs[:n_in], refs[n_in:n_in + r_in]
        o0 = n_in + r_in
        outs, routs = refs[o0:o0 + n_out], refs[o0 + n_out:o0 + n_out + r_out]
        s0 = o0 + n_out + r_out
        scr, rsems = refs[s0:s0 + n_scr], refs[s0 + n_scr:]
        first = last = None
        for d, g in enumerate(grid):
            i = pl.program_id(d)
            first = (i == 0) if first is None else jnp.logical_and(first, i == 0)
            last = (i == g - 1) if last is None else jnp.logical_and(last, i == g - 1)
        if first is None:
            rider.start(rins, routs, rsems)
            body(*ins, *outs, *scr)
            rider.finish(rins, routs, rsems)
            return
        pl.when(first)(lambda: rider.start(rins, routs, rsems))
        body(*ins, *outs, *scr)
        pl.when(last)(lambda: rider.finish(rins, routs, rsems))

    hbm = pl.BlockSpec(memory_space=pl.ANY)
    io_aliases = {n_in + ri: n_out + ro for ri, ro in rider.aliases.items()}
    return (wrapped, list(in_specs) + [hbm] * r_in, list(out_specs) + [hbm] * r_out,
            list(out_shape) + rider.out_shapes, list(scratch) + rider.sems, list(args) + rider.ins, io_aliases)


def comm_call(riders, name):
    def start(rins, routs, sems):
        for r, (i0, i1, o0, o1, s0, s1) in zip(riders, spans):
            r.start(rins[i0:i1], routs[o0:o1], sems[s0:s1])

    def finish(rins, routs, sems):
        for r, (i0, i1, o0, o1, s0, s1) in zip(riders, spans):
            r.finish(rins[i0:i1], routs[o0:o1], sems[s0:s1])

    spans, i, o, s = [], 0, 0, 0
    for r in riders:
        spans.append((i, i + len(r.ins), o, o + len(r.out_shapes), s, s + len(r.sems)))
        i, o, s = i + len(r.ins), o + len(r.out_shapes), s + len(r.sems)
    aliases = {i0 + ri: o0 + ro for r, (i0, _, o0, _, _, _) in zip(riders, spans) for ri, ro in r.aliases.items()}
    both = Rider([a for r in riders for a in r.ins], [a for r in riders for a in r.out_shapes],
                 [a for r in riders for a in r.sems], start, finish, aliases)
    body, in_specs, out_specs, out_shape, scratch, args, io_aliases = _attach(
        both, lambda: None, (), [], [], [], [], [])
    outs = pl.pallas_call(body, name=name, in_specs=in_specs, out_specs=out_specs, out_shape=out_shape,
                          scratch_shapes=scratch, input_output_aliases=io_aliases)(*args)
    return [outs[a:b] for (_, _, a, b, _, _) in spans]


def rms_fwd(x, col_blk, width, g, groups, n_valid, out_dtype, name, tr=512):
    rows = x.shape[0]
    tr = min(tr, rows)
    wg = width // groups

    def body(x_ref, g_ref, y_ref):
        for gi in range(groups):
            sl = slice(gi * wg, (gi + 1) * wg)
            xv = x_ref[:, sl].astype(F32)
            r = lax.rsqrt(jnp.sum(xv * xv, axis=-1, keepdims=True) * (1.0 / n_valid) + EPS)
            y_ref[:, sl] = ((xv * r) * g_ref[:, sl]).astype(out_dtype)

    return pl.pallas_call(
        body, name=name, grid=(rows // tr,),
        in_specs=[pl.BlockSpec((tr, width), lambda i: (i, col_blk)), pl.BlockSpec((1, width), lambda i: (0, 0))],
        out_specs=pl.BlockSpec((tr, width), lambda i: (i, 0)),
        out_shape=jax.ShapeDtypeStruct((rows, width), out_dtype), compiler_params=_cp(("parallel",)))(x, g)


def rms_bwd(x, col_blk, width, g, groups, n_valid, dy, add, name, tr=512):
    rows = x.shape[0]
    tr = min(tr, rows)
    wg = width // groups
    has_add = add is not None

    def body(*refs):
        if has_add:
            x_ref, g_ref, dy_ref, add_ref, dx_ref, dg_ref = refs
        else:
            x_ref, g_ref, dy_ref, dx_ref, dg_ref = refs

        @pl.when(pl.program_id(0) == 0)
        def _():
            dg_ref[...] = jnp.zeros(dg_ref.shape, F32)

        for gi in range(groups):
            sl = slice(gi * wg, (gi + 1) * wg)
            xv = x_ref[:, sl].astype(F32)
            dyv = dy_ref[:, sl].astype(F32)
            r = lax.rsqrt(jnp.sum(xv * xv, axis=-1, keepdims=True) * (1.0 / n_valid) + EPS)
            xh = xv * r
            dyg = dyv * g_ref[:, sl]
            c = jnp.sum(dyg * xh, axis=-1, keepdims=True) * (1.0 / n_valid)
            dx = r * (dyg - xh * c)
            if has_add:
                dx = dx + add_ref[:, sl]
            dx_ref[:, sl] = dx
            dg_ref[:, sl] += jnp.sum(dyv * xh, axis=0, keepdims=True)

    in_specs = [pl.BlockSpec((tr, width), lambda i: (i, col_blk)), pl.BlockSpec((1, width), lambda i: (0, 0)),
                pl.BlockSpec((tr, width), lambda i: (i, 0))]
    args = [x, g, dy]
    if has_add:
        in_specs.append(pl.BlockSpec((tr, width), lambda i: (i, 0)))
        args.append(add)
    return pl.pallas_call(
        body, name=name, grid=(rows // tr,), in_specs=in_specs,
        out_specs=[pl.BlockSpec((tr, width), lambda i: (i, 0)), pl.BlockSpec((1, width), lambda i: (0, 0))],
        out_shape=[jax.ShapeDtypeStruct((rows, width), F32), jax.ShapeDtypeStruct((1, width), F32)],
        compiler_params=_cp(("arbitrary",)))(*args)


def mm(pairs, mode, *, extras=(), epilogue=None, out_dtypes=(F32,), separate=False, tm=1024, tn=512, name,
       rider=None):
    a0, b0 = pairs[0]
    if mode == "nn":
        m_dim, n_dim = a0.shape[0], b0.shape[1]
        dn = (((1,), (0,)), ((), ()))
    elif mode == "nt":
        m_dim, n_dim = a0.shape[0], b0.shape[0]
        dn = (((1,), (1,)), ((), ()))
    else:
        m_dim, n_dim = a0.shape[1], b0.shape[1]
        dn = (((0,), (0,)), ((), ()))
    tm = _tile(m_dim, tm)
    tn = _tile(n_dim, tn)
    n_pairs = len(pairs)
    n_extra = len(extras)

    def body(*refs):
        ins = refs[:2 * n_pairs]
        ex = refs[2 * n_pairs:2 * n_pairs + n_extra]
        outs = refs[2 * n_pairs + n_extra:]
        accs = []
        for k in range(n_pairs):
            a = ins[2 * k][...].astype(BF16)
            b = ins[2 * k + 1][...].astype(BF16)
            accs.append(lax.dot_general(a, b, dn, preferred_element_type=F32))
        if not separate:
            total = accs[0]
            for extra_acc in accs[1:]:
                total = total + extra_acc
            accs = [total]
        res = epilogue(*accs, *[e[...] for e in ex]) if epilogue is not None else tuple(accs)
        for o_ref, r in zip(outs, res):
            o_ref[...] = r.astype(o_ref.dtype)

    in_specs, args = [], []
    for a, b in pairs:
        if mode == "nn":
            in_specs += [pl.BlockSpec((tm, a.shape[1]), lambda i, j: (i, 0)),
                         pl.BlockSpec((b.shape[0], tn), lambda i, j: (0, j))]
        elif mode == "nt":
            in_specs += [pl.BlockSpec((tm, a.shape[1]), lambda i, j: (i, 0)),
                         pl.BlockSpec((tn, b.shape[1]), lambda i, j: (j, 0))]
        else:
            in_specs += [pl.BlockSpec((a.shape[0], tm), lambda i, j: (0, i)),
                         pl.BlockSpec((b.shape[0], tn), lambda i, j: (0, j))]
        args += [a, b]
    for e in extras:
        in_specs.append(pl.BlockSpec((tm, tn), lambda i, j: (i, j)))
        args.append(e)
    grid = (m_dim // tm, n_dim // tn)
    out_specs = [pl.BlockSpec((tm, tn), lambda i, j: (i, j)) for _ in out_dtypes]
    out_shape = [jax.ShapeDtypeStruct((m_dim, n_dim), dt) for dt in out_dtypes]
    scratch, sem, io_aliases = [], ("parallel", "parallel"), {}
    if rider is not None:
        body, in_specs, out_specs, out_shape, scratch, args, io_aliases = _attach(
            rider, body, grid, in_specs, out_specs, out_shape, scratch, args)
        sem = ("arbitrary", "arbitrary")
    outs = pl.pallas_call(body, name=name, grid=grid, in_specs=in_specs, out_specs=out_specs, out_shape=out_shape,
                          scratch_shapes=scratch, input_output_aliases=io_aliases, compiler_params=_cp(sem))(*args)
    main = outs[0] if len(out_dtypes) == 1 else outs[:len(out_dtypes)]
    return main if rider is None else (main, outs[len(out_dtypes):])


def _add_res(acc, res):
    return (acc + res,)


def _swiglu_fwd(gate, up):
    gb = gate.astype(BF16).astype(F32)
    ub = up.astype(BF16).astype(F32)
    return gate, up, gb * _sigmoid(gb) * ub


def _swiglu_bwd(dact, gate, up):
    gv = gate.astype(F32)
    uv = up.astype(F32)
    sg = _sigmoid(gv)
    return dact * uv * (sg * (1.0 + gv * (1.0 - sg))), dact * (gv * sg)


def ew(fn, ins, width, out_dtypes, name, tr=512):
    rows = ins[0][0].shape[0]
    tr = min(tr, rows)
    n_in = len(ins)

    def body(*refs):
        res = fn(*[r[...].astype(F32) for r in refs[:n_in]])
        for o_ref, r in zip(refs[n_in:], res):
            o_ref[...] = r.astype(o_ref.dtype)

    in_specs = [pl.BlockSpec((tr, width), functools.partial(lambda i, cb: (i, cb), cb=cb)) for _, cb in ins]
    outs = pl.pallas_call(
        body, name=name, grid=(rows // tr,), in_specs=in_specs,
        out_specs=[pl.BlockSpec((tr, width), lambda i: (i, 0)) for _ in out_dtypes],
        out_shape=[jax.ShapeDtypeStruct((rows, width), dt) for dt in out_dtypes],
        compiler_params=_cp(("parallel",)))(*[a for a, _ in ins])
    return outs[0] if len(out_dtypes) == 1 else outs


def _gate_fwd(y, z):
    return (y * (z * _sigmoid(z)),)


def _gate_bwd(dyg, y, z):
    sg = _sigmoid(z)
    return dyg * (z * sg), dyg * y * (sg * (1.0 + z * (1.0 - sg)))


CONV_CB = 512
CONV_OFF = 2


def _conv_pre(x, w_ref, row):
    acc = x * w_ref[pl.ds(3, 1), :] + w_ref[pl.ds(4, 1), :]
    shifted = []
    for j in (1, 2, 3):
        xs = jnp.where(row >= j, pltpu.roll(x, j, 0), 0.0)
        shifted.append(xs)
        acc = acc + xs * w_ref[pl.ds(3 - j, 1), :]
    return acc, shifted


def conv_fwd(proj3, w8, name):
    bsz, seq, _ = proj3.shape

    def body(x_ref, w_ref, o_ref):
        x = x_ref[0]
        row = lax.broadcasted_iota(jnp.int32, x.shape, 0)
        pre, _ = _conv_pre(x, w_ref, row)
        o_ref[0] = pre * _sigmoid(pre)

    return pl.pallas_call(
        body, name=name, grid=(bsz, CONV_DIM // CONV_CB),
        in_specs=[pl.BlockSpec((1, seq, CONV_CB), lambda b, c: (b, 0, c + CONV_OFF)),
                  pl.BlockSpec((8, CONV_CB), lambda b, c: (0, c))],
        out_specs=pl.BlockSpec((1, seq, CONV_CB), lambda b, c: (b, 0, c)),
        out_shape=jax.ShapeDtypeStruct((bsz, seq, CONV_DIM), F32),
        compiler_params=_cp(("parallel", "parallel")))(proj3, w8)


def conv_bwd(proj3, w8, dout, name):
    bsz, seq, _ = proj3.shape

    def body(x_ref, w_ref, do_ref, dx_ref, dw_ref):
        @pl.when(pl.program_id(1) == 0)
        def _():
            dw_ref[...] = jnp.zeros(dw_ref.shape, F32)

        x = x_ref[0]
        row = lax.broadcasted_iota(jnp.int32, x.shape, 0)
        pre, shifted = _conv_pre(x, w_ref, row)
        sg = _sigmoid(pre)
        dpre = do_ref[0] * (sg * (1.0 + pre * (1.0 - sg)))
        dx = dpre * w_ref[pl.ds(3, 1), :]
        for j in (1, 2, 3):
            fut = jnp.where(row < seq - j, pltpu.roll(dpre, seq - j, 0), 0.0)
            dx = dx + fut * w_ref[pl.ds(3 - j, 1), :]
            dw_ref[pl.ds(3 - j, 1), :] += jnp.sum(dpre * shifted[j - 1], axis=0, keepdims=True)
        dw_ref[pl.ds(3, 1), :] += jnp.sum(dpre * x, axis=0, keepdims=True)
        dw_ref[pl.ds(4, 1), :] += jnp.sum(dpre, axis=0, keepdims=True)
        dx_ref[0] = dx

    return pl.pallas_call(
        body, name=name, grid=(CONV_DIM // CONV_CB, bsz),
        in_specs=[pl.BlockSpec((1, seq, CONV_CB), lambda c, b: (b, 0, c + CONV_OFF)),
                  pl.BlockSpec((8, CONV_CB), lambda c, b: (0, c)),
                  pl.BlockSpec((1, seq, CONV_CB), lambda c, b: (b, 0, c))],
        out_specs=[pl.BlockSpec((1, seq, CONV_CB), lambda c, b: (b, 0, c)),
                   pl.BlockSpec((8, CONV_CB), lambda c, b: (0, c))],
        out_shape=[jax.ShapeDtypeStruct((bsz, seq, CONV_DIM), F32), jax.ShapeDtypeStruct((8, CONV_DIM), F32)],
        compiler_params=_cp(("parallel", "arbitrary")))(proj3, w8, dout)


def _softplus(v):
    t = jnp.exp(-jnp.abs(v))
    small = t * (1.0 - t * (0.5 - t * (1.0 / 3.0)))
    return jnp.maximum(v, 0.0) + jnp.where(t < 0.01, small, jnp.log(1.0 + t))


def _ssd_chunk_prelude(dt_ref, dtb_ref, alog_ref):
    L = SSD_CHUNK
    raw = dt_ref[0] + dtb_ref[...]
    dt = _softplus(raw)
    a_neg = -jnp.exp(alog_ref[...])
    a = dt * a_neg
    r_i = lax.broadcasted_iota(jnp.int32, (L, L), 0)
    c_i = lax.broadcasted_iota(jnp.int32, (L, L), 1)
    tri_low = (r_i >= c_i).astype(F32)
    tri_up = (r_i <= c_i).astype(F32)
    acs = jnp.dot(tri_low, a, precision=HI, preferred_element_type=F32)
    acs_t = lax.dot_general(a, tri_up, (((0,), (0,)), ((), ())), precision=HI,
                            preferred_element_type=F32)
    tot = jnp.sum(a, axis=0, keepdims=True)
    return raw, dt, a_neg, acs, acs_t, tot, r_i, c_i, tri_up


def _col(arr16, e):
    lane = lax.broadcasted_iota(jnp.int32, arr16.shape, 1)
    return jnp.sum(jnp.where(lane == e, arr16, 0.0), axis=1, keepdims=True)


def _rowvec(arr_t, e):
    sub = lax.broadcasted_iota(jnp.int32, arr_t.shape, 0)
    return jnp.sum(jnp.where(sub == e, arr_t, 0.0), axis=0, keepdims=True)


def _pair(lo_mask, v0, v1):
    return jnp.where(lo_mask, v0, v1)


def ssd_fwd(xbc, dt_raw, dtb, alog, dsk, name):
    bsz, seq, _ = xbc.shape
    L = SSD_CHUNK
    nc = seq // L

    def body(x_ref, b_ref, c_ref, dt_ref, dtb_ref, alog_ref, dsk_ref, y_ref, st_ref, h_sc):
        @pl.when(pl.program_id(1) == 0)
        def _():
            h_sc[...] = jnp.zeros(h_sc.shape, F32)

        _, dt, _, acs, acs_t, tot, r_i, c_i, _ = _ssd_chunk_prelude(dt_ref, dtb_ref, alog_ref)
        causal = r_i >= c_i
        lo = lax.broadcasted_iota(jnp.int32, (1, 128), 1) < 64
        dskv = dsk_ref[...]
        for grp in range(2):
            bg = b_ref[0, :, grp * 128:(grp + 1) * 128].astype(BF16)
            cg = c_ref[0, :, grp * 128:(grp + 1) * 128].astype(BF16)
            cb = lax.dot_general(cg, bg, (((1,), (1,)), ((), ())), preferred_element_type=F32)
            for kk in range(4):
                k = grp * 4 + kk
                e0, e1 = 2 * k, 2 * k + 1
                sl = slice(k * 128, (k + 1) * 128)
                xp = x_ref[0, :, sl]
                cols = [_col(acs, e0), _col(acs, e1)]
                dtp = _pair(lo, _col(dt, e0), _col(dt, e1))
                xdt = xp * dtp
                ydiag = jnp.zeros((L, 128), F32)
                for sub, e in enumerate((e0, e1)):
                    diff = cols[sub] - _rowvec(acs_t, e)
                    lam = jnp.exp(jnp.where(causal, diff, NEG))
                    m_e = (cb * lam).astype(BF16)
                    xm = jnp.where(lo if sub == 0 else jnp.logical_not(lo), xdt, 0.0).astype(BF16)
                    ydiag = ydiag + jnp.dot(m_e, xm, preferred_element_type=F32)
                h = h_sc[k]
                st_ref[0, 0, k] = h
                wp = _pair(lo, jnp.exp(cols[0]), jnp.exp(cols[1]))
                yoff = jnp.dot(cg, h.astype(BF16), preferred_element_type=F32) * wp
                tot0, tot1 = _col(tot, e0), _col(tot, e1)
                up = _pair(lo, jnp.exp(tot0 - cols[0]), jnp.exp(tot1 - cols[1]))
                etot = _pair(lo, jnp.exp(tot0), jnp.exp(tot1))
                h_sc[k] = h * etot + lax.dot_general(bg, (xdt * up).astype(BF16), (((0,), (0,)), ((), ())),
                                                     preferred_element_type=F32)
                dskp = _pair(lo, _col(dskv, e0), _col(dskv, e1))
                y_ref[0, :, sl] = ydiag + yoff + xp * dskp

    p16 = pl.BlockSpec((1, 16), lambda b, c: (0, 0))
    return pl.pallas_call(
        body, name=name, grid=(bsz, nc),
        in_specs=[pl.BlockSpec((1, L, 1024), lambda b, c: (b, c, 0)),
                  pl.BlockSpec((1, L, 256), lambda b, c: (b, c, 4)),
                  pl.BlockSpec((1, L, 256), lambda b, c: (b, c, 5)),
                  pl.BlockSpec((1, L, 16), lambda b, c: (b, c, 0)), p16, p16, p16],
        out_specs=[pl.BlockSpec((1, L, 1024), lambda b, c: (b, c, 0)),
                   pl.BlockSpec((1, 1, SSD_PAIRS, 128, 128), lambda b, c: (b, c, 0, 0, 0))],
        out_shape=[jax.ShapeDtypeStruct((bsz, seq, SSD_INNER), F32),
                   jax.ShapeDtypeStruct((bsz, nc, SSD_PAIRS, 128, 128), F32)],
        scratch_shapes=[pltpu.VMEM((SSD_PAIRS, 128, 128), F32)],
        compiler_params=_cp(("parallel", "arbitrary")))(xbc, xbc, xbc, dt_raw, dtb, alog, dsk)


def ssd_bwd(xbc, dt_raw, dtb, alog, dsk, states, dy, name):
    bsz, seq, _ = xbc.shape
    L = SSD_CHUNK
    nc = seq // L

    def body(x_ref, b_ref, c_ref, dt_ref, dtb_ref, alog_ref, dsk_ref, st_ref, dy_ref,
             dxbc_ref, ddt_ref, gdtb_ref, galog_ref, gdsk_ref, dh_sc):
        first = jnp.logical_and(pl.program_id(0) == 0, pl.program_id(1) == 0)

        @pl.when(first)
        def _():
            gdtb_ref[...] = jnp.zeros(gdtb_ref.shape, F32)
            galog_ref[...] = jnp.zeros(galog_ref.shape, F32)
            gdsk_ref[...] = jnp.zeros(gdsk_ref.shape, F32)

        @pl.when(pl.program_id(1) == 0)
        def _():
            dh_sc[...] = jnp.zeros(dh_sc.shape, F32)

        raw, dt, a_neg, acs, acs_t, tot, r_i, c_i, tri_up = _ssd_chunk_prelude(dt_ref, dtb_ref, alog_ref)
        causal = r_i >= c_i
        causal_t = c_i >= r_i
        lo = lax.broadcasted_iota(jnp.int32, (1, 128), 1) < 64
        hi = jnp.logical_not(lo)
        lane16 = lax.broadcasted_iota(jnp.int32, (L, 16), 1)
        lane16r = lax.broadcasted_iota(jnp.int32, (1, 16), 1)
        last_row = lax.broadcasted_iota(jnp.int32, (L, 1), 0) == L - 1
        dskv = dsk_ref[...]
        ds16 = jnp.zeros((L, 16), F32)
        ddt16 = jnp.zeros((L, 16), F32)
        gdsk = jnp.zeros((1, 16), F32)

        def hsum(t, mask):
            return jnp.sum(jnp.where(mask, t, 0.0), axis=1, keepdims=True)

        for grp in range(2):
            gs = slice(grp * 128, (grp + 1) * 128)
            bg = b_ref[0, :, gs].astype(BF16)
            cg = c_ref[0, :, gs].astype(BF16)
            cb = lax.dot_general(cg, bg, (((1,), (1,)), ((), ())), preferred_element_type=F32)
            cbt = lax.dot_general(bg, cg, (((1,), (1,)), ((), ())), preferred_element_type=F32)
            dcb = jnp.zeros((L, L), F32)
            dcbt = jnp.zeros((L, L), F32)
            dc_g = jnp.zeros((L, 128), F32)
            db_g = jnp.zeros((L, 128), F32)
            for kk in range(4):
                k = grp * 4 + kk
                e0, e1 = 2 * k, 2 * k + 1
                sl = slice(k * 128, (k + 1) * 128)
                xp = x_ref[0, :, sl]
                dyp = dy_ref[0, :, sl]
                h = st_ref[0, 0, k]
                dh = dh_sc[k]
                cols = [_col(acs, e0), _col(acs, e1)]
                tots = [_col(tot, e0), _col(tot, e1)]
                dtp = _pair(lo, _col(dt, e0), _col(dt, e1))
                xdt = xp * dtp
                wp = _pair(lo, jnp.exp(cols[0]), jnp.exp(cols[1]))
                up = _pair(lo, jnp.exp(tots[0] - cols[0]), jnp.exp(tots[1] - cols[1]))
                etot = _pair(lo, jnp.exp(tots[0]), jnp.exp(tots[1]))
                hb = h.astype(BF16)
                dhb = dh.astype(BF16)
                yoff = jnp.dot(cg, hb, preferred_element_type=F32) * wp
                dyw = (dyp * wp).astype(BF16)
                dc_g = dc_g + lax.dot_general(dyw, hb, (((1,), (1,)), ((), ())), preferred_element_type=F32)
                dh_in = lax.dot_general(cg, dyw, (((0,), (0,)), ((), ())), preferred_element_type=F32) + dh * etot
                q_mat = jnp.dot(bg, dhb, preferred_element_type=F32)
                db_g = db_g + lax.dot_general((xdt * up).astype(BF16), dhb, (((1,), (1,)), ((), ())),
                                              preferred_element_type=F32)
                dxdt = up * q_mat
                t_u = q_mat * xdt * up
                t_w = dyp * yoff
                t_h = jnp.sum(dh * h, axis=0, keepdims=True) * etot
                for sub, e in enumerate((e0, e1)):
                    msk = lo if sub == 0 else hi
                    rowv = _rowvec(acs_t, e)
                    lam = jnp.exp(jnp.where(causal, cols[sub] - rowv, NEG))
                    lam_t = jnp.exp(jnp.where(causal_t, rowv - cols[sub], NEG))
                    xm = jnp.where(msk, xdt, 0.0).astype(BF16)
                    dym = jnp.where(msk, dyp, 0.0).astype(BF16)
                    g_mat = lax.dot_general(dym, xm, (((1,), (1,)), ((), ())), preferred_element_type=F32)
                    g_t = lax.dot_general(xm, dym, (((1,), (1,)), ((), ())), preferred_element_type=F32)
                    gl = g_mat * lam
                    glt = g_t * lam_t
                    dcb = dcb + gl
                    dcbt = dcbt + glt
                    e_r = jnp.sum(gl * cb, axis=1, keepdims=True)
                    e_c = jnp.sum(glt * cbt, axis=1, keepdims=True)
                    dxdt = dxdt + jnp.dot((cbt * lam_t).astype(BF16), dym, preferred_element_type=F32)
                    u_rows = hsum(t_u, msk)
                    ds_tot = jnp.sum(u_rows, axis=0, keepdims=True) + hsum(t_h, msk)
                    ds_col = e_r - e_c + hsum(t_w, msk) - u_rows + jnp.where(last_row, ds_tot, 0.0)
                    ds16 = ds16 + jnp.where(lane16 == e, ds_col, 0.0)
                dxp = dxdt * xp
                dyx = dyp * xp
                for sub, e in enumerate((e0, e1)):
                    msk = lo if sub == 0 else hi
                    ddt16 = ddt16 + jnp.where(lane16 == e, hsum(dxp, msk), 0.0)
                    gdsk = gdsk + jnp.where(lane16r == e, jnp.sum(hsum(dyx, msk), axis=0, keepdims=True), 0.0)
                dskp = _pair(lo, _col(dskv, e0), _col(dskv, e1))
                dxbc_ref[0, :, sl] = dxdt * dtp + dyp * dskp
                dh_sc[k] = dh_in
            dc_g = dc_g + jnp.dot(dcb.astype(BF16), bg, preferred_element_type=F32)
            db_g = db_g + jnp.dot(dcbt.astype(BF16), cg, preferred_element_type=F32)
            dxbc_ref[0, :, 1024 + grp * 128:1024 + (grp + 1) * 128] = db_g
            dxbc_ref[0, :, 1280 + grp * 128:1280 + (grp + 1) * 128] = dc_g
        da16 = jnp.dot(tri_up, ds16, precision=HI, preferred_element_type=F32)
        ddt16 = ddt16 + da16 * a_neg
        d_aneg = jnp.sum(da16 * dt, axis=0, keepdims=True)
        ddt_raw = ddt16 * _sigmoid(raw)
        ddt_ref[0] = ddt_raw
        gdtb_ref[...] += jnp.sum(ddt_raw, axis=0, keepdims=True)
        galog_ref[...] += d_aneg * a_neg
        gdsk_ref[...] += gdsk

    p16 = pl.BlockSpec((1, 16), lambda b, c: (0, 0))
    rev = lambda b, c: (b, nc - 1 - c, 0)
    return pl.pallas_call(
        body, name=name, grid=(bsz, nc),
        in_specs=[pl.BlockSpec((1, L, 1024), rev),
                  pl.BlockSpec((1, L, 256), lambda b, c: (b, nc - 1 - c, 4)),
                  pl.BlockSpec((1, L, 256), lambda b, c: (b, nc - 1 - c, 5)),
                  pl.BlockSpec((1, L, 16), rev), p16, p16, p16,
                  pl.BlockSpec((1, 1, SSD_PAIRS, 128, 128), lambda b, c: (b, nc - 1 - c, 0, 0, 0)),
                  pl.BlockSpec((1, L, 1024), rev)],
        out_specs=[pl.BlockSpec((1, L, CONV_DIM), rev), pl.BlockSpec((1, L, 16), rev), p16, p16, p16],
        out_shape=[jax.ShapeDtypeStruct((bsz, seq, CONV_DIM), F32), jax.ShapeDtypeStruct((bsz, seq, 16), F32),
                   jax.ShapeDtypeStruct((1, 16), F32), jax.ShapeDtypeStruct((1, 16), F32),
                   jax.ShapeDtypeStruct((1, 16), F32)],
        scratch_shapes=[pltpu.VMEM((SSD_PAIRS, 128, 128), F32)],
        compiler_params=_cp(("arbitrary", "arbitrary")))(xbc, xbc, xbc, dt_raw, dtb, alog, dsk, states, dy)


def rope(x3, cos_t, sin_a, sin_b, transpose, out_dtype, name, ts=512):
    bsz, seq, width = x3.shape
    heads = width // MLA_QK_PAD
    ts = min(ts, seq)

    def body(x_ref, c_ref, sa_ref, sb_ref, o_ref):
        c, sa, sb = c_ref[0], sa_ref[0], sb_ref[0]
        for h in range(heads):
            base = h * MLA_QK_PAD
            o_ref[0, :, base:base + 128] = x_ref[0, :, base:base + 128].astype(out_dtype)
            v = x_ref[0, :, base + 128:base + 256]
            if transpose:
                r = v * c + pltpu.roll(v * sa, 32, 1) + pltpu.roll(v * sb, 96, 1)
            else:
                r = v * c + pltpu.roll(v, 96, 1) * sa + pltpu.roll(v, 32, 1) * sb
            o_ref[0, :, base + 128:base + 256] = r.astype(out_dtype)

    tab = pl.BlockSpec((1, ts, 128), lambda b, i: (b, i, 0))
    return pl.pallas_call(
        body, name=name, grid=(bsz, seq // ts),
        in_specs=[pl.BlockSpec((1, ts, width), lambda b, i: (b, i, 0)), tab, tab, tab],
        out_specs=pl.BlockSpec((1, ts, width), lambda b, i: (b, i, 0)),
        out_shape=jax.ShapeDtypeStruct(x3.shape, out_dtype),
        compiler_params=_cp(("parallel", "parallel")))(x3, cos_t, sin_a, sin_b)


def _scores(q_ref, k_ref, scale, causal, i, j, tq, tk):
    s = lax.dot_general(q_ref[0].astype(BF16), k_ref[0].astype(BF16), (((1,), (1,)), ((), ())),
                        preferred_element_type=F32) * scale
    if not causal:
        return s, None
    r = i * tq + lax.broadcasted_iota(jnp.int32, (tq, tk), 0)
    c = j * tk + lax.broadcasted_iota(jnp.int32, (tq, tk), 1)
    mask = r >= c
    return jnp.where(mask, s, NEG), mask


def attn_fwd(q, k, v, *, heads, dk, dv, v_mul, v_off, causal, scale, tq, tk, name, rider=None):
    bsz, sq, _ = q.shape
    sk = k.shape[1]
    tq, tk = min(tq, sq), min(tk, sk)
    nq, nk = sq // tq, sk // tk

    def body(q_ref, k_ref, v_ref, o_ref, lse_ref, m_sc, l_sc, acc_sc):
        i, j = pl.program_id(2), pl.program_id(3)

        @pl.when(j == 0)
        def _():
            m_sc[...] = jnp.full(m_sc.shape, NEG, F32)
            l_sc[...] = jnp.zeros(l_sc.shape, F32)
            acc_sc[...] = jnp.zeros(acc_sc.shape, F32)

        def compute(masked=False):
            s, _ = _scores(q_ref, k_ref, scale, masked, i, j, tq, tk)
            m_prev = m_sc[...]
            m_new = jnp.maximum(m_prev, jnp.max(s, axis=1, keepdims=True))
            p = jnp.exp(s - m_new)
            alpha = jnp.exp(m_prev - m_new)
            l_sc[...] = alpha * l_sc[...] + jnp.sum(p, axis=1, keepdims=True)
            acc_sc[...] = alpha * acc_sc[...] + jnp.dot(p.astype(BF16), v_ref[0].astype(BF16),
                                                        preferred_element_type=F32)
            m_sc[...] = m_new

        if causal:
            pl.when(j < i)(compute)
            pl.when(j == i)(functools.partial(compute, True))
        else:
            compute()

        @pl.when(j == nk - 1)
        def _():
            o_ref[0] = acc_sc[...] / l_sc[...]
            lse_ref[0, 0] = m_sc[...] + jnp.log(l_sc[...])

    kvj = (lambda i, j: jnp.minimum(i, j)) if causal else (lambda i, j: j)
    grid = (bsz, heads, nq, nk)
    in_specs = [pl.BlockSpec((1, tq, dk), lambda b, h, i, j: (b, i, h)),
                pl.BlockSpec((1, tk, dk), lambda b, h, i, j: (b, kvj(i, j), h)),
                pl.BlockSpec((1, tk, dv), lambda b, h, i, j: (b, kvj(i, j), v_mul * h + v_off))]
    out_specs = [pl.BlockSpec((1, tq, dv), lambda b, h, i, j: (b, i, h)),
                 pl.BlockSpec((1, 1, tq, 1), lambda b, h, i, j: (b, h, i, 0))]
    out_shape = [jax.ShapeDtypeStruct((bsz, sq, heads * dv), F32), jax.ShapeDtypeStruct((bsz, heads, sq, 1), F32)]
    scratch = [pltpu.VMEM((tq, 1), F32), pltpu.VMEM((tq, 1), F32), pltpu.VMEM((tq, dv), F32)]
    args, sem, io_aliases = [q, k, v], ("parallel", "parallel", "parallel", "arbitrary"), {}
    if rider is not None:
        body, in_specs, out_specs, out_shape, scratch, args, io_aliases = _attach(
            rider, body, grid, in_specs, out_specs, out_shape, scratch, args)
        sem = ("arbitrary",) * 4
    outs = pl.pallas_call(body, name=name, grid=grid, in_specs=in_specs, out_specs=out_specs, out_shape=out_shape,
                          scratch_shapes=scratch, input_output_aliases=io_aliases, compiler_params=_cp(sem))(*args)
    return outs[0], outs[1], outs[2:]


def _attn_grads(q_ref, k_ref, v_ref, do_ref, o_ref, lse_ref, scale, causal, i, j, tq, tk):
    s, mask = _scores(q_ref, k_ref, scale, causal, i, j, tq, tk)
    p = jnp.exp(s - lse_ref[0, 0])
    if mask is not None:
        p = jnp.where(mask, p, 0.0)
    do = do_ref[0]
    dp = lax.dot_general(do.astype(BF16), v_ref[0].astype(BF16), (((1,), (1,)), ((), ())),
                         preferred_element_type=F32)
    delta = jnp.sum(do * o_ref[0], axis=1, keepdims=True)
    ds = p * (dp - delta) * scale
    return p, ds


def attn_bwd(q, k, v, do, do_off, o, lse, *, heads, dk, dv, v_mul, v_off, causal, scale, tq, tk, name,
             rider_dq=None, rider_dkv=None):
    bsz, sq, _ = q.shape
    sk = k.shape[1]
    tq, tk = min(tq, sq), min(tk, sk)
    nq, nk = sq // tq, sk // tk

    def dq_body(q_ref, k_ref, v_ref, do_ref, o_ref, lse_ref, dq_ref, acc_sc):
        i, j = pl.program_id(2), pl.program_id(3)

        @pl.when(j == 0)
        def _():
            acc_sc[...] = jnp.zeros(acc_sc.shape, F32)

        def compute(masked=False):
            _, ds = _attn_grads(q_ref, k_ref, v_ref, do_ref, o_ref, lse_ref, scale, masked, i, j, tq, tk)
            acc_sc[...] += jnp.dot(ds.astype(BF16), k_ref[0].astype(BF16), preferred_element_type=F32)

        if causal:
            pl.when(j < i)(compute)
            pl.when(j == i)(functools.partial(compute, True))
        else:
            compute()

        @pl.when(j == nk - 1)
        def _():
            dq_ref[0] = acc_sc[...]

    kvj = (lambda i, j: jnp.minimum(i, j)) if causal else (lambda i, j: j)
    grid = (bsz, heads, nq, nk)
    in_specs = [pl.BlockSpec((1, tq, dk), lambda b, h, i, j: (b, i, h)),
                pl.BlockSpec((1, tk, dk), lambda b, h, i, j: (b, kvj(i, j), h)),
                pl.BlockSpec((1, tk, dv), lambda b, h, i, j: (b, kvj(i, j), v_mul * h + v_off)),
                pl.BlockSpec((1, tq, dv), lambda b, h, i, j: (b, i, h + do_off)),
                pl.BlockSpec((1, tq, dv), lambda b, h, i, j: (b, i, h)),
                pl.BlockSpec((1, 1, tq, 1), lambda b, h, i, j: (b, h, i, 0))]
    out_specs = [pl.BlockSpec((1, tq, dk), lambda b, h, i, j: (b, i, h))]
    out_shape = [jax.ShapeDtypeStruct((bsz, sq, heads * dk), F32)]
    scratch, args = [pltpu.VMEM((tq, dk), F32)], [q, k, v, do, o, lse]
    sem, io_aliases = ("parallel", "parallel", "parallel", "arbitrary"), {}
    if rider_dq is not None:
        dq_body, in_specs, out_specs, out_shape, scratch, args, io_aliases = _attach(
            rider_dq, dq_body, grid, in_specs, out_specs, out_shape, scratch, args)
        sem = ("arbitrary",) * 4
    dq_outs = pl.pallas_call(dq_body, name=name + "_dq", grid=grid, in_specs=in_specs, out_specs=out_specs,
                             out_shape=out_shape, scratch_shapes=scratch, input_output_aliases=io_aliases,
                             compiler_params=_cp(sem))(*args)
    dq = dq_outs[0]

    def dkv_body(q_ref, k_ref, v_ref, do_ref, o_ref, lse_ref, dk_ref, dv_ref, dk_sc, dv_sc):
        j, i = pl.program_id(2), pl.program_id(3)

        @pl.when(i == 0)
        def _():
            dk_sc[...] = jnp.zeros(dk_sc.shape, F32)
            dv_sc[...] = jnp.zeros(dv_sc.shape, F32)

        def compute(masked=False):
            p, ds = _attn_grads(q_ref, k_ref, v_ref, do_ref, o_ref, lse_ref, scale, masked, i, j, tq, tk)
            dv_sc[...] += lax.dot_general(p.astype(BF16), do_ref[0].astype(BF16), (((0,), (0,)), ((), ())),
                                          preferred_element_type=F32)
            dk_sc[...] += lax.dot_general(ds.astype(BF16), q_ref[0].astype(BF16), (((0,), (0,)), ((), ())),
                                          preferred_element_type=F32)

        if causal:
            pl.when(i > j)(compute)
            pl.when(i == j)(functools.partial(compute, True))
        else:
            compute()

        @pl.when(i == nq - 1)
        def _():
            dk_ref[0] = dk_sc[...]
            dv_ref[0] = dv_sc[...]

    qi = (lambda j, i: jnp.maximum(i, j)) if causal else (lambda j, i: i)
    grid = (bsz, heads, nk, nq)
    in_specs = [pl.BlockSpec((1, tq, dk), lambda b, h, j, i: (b, qi(j, i), h)),
                pl.BlockSpec((1, tk, dk), lambda b, h, j, i: (b, j, h)),
                pl.BlockSpec((1, tk, dv), lambda b, h, j, i: (b, j, v_mul * h + v_off)),
                pl.BlockSpec((1, tq, dv), lambda b, h, j, i: (b, qi(j, i), h + do_off)),
                pl.BlockSpec((1, tq, dv), lambda b, h, j, i: (b, qi(j, i), h)),
                pl.BlockSpec((1, 1, tq, 1), lambda b, h, j, i: (b, h, qi(j, i), 0))]
    out_specs = [pl.BlockSpec((1, tk, dk), lambda b, h, j, i: (b, j, h)),
                 pl.BlockSpec((1, tk, dv), lambda b, h, j, i: (b, j, h))]
    out_shape = [jax.ShapeDtypeStruct((bsz, sk, heads * dk), F32), jax.ShapeDtypeStruct((bsz, sk, heads * dv), F32)]
    scratch, args = [pltpu.VMEM((tk, dk), F32), pltpu.VMEM((tk, dv), F32)], [q, k, v, do, o, lse]
    sem, io_aliases = ("parallel", "parallel", "parallel", "arbitrary"), {}
    if rider_dkv is not None:
        dkv_body, in_specs, out_specs, out_shape, scratch, args, io_aliases = _attach(
            rider_dkv, dkv_body, grid, in_specs, out_specs, out_shape, scratch, args)
        sem = ("arbitrary",) * 4
    dkv_outs = pl.pallas_call(dkv_body, name=name + "_dkv", grid=grid, in_specs=in_specs, out_specs=out_specs,
                              out_shape=out_shape, scratch_shapes=scratch, input_output_aliases=io_aliases,
                              compiler_params=_cp(sem))(*args)
    return dq, dkv_outs[0], dkv_outs[1], dq_outs[1:], dkv_outs[2:]


def loss_head(y, target, name, tr=512):
    rows, width = y.shape
    tr = min(tr, rows)

    def body(y_ref, t_ref, dy_ref, l_ref):
        @pl.when(pl.program_id(0) == 0)
        def _():
            l_ref[...] = jnp.zeros(l_ref.shape, F32)

        err = y_ref[...] - t_ref[...]
        dy_ref[...] = err * (1.0 / width)
        l_ref[...] += jnp.sum(jnp.sum(err * err, axis=1, keepdims=True), axis=0, keepdims=True) * (0.5 / width)

    return pl.pallas_call(
        body, name=name, grid=(rows // tr,),
        in_specs=[pl.BlockSpec((tr, width), lambda i: (i, 0)), pl.BlockSpec((tr, width), lambda i: (i, 0))],
        out_specs=[pl.BlockSpec((tr, width), lambda i: (i, 0)), pl.BlockSpec((1, 1), lambda i: (0, 0))],
        out_shape=[jax.ShapeDtypeStruct((rows, width), F32), jax.ShapeDtypeStruct((1, 1), F32)],
        compiler_params=_cp(("arbitrary",)))(y, target)


def _mesh_pos():
    return lax.axis_index("x"), lax.axis_index("y"), lax.axis_index("c")


def all_gather(blocks, name):
    n_arr = len(blocks)

    def body(*refs):
        x_refs, out_refs = refs[:n_arr], refs[n_arr:2 * n_arr]
        send_sems, recv_sems, local_sems = refs[2 * n_arr:]
        x, y, c = _mesh_pos()
        me, sibling = (x, y, c), (x, y, 1 - c)
        chips = [(1 - x, y), (x, 1 - y), (1 - x, 1 - y)]

        def slot(a, px, py, pc):
            return out_refs[a].at[4 * px + 2 * py + pc]

        def copy(a, k, blk, to, src=None):
            return pltpu.make_async_remote_copy(
                src_ref=slot(a, *blk) if src is None else src, dst_ref=slot(a, *blk),
                send_sem=send_sems.at[7 * a + k], recv_sem=recv_sems.at[7 * a + k],
                device_id=to, device_id_type=pl.DeviceIdType.MESH)

        mine = [pltpu.make_async_copy(x_refs[a], slot(a, *me), local_sems.at[a]) for a in range(n_arr)]
        for cp in mine:
            cp.start()
        first = []
        for a in range(n_arr):
            first.append(copy(a, 0, me, sibling, src=x_refs[a]))
            first += [copy(a, 1 + j, me, (*chip, c), src=x_refs[a]) for j, chip in enumerate(chips)]
        for cp in first:
            cp.start()
        passed = []
        for j, chip in enumerate(chips):
            for a in range(n_arr):
                copy(a, 1 + j, (*chip, c), me).wait_recv()
                passed.append(copy(a, 4 + j, (*chip, c), sibling))
                passed[-1].start()
        for a in range(n_arr):
            copy(a, 0, sibling, me).wait_recv()
            for j, chip in enumerate(chips):
                copy(a, 4 + j, (*chip, 1 - c), me).wait_recv()
        for cp in first + passed:
            cp.wait_send()
        for cp in mine:
            cp.wait()

    return pl.pallas_call(
        body, name=name,
        out_shape=[jax.ShapeDtypeStruct((N_DEV,) + b.shape, b.dtype) for b in blocks],
        in_specs=[pl.BlockSpec(memory_space=pl.ANY)] * n_arr,
        out_specs=[pl.BlockSpec(memory_space=pl.ANY)] * n_arr,
        scratch_shapes=[pltpu.SemaphoreType.DMA((7 * n_arr,)), pltpu.SemaphoreType.DMA((7 * n_arr,)),
                        pltpu.SemaphoreType.DMA((n_arr,))],
    )(*blocks)


def _slot(px, py, pc):
    return 4 * px + 2 * py + pc


def gather_near_rider(blocks):
    n_arr = len(blocks)

    def parts(rins, routs, sems):
        send_sems, recv_sems, local_sems = sems
        x, y, c = _mesh_pos()
        peers = [(x, y, 1 - c), (1 - x, y, c), (x, 1 - y, c)]
        local = [pltpu.make_async_copy(rins[a], routs[a].at[_slot(x, y, c)], local_sems.at[a]) for a in range(n_arr)]
        sends = [pltpu.make_async_remote_copy(
            src_ref=rins[a], dst_ref=routs[a].at[_slot(x, y, c)], send_sem=send_sems.at[3 * a + k],
            recv_sem=recv_sems.at[3 * a + k], device_id=p, device_id_type=pl.DeviceIdType.MESH)
            for a in range(n_arr) for k, p in enumerate(peers)]
        recvs = [pltpu.make_async_remote_copy(
            src_ref=rins[a], dst_ref=routs[a].at[_slot(*p)], send_sem=send_sems.at[3 * a + k],
            recv_sem=recv_sems.at[3 * a + k], device_id=p, device_id_type=pl.DeviceIdType.MESH)
            for a in range(n_arr) for k, p in enumerate(peers)]
        return local, sends, recvs

    def start(rins, routs, sems):
        local, sends, _ = parts(rins, routs, sems)
        for cp in local + sends:
            cp.start()

    def finish(rins, routs, sems):
        local, sends, recvs = parts(rins, routs, sems)
        for cp in recvs:
            cp.wait_recv()
        for cp in sends:
            cp.wait_send()
        for cp in local:
            cp.wait()

    return Rider(blocks, [jax.ShapeDtypeStruct((N_DEV,) + b.shape, b.dtype) for b in blocks],
                 [pltpu.SemaphoreType.DMA((3 * n_arr,)), pltpu.SemaphoreType.DMA((3 * n_arr,)),
                  pltpu.SemaphoreType.DMA((n_arr,))], start, finish)


def gather_diagonal_rider(blocks, gathered):
    n_arr = len(blocks)

    def parts(rins, routs, sems):
        send_sems, recv_sems = sems
        x, y, c = _mesh_pos()
        p = (1 - x, 1 - y, c)
        sends = [pltpu.make_async_remote_copy(
            src_ref=rins[a], dst_ref=routs[a].at[_slot(x, y, c)], send_sem=send_sems.at[a],
            recv_sem=recv_sems.at[a], device_id=p, device_id_type=pl.DeviceIdType.MESH) for a in range(n_arr)]
        recvs = [pltpu.make_async_remote_copy(
            src_ref=rins[a], dst_ref=routs[a].at[_slot(*p)], send_sem=send_sems.at[a],
            recv_sem=recv_sems.at[a], device_id=p, device_id_type=pl.DeviceIdType.MESH) for a in range(n_arr)]
        return sends, recvs

    def start(rins, routs, sems):
        for cp in parts(rins, routs, sems)[0]:
            cp.start()

    def finish(rins, routs, sems):
        sends, recvs = parts(rins, routs, sems)
        for cp in recvs:
            cp.wait_recv()
        for cp in sends:
            cp.wait_send()

    return Rider(list(blocks) + list(gathered), [jax.ShapeDtypeStruct(g.shape, g.dtype) for g in gathered],
                 [pltpu.SemaphoreType.DMA((n_arr,)), pltpu.SemaphoreType.DMA((n_arr,))], start, finish,
                 aliases={n_arr + a: a for a in range(n_arr)})


def gather_forward_rider(gathered):
    n_arr = len(gathered)

    def parts(rins, routs, sems):
        send_sems, recv_sems = sems
        x, y, c = _mesh_pos()
        chips = [(1 - x, y), (x, 1 - y), (1 - x, 1 - y)]
        sends = [pltpu.make_async_remote_copy(
            src_ref=routs[a].at[_slot(px, py, c)], dst_ref=routs[a].at[_slot(px, py, c)],
            send_sem=send_sems.at[3 * a + k], recv_sem=recv_sems.at[3 * a + k],
            device_id=(x, y, 1 - c), device_id_type=pl.DeviceIdType.MESH)
            for a in range(n_arr) for k, (px, py) in enumerate(chips)]
        recvs = [pltpu.make_async_remote_copy(
            src_ref=routs[a].at[_slot(px, py, c)], dst_ref=routs[a].at[_slot(px, py, 1 - c)],
            send_sem=send_sems.at[3 * a + k], recv_sem=recv_sems.at[3 * a + k],
            device_id=(x, y, 1 - c), device_id_type=pl.DeviceIdType.MESH)
            for a in range(n_arr) for k, (px, py) in enumerate(chips)]
        return sends, recvs

    def start(rins, routs, sems):
        for cp in parts(rins, routs, sems)[0]:
            cp.start()

    def finish(rins, routs, sems):
        sends, recvs = parts(rins, routs, sems)
        for cp in recvs:
            cp.wait_recv()
        for cp in sends:
            cp.wait_send()

    return Rider(gathered, [jax.ShapeDtypeStruct(g.shape, g.dtype) for g in gathered],
                 [pltpu.SemaphoreType.DMA((3 * n_arr,)), pltpu.SemaphoreType.DMA((3 * n_arr,))], start, finish,
                 aliases={a: a for a in range(n_arr)})


def sibling_stage_rider(blobs):
    n_arr = len(blobs)

    def copies(rins, routs, sems):
        send_sems, recv_sems = sems
        x, y, c = _mesh_pos()
        return [pltpu.make_async_remote_copy(
            src_ref=rins[a].at[2 * j + (1 - c)], dst_ref=routs[a].at[j],
            send_sem=send_sems.at[4 * a + j], recv_sem=recv_sems.at[4 * a + j],
            device_id=(x, y, 1 - c), device_id_type=pl.DeviceIdType.MESH)
            for a in range(n_arr) for j in range(4)]

    def start(rins, routs, sems):
        for cp in copies(rins, routs, sems):
            cp.start()

    def finish(rins, routs, sems):
        for cp in copies(rins, routs, sems):
            cp.wait()

    return Rider(blobs, [jax.ShapeDtypeStruct((4,) + b.shape[1:], b.dtype) for b in blobs],
                 [pltpu.SemaphoreType.DMA((4 * n_arr,)), pltpu.SemaphoreType.DMA((4 * n_arr,))], start, finish)


def chip_partial(blob, from_sibling, name, tr):
    _, rows, cols = blob.shape
    c_idx = lax.axis_index("c").astype(jnp.int32).reshape(1)

    def body(c_ref, b_ref, s_ref, o_ref):
        o_ref[...] = (b_ref[...].astype(F32) + s_ref[...].astype(F32)).astype(o_ref.dtype)

    return pl.pallas_call(
        body, name=name,
        grid_spec=pltpu.PrefetchScalarGridSpec(
            num_scalar_prefetch=1, grid=(4, rows // tr),
            in_specs=[pl.BlockSpec((1, tr, cols), lambda j, i, c_ref: (2 * j + c_ref[0], i, 0)),
                      pl.BlockSpec((1, tr, cols), lambda j, i, c_ref: (j, i, 0))],
            out_specs=pl.BlockSpec((1, tr, cols), lambda j, i, c_ref: (j, i, 0))),
        out_shape=jax.ShapeDtypeStruct((4, rows, cols), blob.dtype),
        compiler_params=_cp(("parallel", "parallel")))(c_idx, blob, from_sibling)


def chip_stage_rider(partials, diagonal, row_ranges, received=None):
    n_arr = len(partials)
    n_remote = 1 if diagonal else 2

    def copies(rins, routs, sems):
        send_sems, recv_sems = sems[0], sems[1]
        x, y, c = _mesh_pos()
        chips = [(1 - x, 1 - y)] if diagonal else [(1 - x, y), (x, 1 - y)]
        return [pltpu.make_async_remote_copy(
            src_ref=rins[a].at[2 * px + py, pl.ds(*row_ranges[a])],
            dst_ref=routs[a].at[k if diagonal else 1 + k, pl.ds(*row_ranges[a])],
            send_sem=send_sems.at[n_remote * a + k], recv_sem=recv_sems.at[n_remote * a + k],
            device_id=(px, py, c), device_id_type=pl.DeviceIdType.MESH)
            for a in range(n_arr) for k, (px, py) in enumerate(chips)]

    def local(rins, routs, sems):
        x, y, _ = _mesh_pos()
        return [pltpu.make_async_copy(rins[a].at[2 * x + y, pl.ds(*row_ranges[a])],
                                      routs[a].at[0, pl.ds(*row_ranges[a])], sems[2].at[a]) for a in range(n_arr)]

    def start(rins, routs, sems):
        for cp in copies(rins, routs, sems) + ([] if diagonal else local(rins, routs, sems)):
            cp.start()

    def finish(rins, routs, sems):
        for cp in copies(rins, routs, sems) + ([] if diagonal else local(rins, routs, sems)):
            cp.wait()

    sem_list = [pltpu.SemaphoreType.DMA((n_remote * n_arr,)), pltpu.SemaphoreType.DMA((n_remote * n_arr,))]
    if not diagonal:
        sem_list.append(pltpu.SemaphoreType.DMA((n_arr,)))
    out_shapes = [jax.ShapeDtypeStruct((1 if diagonal else 3,) + p.shape[1:], p.dtype) for p in partials]
    if received is None:
        return Rider(partials, out_shapes, sem_list, start, finish)
    return Rider(list(partials) + list(received), out_shapes, sem_list, start, finish,
                 aliases={n_arr + a: a for a in range(n_arr)})


def _halves(partials):
    first, second = [], []
    for p in partials:
        rows = p.shape[1]
        cut = (rows // 32 + (rows // 16) % 2) * 16
        first.append((0, cut))
        second.append((cut, rows - cut))
    return first, second


def adamw(parts, w, m, v, layer, earlier, name, tr=128):
    n_layers, rows, cols = w.shape
    tr = min(tr, rows)
    n_arr = len(parts)
    n_pass = 0 if earlier is None else 4
    c1 = 1.0 - ADAM_B1 ** ADAM_STEP
    c2 = 1.0 - ADAM_B2 ** ADAM_STEP

    def body(*refs):
        p_refs = refs[:n_arr]
        w_ref, m_ref, v_ref = refs[n_arr:n_arr + 3]
        g_ref, d_ref, nm_ref, nv_ref = refs[n_arr + 3 + n_pass:]
        g = None
        for p_ref in p_refs:
            for s in range(p_ref.shape[0]):
                term = p_ref[s].astype(F32)
                g = term if g is None else g + term
        nm = ADAM_B1 * m_ref[0] + (1.0 - ADAM_B1) * g
        nv = ADAM_B2 * v_ref[0] + (1.0 - ADAM_B2) * (g * g)
        m_hat = nm / c1
        v_hat = nv / c2
        g_ref[0] = g
        d_ref[0] = -ADAM_LR * (m_hat / (jnp.sqrt(v_hat) + ADAM_EPS) + ADAM_WD * w_ref[0])
        nm_ref[0] = nm
        nv_ref[0] = nv

    blk = pl.BlockSpec((1, tr, cols), lambda i: (layer, i, 0))
    in_specs = [pl.BlockSpec((p.shape[0], tr, cols), lambda i: (0, i, 0)) for p in parts] + [blk, blk, blk]
    args = list(parts) + [w, m, v]
    aliases = {}
    if earlier is not None:
        in_specs += [pl.BlockSpec(memory_space=pl.ANY)] * 4
        aliases = {n_arr + 3 + k: k for k in range(4)}
        args += list(earlier)
    return pl.pallas_call(
        body, name=name, grid=(rows // tr,), in_specs=in_specs, out_specs=[blk, blk, blk, blk],
        out_shape=[jax.ShapeDtypeStruct((n_layers, rows, cols), F32)] * 4, input_output_aliases=aliases,
        compiler_params=_cp(("parallel",)))(*args)


def _pack_local(fam, shards, dtype):
    parts = []
    for n, rows, prows, transposed in fam:
        a = shards[n].T if transposed else shards[n]
        parts.append(jnp.pad(a.astype(dtype), ((0, prows - rows), (0, 0))))
    return jnp.concatenate(parts, axis=0)


def _unpack_gathered(fam, gathered):
    out, off = {}, 0
    for n, rows, prows, _ in fam:
        out[n] = gathered[:, off:off + rows].reshape(N_DEV * rows, gathered.shape[2])
        off += prows
    return out


def _pack_full_grads(fam, grads):
    segs = []
    for n, rows, prows, _ in fam:
        g = grads[n].astype(BF16)
        g = g.reshape(N_DEV, rows, g.shape[1])
        segs.append(jnp.pad(g, ((0, 0), (0, prows - rows), (0, 0))))
    return jnp.concatenate(segs, axis=1)


def _pack_local_layers(fam, src, dtype):
    parts = []
    for n, rows, prows, transposed in fam:
        a = src[n].transpose(0, 2, 1) if transposed else src[n]
        parts.append(jnp.pad(a.astype(dtype), ((0, 0), (0, prows - rows), (0, 0))))
    return jnp.concatenate(parts, axis=1)


def _unpack_local_layers(fam, blob):
    out, off = {}, 0
    for n, rows, prows, transposed in fam:
        a = blob[:, off:off + rows]
        out[n] = a.transpose(0, 2, 1) if transposed else a
        off += prows
    return out


def _pack_small(vals):
    flat = jnp.concatenate([vals[n].reshape(-1).astype(F32) for n, _ in SMALL])
    flat = jnp.pad(flat, (0, SMALL_ROWS * 128 - flat.shape[0]))
    return flat.reshape(SMALL_ROWS, 128)


def _unpack_small(flat2d):
    flat = flat2d.reshape(-1)
    out, off = {}, 0
    for n, k in SMALL:
        out[n] = flat[off:off + DEPTH * k].reshape(DEPTH, k)
        off += DEPTH * k
    return out


def _pad_w_in_t(w_t):
    return jnp.concatenate([w_t[:2560], w_t[2576:IN_COLS], w_t[2560:2576],
                            jnp.zeros((IN_PAD - IN_COLS, w_t.shape[1]), w_t.dtype)], axis=0)


def _unpad_w_in_t(g_t):
    return jnp.concatenate([g_t[:2560], g_t[3648:3664], g_t[2560:3648]], axis=0)


def _pad_head_rows(w_t):
    k = w_t.shape[1]
    v = jnp.pad(w_t.reshape(MLA_HEADS, MLA_QK, k), ((0, 0), (0, MLA_QK_PAD - MLA_QK), (0, 0)))
    return v.reshape(MLA_HEADS * MLA_QK_PAD, k)


def _unpad_head_rows(g_t):
    k = g_t.shape[1]
    return g_t.reshape(MLA_HEADS, MLA_QK_PAD, k)[:, :MLA_QK].reshape(MLA_HEADS * MLA_QK, k)


def _row(v):
    return v.reshape(1, -1).astype(F32)


def layer_fwd(x, memf, w, p, tabs, bsz, next_blocks):
    t_rows = x.shape[0]
    seq = t_rows // bsz
    r = {}
    r["x"] = x
    h1 = rms_fwd(x, 0, D_MODEL, _row(p["attn_norm_g"]), 1, D_MODEL, BF16, "norm1_fwd")
    proj = mm([(h1, w["w_in"])], "nt", tn=768, name="in_proj")
    r["h1"], r["proj"] = h1, proj
    proj3 = proj.reshape(bsz, seq, IN_PAD)

    w8 = jnp.concatenate([p["conv_w_full"], p["conv_b"].reshape(1, -1), jnp.zeros((3, CONV_DIM), F32)], 0)
    xbc = conv_fwd(proj3, w8, "conv_fwd")
    tail = proj[:, TAIL_COL:TAIL_COL + 128]
    dt_raw = tail[:, 64:80].reshape(bsz, seq, 16)
    y, states = ssd_fwd(xbc, dt_raw, _row(p["dt_bias"]), _row(p["a_log"]), _row(p["d_skip"]), "ssd_fwd")
    y2 = y.reshape(t_rows, SSD_INNER)
    yg = ew(_gate_fwd, [(y2, 0), (proj, 0)], SSD_INNER, (F32,), "ssd_gate_fwd")
    y_ssd = rms_fwd(yg, 0, SSD_INNER, _row(p["ssd_norm_g"]), 2, 512, BF16, "ssd_norm_fwd")
    r.update(w8=w8, xbc=xbc, dt_raw=dt_raw, states=states, y=y2, yg=yg)

    qn = rms_fwd(proj, 5, Q_LORA, _row(p["q_a_norm_g"]), 1, Q_LORA, BF16, "qa_norm_fwd")
    kvn = rms_fwd(proj, 6, Q_LORA, _row(p["kv_a_norm_g"]), 1, Q_LORA, BF16, "kva_norm_fwd")
    q_raw = mm([(qn, w["w_q_b"])], "nt", name="q_b_proj")
    kv = mm([(kvn, w["w_kv_b"])], "nt", name="kv_b_proj")
    lane = jnp.arange(128) < 64
    krope = jnp.where(lane[None, :], tail, 0.0)
    k_raw = jnp.concatenate([kv.reshape(t_rows, MLA_HEADS, 256)[:, :, :128],
                             jnp.broadcast_to(krope[:, None, :], (t_rows, MLA_HEADS, 128))], axis=-1)
    k_raw = k_raw.reshape(t_rows, MLA_HEADS * MLA_QK_PAD)
    gq = _row(jnp.tile(jnp.pad(p["mla_q_norm_g"], (0, 64)), MLA_HEADS))
    gk = _row(jnp.tile(jnp.pad(p["mla_k_norm_g"], (0, 64)), MLA_HEADS))
    qh = rms_fwd(q_raw, 0, 2048, gq, MLA_HEADS, MLA_QK, F32, "q_head_norm_fwd")
    kh = rms_fwd(k_raw, 0, 2048, gk, MLA_HEADS, MLA_QK, F32, "k_head_norm_fwd")
    q_fin = rope(qh.reshape(bsz, seq, 2048), *tabs, False, BF16, "rope_fwd")
    k_fin = rope(kh.reshape(bsz, seq, 2048), *tabs, False, BF16, "rope_fwd")
    kv3 = kv.reshape(bsz, seq, 2048)
    riding = next_blocks is not None
    o, lse, gathered = attn_fwd(q_fin, k_fin, kv3, heads=MLA_HEADS, dk=MLA_QK_PAD, dv=MLA_V, v_mul=2, v_off=1,
                                causal=True, scale=MLA_QK ** -0.5, tq=512, tk=512,
                                name="mla_attn_fwd_gather_near" if riding else "mla_attn_fwd",
                                rider=gather_near_rider(next_blocks) if riding else None)
    r.update(qn=qn, kvn=kvn, q_raw=q_raw, k_raw=k_raw, gq=gq, gk=gk, q_fin=q_fin, k_fin=k_fin, kv3=kv3, o=o, lse=lse)
    mixed_in = jnp.concatenate([y_ssd, o.reshape(t_rows, SSD_INNER).astype(BF16)], axis=1)
    x2 = mm([(mixed_in, w["w_out"])], "nn", extras=(x,), epilogue=_add_res, name="out_proj")
    r.update(mixed_in=mixed_in, x2=x2)

    h2 = rms_fwd(x2, 0, D_MODEL, _row(p["xattn_norm_g"]), 1, D_MODEL, BF16, "norm2_fwd")
    mn = rms_fwd(memf, 0, D_MODEL, _row(p["mem_norm_g"]), 1, D_MODEL, BF16, "mem_norm_fwd")
    xq_raw = mm([(h2, w["w_xq"])], "nn", name="xq_proj")
    xk_raw = mm([(mn, w["w_xk"])], "nn", name="xk_proj")
    xv = mm([(mn, w["w_xv"])], "nn", name="xv_proj")
    gxq = _row(jnp.tile(p["xq_norm_g"], X_HEADS))
    gxk = _row(jnp.tile(p["xk_norm_g"], X_HEADS))
    xq = rms_fwd(xq_raw, 0, X_INNER, gxq, X_HEADS, X_HEAD_DIM, BF16, "xq_norm_fwd")
    xk = rms_fwd(xk_raw, 0, X_INNER, gxk, X_HEADS, X_HEAD_DIM, BF16, "xk_norm_fwd")
    mlen = memf.shape[0] // bsz
    xq3, xk3, xv3 = xq.reshape(bsz, seq, X_INNER), xk.reshape(bsz, mlen, X_INNER), xv.reshape(bsz, mlen, X_INNER)
    xo, xlse, _ = attn_fwd(xq3, xk3, xv3, heads=X_HEADS, dk=X_HEAD_DIM, dv=X_HEAD_DIM, v_mul=1, v_off=0,
                           causal=False, scale=X_HEAD_DIM ** -0.5, tq=512, tk=256, name="xattn_fwd")
    xo2 = xo.reshape(t_rows, X_INNER)
    x3 = mm([(xo2, w["w_xo"])], "nt", extras=(x2,), epilogue=_add_res, name="xo_proj")
    r.update(h2=h2, mn=mn, xq_raw=xq_raw, xk_raw=xk_raw, gxq=gxq, gxk=gxk, xq3=xq3, xk3=xk3, xv3=xv3,
             xo=xo, xlse=xlse, x3=x3)

    h3 = rms_fwd(x3, 0, D_MODEL, _row(p["ffn_norm_g"]), 1, D_MODEL, BF16, "norm3_fwd")
    if riding:
        (gate, up, act), gathered = mm(
            [(h3, w["w_gate"]), (h3, w["w_up"])], "nt", epilogue=_swiglu_fwd, separate=True,
            out_dtypes=(BF16, BF16, BF16), name="gate_up_proj_gather_diagonal",
            rider=gather_diagonal_rider(next_blocks, gathered))
        x4, gathered = mm([(act, w["w_down"])], "nn", extras=(x3,), epilogue=_add_res, tm=512,
                          name="down_proj_gather_forward", rider=gather_forward_rider(gathered))
    else:
        gate, up, act = mm([(h3, w["w_gate"]), (h3, w["w_up"])], "nt", epilogue=_swiglu_fwd, separate=True,
                           out_dtypes=(BF16, BF16, BF16), name="gate_up_proj")
        x4 = mm([(act, w["w_down"])], "nn", extras=(x3,), epilogue=_add_res, tm=512, name="down_proj")
        gathered = None
    r.update(h3=h3, gate=gate, up=up, act=act)
    return x4, r, gathered


def layer_bwd(dx4, r, memf, w, p, tabs, bsz, pending):
    t_rows = dx4.shape[0]
    seq = t_rows // bsz
    big, small = {}, {}
    rider_xy = rider_diag = None

    if pending is None:
        dgate, dup = mm([(dx4, w["w_down"])], "nt", extras=(r["gate"], r["up"]), epilogue=_swiglu_bwd,
                        out_dtypes=(BF16, BF16), name="down_proj_bwd")
    else:
        (dgate, dup), from_sib = mm([(dx4, w["w_down"])], "nt", extras=(r["gate"], r["up"]), epilogue=_swiglu_bwd,
                                    out_dtypes=(BF16, BF16), name="down_proj_bwd_sibling_exchange",
                                    rider=sibling_stage_rider(pending))
        partials = [chip_partial(b, s, "grad_chip_partial", tr) for b, s, tr in zip(pending, from_sib, PARTIAL_TR)]
        first_half, second_half = _halves(partials)
    if pending is None:
        big["w_down"] = mm([(r["act"], dx4)], "tn", out_dtypes=(BF16,), name="down_proj_wgrad")
    else:
        big["w_down"], recv_diag = mm([(r["act"], dx4)], "tn", out_dtypes=(BF16,),
                                      name="down_proj_wgrad_diagonal_exchange",
                                      rider=chip_stage_rider(partials, True, first_half))
    big["w_gate"] = mm([(dgate, r["h3"])], "tn", out_dtypes=(BF16,), tm=512, tn=1024, name="gate_proj_wgrad")
    big["w_up"] = mm([(dup, r["h3"])], "tn", out_dtypes=(BF16,), tm=512, tn=1024, name="gate_proj_wgrad")
    if pending is None:
        dh3 = mm([(dgate, w["w_gate"]), (dup, w["w_up"])], "nn", tm=512, tn=256, name="gate_up_proj_bwd")
    else:
        dh3, recv_xy = mm([(dgate, w["w_gate"]), (dup, w["w_up"])], "nn", tm=512, tn=256,
                          name="gate_up_proj_bwd_near_exchange",
                          rider=chip_stage_rider(partials, False, first_half))
        rider_xy = chip_stage_rider(partials, False, second_half, received=recv_xy)
        rider_diag = chip_stage_rider(partials, True, second_half, received=recv_diag)
    dx3, g = rms_bwd(r["x3"], 0, D_MODEL, _row(p["ffn_norm_g"]), 1, D_MODEL, dh3, dx4, "norm3_bwd")
    small["ffn_norm_g"] = g[0]

    xo2 = r["xo"].reshape(t_rows, X_INNER)
    dxo = mm([(dx3, w["w_xo"])], "nn", name="xo_proj_bwd")
    big["w_xo"] = mm([(dx3, xo2)], "tn", out_dtypes=(BF16,), tm=512, name="xo_proj_wgrad")
    dxq, dxk, dxv, _, _ = attn_bwd(r["xq3"], r["xk3"], r["xv3"], dxo.reshape(bsz, seq, X_INNER), 0, r["xo"],
                                   r["xlse"], heads=X_HEADS, dk=X_HEAD_DIM, dv=X_HEAD_DIM, v_mul=1, v_off=0,
                                   causal=False, scale=X_HEAD_DIM ** -0.5, tq=512, tk=256, name="xattn_bwd")
    mrows = memf.shape[0]
    dxq_raw, g = rms_bwd(r["xq_raw"], 0, X_INNER, r["gxq"], X_HEADS, X_HEAD_DIM, dxq.reshape(t_rows, X_INNER),
                         None, "xq_norm_bwd")
    small["xq_norm_g"] = g.reshape(X_HEADS, X_HEAD_DIM).sum(0)
    dxk_raw, g = rms_bwd(r["xk_raw"], 0, X_INNER, r["gxk"], X_HEADS, X_HEAD_DIM, dxk.reshape(mrows, X_INNER),
                         None, "xk_norm_bwd")
    small["xk_norm_g"] = g.reshape(X_HEADS, X_HEAD_DIM).sum(0)
    dxv2 = dxv.reshape(mrows, X_INNER)
    big["w_xq"] = mm([(r["h2"], dxq_raw)], "tn", out_dtypes=(BF16,), name="xq_proj_wgrad")
    big["w_xk"] = mm([(r["mn"], dxk_raw)], "tn", out_dtypes=(BF16,), name="xkv_proj_wgrad")
    big["w_xv"] = mm([(r["mn"], dxv2)], "tn", out_dtypes=(BF16,), name="xkv_proj_wgrad")
    dh2 = mm([(dxq_raw, w["w_xq"])], "nt", name="xq_proj_bwd")
    dmn = mm([(dxk_raw, w["w_xk"]), (dxv2, w["w_xv"])], "nt", name="xkv_proj_bwd")
    _, g = rms_bwd(memf, 0, D_MODEL, _row(p["mem_norm_g"]), 1, D_MODEL, dmn, None, "mem_norm_bwd")
    small["mem_norm_g"] = g[0]
    dx2, g = rms_bwd(r["x2"], 0, D_MODEL, _row(p["xattn_norm_g"]), 1, D_MODEL, dh2, dx3, "norm2_bwd")
    small["xattn_norm_g"] = g[0]

    dmixed = mm([(dx2, w["w_out"])], "nt", name="out_proj_bwd")
    big["w_out"] = mm([(r["mixed_in"], dx2)], "tn", out_dtypes=(BF16,), name="out_proj_wgrad")
    dmixed3 = dmixed.reshape(bsz, seq, D_MODEL)

    dq_fin, dk_fin, dv, recv_xy, recv_diag = attn_bwd(
        r["q_fin"], r["k_fin"], r["kv3"], dmixed3, MLA_HEADS, r["o"], r["lse"], heads=MLA_HEADS, dk=MLA_QK_PAD,
        dv=MLA_V, v_mul=2, v_off=1, causal=True, scale=MLA_QK ** -0.5, tq=512, tk=512,
        name="mla_attn_bwd" if pending is None else "mla_attn_bwd_chip_exchange",
        rider_dq=rider_xy, rider_dkv=rider_diag)
    dqh = rope(dq_fin, *tabs, True, F32, "rope_bwd").reshape(t_rows, 2048)
    dkh = rope(dk_fin, *tabs, True, F32, "rope_bwd").reshape(t_rows, 2048)
    dq_raw, g = rms_bwd(r["q_raw"], 0, 2048, r["gq"], MLA_HEADS, MLA_QK, dqh, None, "q_head_norm_bwd")
    small["mla_q_norm_g"] = g.reshape(MLA_HEADS, MLA_QK_PAD)[:, :MLA_QK].sum(0)
    dk_raw, g = rms_bwd(r["k_raw"], 0, 2048, r["gk"], MLA_HEADS, MLA_QK, dkh, None, "q_head_norm_bwd")
    small["mla_k_norm_g"] = g.reshape(MLA_HEADS, MLA_QK_PAD)[:, :MLA_QK].sum(0)
    big["w_q_b"] = _unpad_head_rows(mm([(dq_raw, r["qn"])], "tn", out_dtypes=(BF16,), tm=512,
                                       name="qkv_b_proj_wgrad"))
    dqn = mm([(dq_raw, w["w_q_b"])], "nn", name="qkv_b_proj_bwd")
    dk_raw3 = dk_raw.reshape(t_rows, MLA_HEADS, MLA_QK_PAD)
    dkv = jnp.concatenate([dk_raw3[:, :, :128], dv.reshape(t_rows, MLA_HEADS, MLA_V)], axis=-1)
    dkv = dkv.reshape(t_rows, 2048)
    dkrope = dk_raw3[:, :, 128:192].sum(1)
    big["w_kv_b"] = mm([(dkv, r["kvn"])], "tn", out_dtypes=(BF16,), tm=512, name="qkv_b_proj_wgrad")
    dkvn = mm([(dkv, w["w_kv_b"])], "nn", name="qkv_b_proj_bwd")
    dq_a, g = rms_bwd(r["proj"], 5, Q_LORA, _row(p["q_a_norm_g"]), 1, Q_LORA, dqn, None, "qa_norm_bwd")
    small["q_a_norm_g"] = g[0]
    dkv_a, g = rms_bwd(r["proj"], 6, Q_LORA, _row(p["kv_a_norm_g"]), 1, Q_LORA, dkvn, None, "qa_norm_bwd")
    small["kv_a_norm_g"] = g[0]

    dyg, g = rms_bwd(r["yg"], 0, SSD_INNER, _row(p["ssd_norm_g"]), 2, 512, dmixed, None, "ssd_norm_bwd")
    small["ssd_norm_g"] = g[0]
    dy, dz = ew(_gate_bwd, [(dyg, 0), (r["y"], 0), (r["proj"], 0)], SSD_INNER, (F32, F32), "ssd_gate_bwd")
    dxbc_c, ddt_raw, g_dtb, g_alog, g_dsk = ssd_bwd(
        r["xbc"], r["dt_raw"], _row(p["dt_bias"]), _row(p["a_log"]), _row(p["d_skip"]), r["states"],
        dy.reshape(bsz, seq, SSD_INNER), "ssd_bwd")
    small["dt_bias"], small["a_log"], small["d_skip"] = g_dtb[0], g_alog[0], g_dsk[0]
    dxbc, dw8 = conv_bwd(r["proj"].reshape(bsz, seq, IN_PAD), r["w8"], dxbc_c, "conv_bwd")
    small["conv_w"] = dw8[:4]
    small["conv_b"] = dw8[4]

    dtail = jnp.concatenate([dkrope, ddt_raw.reshape(t_rows, 16), jnp.zeros((t_rows, 48 + 128), F32)], axis=1)
    dproj = jnp.concatenate([dz.astype(BF16), dxbc.reshape(t_rows, CONV_DIM).astype(BF16), dq_a.astype(BF16),
                             dkv_a.astype(BF16), dtail.astype(BF16)], axis=1)
    big["w_in"] = _unpad_w_in_t(mm([(dproj, r["h1"])], "tn", out_dtypes=(BF16,), tm=768, tn=1024,
                                   name="in_proj_wgrad"))
    dh1 = mm([(dproj, w["w_in"])], "nn", name="in_proj_bwd")
    dx, g = rms_bwd(r["x"], 0, D_MODEL, _row(p["attn_norm_g"]), 1, D_MODEL, dh1, dx2, "norm1_bwd")
    small["attn_norm_g"] = g[0]
    if pending is None:
        recv_xy = recv_diag = None
    return dx, big, small, recv_xy, recv_diag


def _rope_tables(positions):
    inv_freq = 1.0 / (ROPE_THETA ** (jnp.arange(0, 64, 2, dtype=F32) / 64))
    ang = positions.astype(F32)[..., None] * inv_freq
    c, s = jnp.cos(ang), jnp.sin(ang)
    z32, z64 = jnp.zeros_like(c), jnp.zeros(c.shape[:-1] + (64,), F32)
    return (jnp.concatenate([c, c, z64], -1), jnp.concatenate([-s, z32, z64], -1),
            jnp.concatenate([z32, s, z64], -1))


def kernel(x, mem, positions, attn_norm_g, w_in, conv_w, conv_b, dt_bias, a_log, d_skip, ssd_norm_g, q_a_norm_g, w_q_b, kv_a_norm_g, w_kv_b, mla_q_norm_g, mla_k_norm_g, w_out, xattn_norm_g, mem_norm_g, w_xq, w_xk, w_xv, xq_norm_g, xk_norm_g, w_xo, ffn_norm_g, w_gate, w_up, w_down, loss_target, m_attn_norm_g, m_w_in, m_conv_w, m_conv_b, m_dt_bias, m_a_log, m_d_skip, m_ssd_norm_g, m_q_a_norm_g, m_w_q_b, m_kv_a_norm_g, m_w_kv_b, m_mla_q_norm_g, m_mla_k_norm_g, m_w_out, m_xattn_norm_g, m_mem_norm_g, m_w_xq, m_w_xk, m_w_xv, m_xq_norm_g, m_xk_norm_g, m_w_xo, m_ffn_norm_g, m_w_gate, m_w_up, m_w_down, v_attn_norm_g, v_w_in, v_conv_w, v_conv_b, v_dt_bias, v_a_log, v_d_skip, v_ssd_norm_g, v_q_a_norm_g, v_w_q_b, v_kv_a_norm_g, v_w_kv_b, v_mla_q_norm_g, v_mla_k_norm_g, v_w_out, v_xattn_norm_g, v_mem_norm_g, v_w_xq, v_w_xk, v_w_xv, v_xq_norm_g, v_xk_norm_g, v_w_xo, v_ffn_norm_g, v_w_gate, v_w_up, v_w_down):
    args = dict(locals())
    weights = {n: args[n] for n in WEIGHT_ORDER}
    mom_m = {n: args["m_" + n] for n in WEIGHT_ORDER}
    mom_v = {n: args["v_" + n] for n in WEIGHT_ORDER}
    bsz, seq, _ = x.shape
    t_rows = bsz * seq
    xf = x.reshape(t_rows, D_MODEL)
    memf = mem.reshape(-1, D_MODEL)
    tabs = _rope_tables(positions)

    full = []
    def local_blocks(l):
        return [_pack_local(fam, {n: weights[n][l] for n, _, _, _ in fam}, BF16) for fam, _ in FAMS]

    def full_weights(gathered):
        wl = {}
        for (fam, _), g in zip(FAMS, gathered):
            wl.update(_unpack_gathered(fam, g))
        wl["w_in"] = _pad_w_in_t(wl["w_in"])
        wl["w_q_b"] = _pad_head_rows(wl["w_q_b"])
        return wl

    full.append(full_weights(all_gather(local_blocks(0), "weight_all_gather")))
    small_p = [{n: weights[n][l] for n, _ in SMALL} for l in range(DEPTH)]
    cw = all_gather([conv_w.reshape(-1, 128)], "conv_w_all_gather")[0].reshape(N_DEV, DEPTH, 4, -1)
    cw = cw.transpose(1, 2, 0, 3).reshape(DEPTH, 4, CONV_DIM)
    for l in range(DEPTH):
        small_p[l]["conv_w_full"] = cw[l]

    saved = []
    h = xf
    for l in range(DEPTH):
        h, res, gathered = layer_fwd(h, memf, full[l], small_p[l], tabs, bsz,
                                     local_blocks(l + 1) if l + 1 < DEPTH else None)
        saved.append(res)
        if gathered is not None:
            full.append(full_weights(gathered))
    dy, loss_part = loss_head(h, loss_target.reshape(t_rows, D_MODEL), "loss_head")
    loss = lax.psum(loss_part[0, 0], ("x", "y", "c"))

    small_g = [None] * DEPTH
    outs = {k: {} for k in ("g", "d", "m", "v")}
    stacked = [[_pack_local_layers(fam, src, F32) for src in (weights, mom_m, mom_v)] for fam, _ in FAMS]
    results = [None] * len(FAMS)

    def update_layer(l, recv_xy, recv_diag):
        for a, (_, width) in enumerate(FAMS):
            results[a] = adamw([recv_xy[a], recv_diag[a]], *stacked[a], l, results[a], "adamw_big",
                               tr=48 if width == 2048 else 64)

    dh, pending = dy, None
    for l in reversed(range(DEPTH)):
        dh, big_g, small_g[l], recv_xy, recv_diag = layer_bwd(dh, saved[l], memf, full[l], small_p[l], tabs, bsz,
                                                             pending)
        if pending is not None:
            update_layer(l + 1, recv_xy, recv_diag)
        pending = [_pack_full_grads(fam, big_g) for fam, _ in FAMS]
    from_sib = comm_call([sibling_stage_rider(pending)], "grad_sibling_exchange")[0]
    partials = [chip_partial(b, s, "grad_chip_partial", tr) for b, s, tr in zip(pending, from_sib, PARTIAL_TR)]
    whole = [(0, part.shape[1]) for part in partials]
    recv_xy, recv_diag = comm_call([chip_stage_rider(partials, False, whole), chip_stage_rider(partials, True, whole)],
                                   "grad_chip_exchange")
    update_layer(0, recv_xy, recv_diag)
    for ki, key in enumerate(("g", "d", "m", "v")):
        for a, (fam, _) in enumerate(FAMS):
            outs[key].update(_unpack_local_layers(fam, results[a][ki]))

    sg = _pack_small({n: jnp.stack([small_g[l][n] for l in range(DEPTH)]) for n, _ in SMALL})
    cg = jnp.stack([small_g[l]["conv_w"] for l in range(DEPTH)]).reshape(CONVW_ROWS, 128)
    parts = all_gather([jnp.concatenate([sg, cg], axis=0)], "small_grad_all_gather")[0]
    sm = adamw([parts[:, :SMALL_ROWS]], _pack_small({n: weights[n] for n, _ in SMALL})[None],
               _pack_small({n: mom_m[n] for n, _ in SMALL})[None], _pack_small({n: mom_v[n] for n, _ in SMALL})[None],
               0, None, "adamw_small", tr=SMALL_ROWS)
    for ki, key in enumerate(("g", "d", "m", "v")):
        outs[key].update(_unpack_small(sm[ki][0]))
    my_idx = 4 * lax.axis_index("x") + 2 * lax.axis_index("y") + lax.axis_index("c")
    cparts = parts[:, SMALL_ROWS:].reshape(N_DEV, DEPTH, 4, N_DEV, -1)
    cparts = lax.dynamic_index_in_dim(cparts, my_idx, axis=3, keepdims=False).reshape(N_DEV, -1, 128)
    cres = adamw([cparts], conv_w.reshape(1, -1, 128), m_conv_w.reshape(1, -1, 128), v_conv_w.reshape(1, -1, 128),
                 0, None, "adamw_conv_w")
    for ki, key in enumerate(("g", "d", "m", "v")):
        outs[key]["conv_w"] = cres[ki].reshape(conv_w.shape)

    grad_x = dh.reshape(bsz, seq, D_MODEL)
    return (loss, grad_x, *[outs["g"][n] for n in WEIGHT_ORDER], *[outs["d"][n] for n in WEIGHT_ORDER],
            *[outs["m"][n] for n in WEIGHT_ORDER], *[outs["v"][n] for n in WEIGHT_ORDER])
```

```python
import functools

import jax
import jax.numpy as jnp
from jax import lax
from jax.experimental import pallas as pl
from jax.experimental.pallas import tpu as pltpu

F32 = jnp.float32
BF16 = jnp.bfloat16

D_MODEL = 2048
DEPTH = 4
N_DEV = 8
SSD_INNER = 1024
SSD_HEADS = 16
SSD_PAIRS = 8
SSD_STATE = 128
SSD_CHUNK = 128
CONV_DIM = 1536
MLA_HEADS = 8
MLA_QK = 192
MLA_QK_PAD = 256
MLA_V = 128
Q_LORA = 512
X_HEADS = 4
X_HEAD_DIM = 128
X_INNER = 512
FFN = 5632
IN_COLS = 3664
IN_PAD = 3840
TAIL_COL = 3584
ROPE_THETA = 10000.0
EPS = 1e-6
NEG = -1e30
VMEM_LIMIT = 56 * 1024 * 1024
HI = lax.Precision.HIGHEST

ADAM_LR = 0.001
ADAM_B1 = 0.9
ADAM_B2 = 0.999
ADAM_EPS = 1e-08
ADAM_WD = 0.01
ADAM_STEP = 10

SMALL_ROWS = 392
CONVW_ROWS = 192

FAM_A = (("w_in", 458, 464, True), ("w_out", 256, 256, False), ("w_gate", 704, 704, True),
         ("w_up", 704, 704, True), ("w_down", 704, 704, False))
FAM_B = (("w_q_b", 192, 192, True), ("w_kv_b", 256, 256, True), ("w_xq", 256, 256, False),
         ("w_xk", 256, 256, False), ("w_xv", 256, 256, False), ("w_xo", 256, 256, True))
FAMS = ((FAM_A, 2048), (FAM_B, 512))
PARTIAL_TR = (944, 1472)
ADAMW_TR = {"w_in": 512, "w_gate": 256, "w_up": 256, "w_down": 176, "w_out": 128, "w_q_b": 512, "w_kv_b": 512,
            "w_xq": 256, "w_xk": 256, "w_xv": 256, "w_xo": 512}
SMALL = (
    ("attn_norm_g", 2048), ("conv_b", 1536), ("dt_bias", 16), ("a_log", 16), ("d_skip", 16),
    ("ssd_norm_g", 1024), ("q_a_norm_g", 512), ("kv_a_norm_g", 512), ("mla_q_norm_g", 192),
    ("mla_k_norm_g", 192), ("xattn_norm_g", 2048), ("mem_norm_g", 2048), ("xq_norm_g", 128),
    ("xk_norm_g", 128), ("ffn_norm_g", 2048),
)
WEIGHT_ORDER = ("attn_norm_g", "w_in", "conv_w", "conv_b", "dt_bias", "a_log", "d_skip", "ssd_norm_g",
                "q_a_norm_g", "w_q_b", "kv_a_norm_g", "w_kv_b", "mla_q_norm_g", "mla_k_norm_g", "w_out",
                "xattn_norm_g", "mem_norm_g", "w_xq", "w_xk", "w_xv", "xq_norm_g", "xk_norm_g", "w_xo",
                "ffn_norm_g", "w_gate", "w_up", "w_down")


def _cp(sem=None):
    if sem is None:
        return pltpu.CompilerParams(vmem_limit_bytes=VMEM_LIMIT)
    return pltpu.CompilerParams(vmem_limit_bytes=VMEM_LIMIT, dimension_semantics=sem)


def _tile(n, pref):
    if n <= pref:
        return n
    t = (pref // 128) * 128
    while t > 128 and n % t:
        t -= 128
    return t


def _sigmoid(v):
    return 1.0 / (1.0 + jnp.exp(-v))


class Rider:
    def __init__(self, ins, out_shapes, sems, start, finish, aliases=None):
        self.ins, self.out_shapes, self.sems = list(ins), list(out_shapes), list(sems)
        self.start, self.finish = start, finish
        self.aliases = dict(aliases or {})


def _attach(rider, body, grid, in_specs, out_specs, out_shape, scratch, args):
    n_in, n_out, n_scr = len(in_specs), len(out_specs), len(scratch)
    r_in, r_out = len(rider.ins), len(rider.out_shapes)

    def wrapped(*refs):
        ins, rins = refs[:n_in], refs[n_in:n_in + r_in]
        o0 = n_in + r_in
        outs, routs = refs[o0:o0 + n_out], refs[o0 + n_out:o0 + n_out + r_out]
        s0 = o0 + n_out + r_out
        scr, rsems = refs[s0:s0 + n_scr], refs[s0 + n_scr:]
        first = last = None
        for d, g in enumerate(grid):
            i = pl.program_id(d)
            first = (i == 0) if first is None else jnp.logical_and(first, i == 0)
            last = (i == g - 1) if last is None else jnp.logical_and(last, i == g - 1)
        if first is None:
            rider.start(rins, routs, rsems)
            body(*ins, *outs, *scr)
            rider.finish(rins, routs, rsems)
            return
        pl.when(first)(lambda: rider.start(rins, routs, rsems))
        body(*ins, *outs, *scr)
        pl.when(last)(lambda: rider.finish(rins, routs, rsems))

    hbm = pl.BlockSpec(memory_space=pl.ANY)
    io_aliases = {n_in + ri: n_out + ro for ri, ro in rider.aliases.items()}
    return (wrapped, list(in_specs) + [hbm] * r_in, list(out_specs) + [hbm] * r_out,
            list(out_shape) + rider.out_shapes, list(scratch) + rider.sems, list(args) + rider.ins, io_aliases)


def comm_call(riders, name):
    def start(rins, routs, sems):
        for r, (i0, i1, o0, o1, s0, s1) in zip(riders, spans):
            r.start(rins[i0:i1], routs[o0:o1], sems[s0:s1])

    def finish(rins, routs, sems):
        for r, (i0, i1, o0, o1, s0, s1) in zip(riders, spans):
            r.finish(rins[i0:i1], routs[o0:o1], sems[s0:s1])

    spans, i, o, s = [], 0, 0, 0
    for r in riders:
        spans.append((i, i + len(r.ins), o, o + len(r.out_shapes), s, s + len(r.sems)))
        i, o, s = i + len(r.ins), o + len(r.out_shapes), s + len(r.sems)
    aliases = {i0 + ri: o0 + ro for r, (i0, _, o0, _, _, _) in zip(riders, spans) for ri, ro in r.aliases.items()}
    both = Rider([a for r in riders for a in r.ins], [a for r in riders for a in r.out_shapes],
                 [a for r in riders for a in r.sems], start, finish, aliases)
    body, in_specs, out_specs, out_shape, scratch, args, io_aliases = _attach(
        both, lambda: None, (), [], [], [], [], [])
    outs = pl.pallas_call(body, name=name, in_specs=in_specs, out_specs=out_specs, out_shape=out_shape,
                          scratch_shapes=scratch, input_output_aliases=io_aliases)(*args)
    return [outs[a:b] for (_, _, a, b, _, _) in spans]


def rms_fwd(x, col_blk, width, g, groups, n_valid, out_dtype, name, tr=512):
    rows = x.shape[0]
    tr = min(tr, rows)
    wg = width // groups

    def body(x_ref, g_ref, y_ref):
        for gi in range(groups):
            sl = slice(gi * wg, (gi + 1) * wg)
            xv = x_ref[:, sl].astype(F32)
            r = lax.rsqrt(jnp.sum(xv * xv, axis=-1, keepdims=True) * (1.0 / n_valid) + EPS)
            y_ref[:, sl] = ((xv * r) * g_ref[:, sl]).astype(out_dtype)

    return pl.pallas_call(
        body, name=name, grid=(rows // tr,),
        in_specs=[pl.BlockSpec((tr, width), lambda i: (i, col_blk)), pl.BlockSpec((1, width), lambda i: (0, 0))],
        out_specs=pl.BlockSpec((tr, width), lambda i: (i, 0)),
        out_shape=jax.ShapeDtypeStruct((rows, width), out_dtype), compiler_params=_cp(("parallel",)))(x, g)


def rms_bwd(x, col_blk, width, g, groups, n_valid, dy, add, name, tr=512):
    rows = x.shape[0]
    tr = min(tr, rows)
    wg = width // groups
    has_add = add is not None

    def body(*refs):
        if has_add:
            x_ref, g_ref, dy_ref, add_ref, dx_ref, dg_ref = refs
        else:
            x_ref, g_ref, dy_ref, dx_ref, dg_ref = refs

        @pl.when(pl.program_id(0) == 0)
        def _():
            dg_ref[...] = jnp.zeros(dg_ref.shape, F32)

        for gi in range(groups):
            sl = slice(gi * wg, (gi + 1) * wg)
            xv = x_ref[:, sl].astype(F32)
            dyv = dy_ref[:, sl].astype(F32)
            r = lax.rsqrt(jnp.sum(xv * xv, axis=-1, keepdims=True) * (1.0 / n_valid) + EPS)
            xh = xv * r
            dyg = dyv * g_ref[:, sl]
            c = jnp.sum(dyg * xh, axis=-1, keepdims=True) * (1.0 / n_valid)
            dx = r * (dyg - xh * c)
            if has_add:
                dx = dx + add_ref[:, sl]
            dx_ref[:, sl] = dx
            dg_ref[:, sl] += jnp.sum(dyv * xh, axis=0, keepdims=True)

    in_specs = [pl.BlockSpec((tr, width), lambda i: (i, col_blk)), pl.BlockSpec((1, width), lambda i: (0, 0)),
                pl.BlockSpec((tr, width), lambda i: (i, 0))]
    args = [x, g, dy]
    if has_add:
        in_specs.append(pl.BlockSpec((tr, width), lambda i: (i, 0)))
        args.append(add)
    return pl.pallas_call(
        body, name=name, grid=(rows // tr,), in_specs=in_specs,
        out_specs=[pl.BlockSpec((tr, width), lambda i: (i, 0)), pl.BlockSpec((1, width), lambda i: (0, 0))],
        out_shape=[jax.ShapeDtypeStruct((rows, width), F32), jax.ShapeDtypeStruct((1, width), F32)],
        compiler_params=_cp(("arbitrary",)))(*args)


def mm(pairs, mode, *, extras=(), epilogue=None, out_dtypes=(F32,), separate=False, tm=1024, tn=512, name,
       rider=None):
    a0, b0 = pairs[0]
    if mode == "nn":
        m_dim, n_dim = a0.shape[0], b0.shape[1]
        dn = (((1,), (0,)), ((), ()))
    elif mode == "nt":
        m_dim, n_dim = a0.shape[0], b0.shape[0]
        dn = (((1,), (1,)), ((), ()))
    else:
        m_dim, n_dim = a0.shape[1], b0.shape[1]
        dn = (((0,), (0,)), ((), ()))
    tm = _tile(m_dim, tm)
    tn = _tile(n_dim, tn)
    n_pairs = len(pairs)
    n_extra = len(extras)

    def body(*refs):
        ins = refs[:2 * n_pairs]
        ex = refs[2 * n_pairs:2 * n_pairs + n_extra]
        outs = refs[2 * n_pairs + n_extra:]
        accs = []
        for k in range(n_pairs):
            a = ins[2 * k][...].astype(BF16)
            b = ins[2 * k + 1][...].astype(BF16)
            accs.append(lax.dot_general(a, b, dn, preferred_element_type=F32))
        if not separate:
            total = accs[0]
            for extra_acc in accs[1:]:
                total = total + extra_acc
            accs = [total]
        res = epilogue(*accs, *[e[...] for e in ex]) if epilogue is not None else tuple(accs)
        for o_ref, r in zip(outs, res):
            o_ref[...] = r.astype(o_ref.dtype)

    in_specs, args = [], []
    for a, b in pairs:
        if mode == "nn":
            in_specs += [pl.BlockSpec((tm, a.shape[1]), lambda i, j: (i, 0)),
                         pl.BlockSpec((b.shape[0], tn), lambda i, j: (0, j))]
        elif mode == "nt":
            in_specs += [pl.BlockSpec((tm, a.shape[1]), lambda i, j: (i, 0)),
                         pl.BlockSpec((tn, b.shape[1]), lambda i, j: (j, 0))]
        else:
            in_specs += [pl.BlockSpec((a.shape[0], tm), lambda i, j: (0, i)),
                         pl.BlockSpec((b.shape[0], tn), lambda i, j: (0, j))]
        args += [a, b]
    for e in extras:
        in_specs.append(pl.BlockSpec((tm, tn), lambda i, j: (i, j)))
        args.append(e)
    grid = (m_dim // tm, n_dim // tn)
    out_specs = [pl.BlockSpec((tm, tn), lambda i, j: (i, j)) for _ in out_dtypes]
    out_shape = [jax.ShapeDtypeStruct((m_dim, n_dim), dt) for dt in out_dtypes]
    scratch, sem, io_aliases = [], ("parallel", "parallel"), {}
    if rider is not None:
        body, in_specs, out_specs, out_shape, scratch, args, io_aliases = _attach(
            rider, body, grid, in_specs, out_specs, out_shape, scratch, args)
        sem = ("arbitrary", "arbitrary")
    outs = pl.pallas_call(body, name=name, grid=grid, in_specs=in_specs, out_specs=out_specs, out_shape=out_shape,
                          scratch_shapes=scratch, input_output_aliases=io_aliases, compiler_params=_cp(sem))(*args)
    main = outs[0] if len(out_dtypes) == 1 else outs[:len(out_dtypes)]
    return main if rider is None else (main, outs[len(out_dtypes):])


def _add_res(acc, res):
    return (acc + res,)


def _swiglu_fwd(gate, up):
    gb = gate.astype(BF16).astype(F32)
    ub = up.astype(BF16).astype(F32)
    return gate, up, gb * _sigmoid(gb) * ub


def _swiglu_bwd(dact, gate, up):
    gv = gate.astype(F32)
    uv = up.astype(F32)
    sg = _sigmoid(gv)
    return dact * uv * (sg * (1.0 + gv * (1.0 - sg))), dact * (gv * sg)


def ew(fn, ins, width, out_dtypes, name, tr=512):
    rows = ins[0][0].shape[0]
    tr = min(tr, rows)
    n_in = len(ins)

    def body(*refs):
        res = fn(*[r[...].astype(F32) for r in refs[:n_in]])
        for o_ref, r in zip(refs[n_in:], res):
            o_ref[...] = r.astype(o_ref.dtype)

    in_specs = [pl.BlockSpec((tr, width), functools.partial(lambda i, cb: (i, cb), cb=cb)) for _, cb in ins]
    outs = pl.pallas_call(
        body, name=name, grid=(rows // tr,), in_specs=in_specs,
        out_specs=[pl.BlockSpec((tr, width), lambda i: (i, 0)) for _ in out_dtypes],
        out_shape=[jax.ShapeDtypeStruct((rows, width), dt) for dt in out_dtypes],
        compiler_params=_cp(("parallel",)))(*[a for a, _ in ins])
    return outs[0] if len(out_dtypes) == 1 else outs


def _gate_fwd(y, z):
    return (y * (z * _sigmoid(z)),)


def _gate_bwd(dyg, y, z):
    sg = _sigmoid(z)
    return dyg * (z * sg), dyg * y * (sg * (1.0 + z * (1.0 - sg)))


CONV_CB = 512
CONV_OFF = 2


def _conv_pre(x, w_ref, row):
    acc = x * w_ref[pl.ds(3, 1), :] + w_ref[pl.ds(4, 1), :]
    shifted = []
    for j in (1, 2, 3):
        xs = jnp.where(row >= j, pltpu.roll(x, j, 0), 0.0)
        shifted.append(xs)
        acc = acc + xs * w_ref[pl.ds(3 - j, 1), :]
    return acc, shifted


def conv_fwd(proj3, w8, name):
    bsz, seq, _ = proj3.shape

    def body(x_ref, w_ref, o_ref):
        x = x_ref[0]
        row = lax.broadcasted_iota(jnp.int32, x.shape, 0)
        pre, _ = _conv_pre(x, w_ref, row)
        o_ref[0] = pre * _sigmoid(pre)

    return pl.pallas_call(
        body, name=name, grid=(bsz, CONV_DIM // CONV_CB),
        in_specs=[pl.BlockSpec((1, seq, CONV_CB), lambda b, c: (b, 0, c + CONV_OFF)),
                  pl.BlockSpec((8, CONV_CB), lambda b, c: (0, c))],
        out_specs=pl.BlockSpec((1, seq, CONV_CB), lambda b, c: (b, 0, c)),
        out_shape=jax.ShapeDtypeStruct((bsz, seq, CONV_DIM), F32),
        compiler_params=_cp(("parallel", "parallel")))(proj3, w8)


def conv_bwd(proj3, w8, dout, name):
    bsz, seq, _ = proj3.shape

    def body(x_ref, w_ref, do_ref, dx_ref, dw_ref):
        @pl.when(pl.program_id(1) == 0)
        def _():
            dw_ref[...] = jnp.zeros(dw_ref.shape, F32)

        x = x_ref[0]
        row = lax.broadcasted_iota(jnp.int32, x.shape, 0)
        pre, shifted = _conv_pre(x, w_ref, row)
        sg = _sigmoid(pre)
        dpre = do_ref[0] * (sg * (1.0 + pre * (1.0 - sg)))
        dx = dpre * w_ref[pl.ds(3, 1), :]
        for j in (1, 2, 3):
            fut = jnp.where(row < seq - j, pltpu.roll(dpre, seq - j, 0), 0.0)
            dx = dx + fut * w_ref[pl.ds(3 - j, 1), :]
            dw_ref[pl.ds(3 - j, 1), :] += jnp.sum(dpre * shifted[j - 1], axis=0, keepdims=True)
        dw_ref[pl.ds(3, 1), :] += jnp.sum(dpre * x, axis=0, keepdims=True)
        dw_ref[pl.ds(4, 1), :] += jnp.sum(dpre, axis=0, keepdims=True)
        dx_ref[0] = dx

    return pl.pallas_call(
        body, name=name, grid=(CONV_DIM // CONV_CB, bsz),
        in_specs=[pl.BlockSpec((1, seq, CONV_CB), lambda c, b: (b, 0, c + CONV_OFF)),
                  pl.BlockSpec((8, CONV_CB), lambda c, b: (0, c)),
                  pl.BlockSpec((1, seq, CONV_CB), lambda c, b: (b, 0, c))],
        out_specs=[pl.BlockSpec((1, seq, CONV_CB), lambda c, b: (b, 0, c)),
                   pl.BlockSpec((8, CONV_CB), lambda c, b: (0, c))],
        out_shape=[jax.ShapeDtypeStruct((bsz, seq, CONV_DIM), F32), jax.ShapeDtypeStruct((8, CONV_DIM), F32)],
        compiler_params=_cp(("parallel", "arbitrary")))(proj3, w8, dout)


def _softplus(v):
    t = jnp.exp(-jnp.abs(v))
    small = t * (1.0 - t * (0.5 - t * (1.0 / 3.0)))
    return jnp.maximum(v, 0.0) + jnp.where(t < 0.01, small, jnp.log(1.0 + t))


def _ssd_chunk_prelude(dt_ref, dtb_ref, alog_ref):
    L = SSD_CHUNK
    raw = dt_ref[0] + dtb_ref[...]
    dt = _softplus(raw)
    a_neg = -jnp.exp(alog_ref[...])
    a = dt * a_neg
    r_i = lax.broadcasted_iota(jnp.int32, (L, L), 0)
    c_i = lax.broadcasted_iota(jnp.int32, (L, L), 1)
    tri_low = (r_i >= c_i).astype(F32)
    tri_up = (r_i <= c_i).astype(F32)
    acs = jnp.dot(tri_low, a, precision=HI, preferred_element_type=F32)
    acs_t = lax.dot_general(a, tri_up, (((0,), (0,)), ((), ())), precision=HI,
                            preferred_element_type=F32)
    tot = jnp.sum(a, axis=0, keepdims=True)
    return raw, dt, a_neg, acs, acs_t, tot, r_i, c_i, tri_up


def _col(arr16, e):
    lane = lax.broadcasted_iota(jnp.int32, arr16.shape, 1)
    return jnp.sum(jnp.where(lane == e, arr16, 0.0), axis=1, keepdims=True)


def _rowvec(arr_t, e):
    sub = lax.broadcasted_iota(jnp.int32, arr_t.shape, 0)
    return jnp.sum(jnp.where(sub == e, arr_t, 0.0), axis=0, keepdims=True)


def _pair(lo_mask, v0, v1):
    return jnp.where(lo_mask, v0, v1)


def ssd_fwd(xbc, dt_raw, dtb, alog, dsk, name):
    bsz, seq, _ = xbc.shape
    L = SSD_CHUNK
    nc = seq // L

    def body(x_ref, b_ref, c_ref, dt_ref, dtb_ref, alog_ref, dsk_ref, y_ref, st_ref, h_sc):
        @pl.when(pl.program_id(1) == 0)
        def _():
            h_sc[...] = jnp.zeros(h_sc.shape, F32)

        _, dt, _, acs, acs_t, tot, r_i, c_i, _ = _ssd_chunk_prelude(dt_ref, dtb_ref, alog_ref)
        causal = r_i >= c_i
        lo = lax.broadcasted_iota(jnp.int32, (1, 128), 1) < 64
        dskv = dsk_ref[...]
        for grp in range(2):
            bg = b_ref[0, :, grp * 128:(grp + 1) * 128].astype(BF16)
            cg = c_ref[0, :, grp * 128:(grp + 1) * 128].astype(BF16)
            cb = lax.dot_general(cg, bg, (((1,), (1,)), ((), ())), preferred_element_type=F32)
            for kk in range(4):
                k = grp * 4 + kk
                e0, e1 = 2 * k, 2 * k + 1
                sl = slice(k * 128, (k + 1) * 128)
                xp = x_ref[0, :, sl]
                cols = [_col(acs, e0), _col(acs, e1)]
                dtp = _pair(lo, _col(dt, e0), _col(dt, e1))
                xdt = xp * dtp
                ydiag = jnp.zeros((L, 128), F32)
                for sub, e in enumerate((e0, e1)):
                    diff = cols[sub] - _rowvec(acs_t, e)
                    lam = jnp.exp(jnp.where(causal, diff, NEG))
                    m_e = (cb * lam).astype(BF16)
                    xm = jnp.where(lo if sub == 0 else jnp.logical_not(lo), xdt, 0.0).astype(BF16)
                    ydiag = ydiag + jnp.dot(m_e, xm, preferred_element_type=F32)
                h = h_sc[k]
                st_ref[0, 0, k] = h
                wp = _pair(lo, jnp.exp(cols[0]), jnp.exp(cols[1]))
                yoff = jnp.dot(cg, h.astype(BF16), preferred_element_type=F32) * wp
                tot0, tot1 = _col(tot, e0), _col(tot, e1)
                up = _pair(lo, jnp.exp(tot0 - cols[0]), jnp.exp(tot1 - cols[1]))
                etot = _pair(lo, jnp.exp(tot0), jnp.exp(tot1))
                h_sc[k] = h * etot + lax.dot_general(bg, (xdt * up).astype(BF16), (((0,), (0,)), ((), ())),
                                                     preferred_element_type=F32)
                dskp = _pair(lo, _col(dskv, e0), _col(dskv, e1))
                y_ref[0, :, sl] = ydiag + yoff + xp * dskp

    p16 = pl.BlockSpec((1, 16), lambda b, c: (0, 0))
    return pl.pallas_call(
        body, name=name, grid=(bsz, nc),
        in_specs=[pl.BlockSpec((1, L, 1024), lambda b, c: (b, c, 0)),
                  pl.BlockSpec((1, L, 256), lambda b, c: (b, c, 4)),
                  pl.BlockSpec((1, L, 256), lambda b, c: (b, c, 5)),
                  pl.BlockSpec((1, L, 16), lambda b, c: (b, c, 0)), p16, p16, p16],
        out_specs=[pl.BlockSpec((1, L, 1024), lambda b, c: (b, c, 0)),
                   pl.BlockSpec((1, 1, SSD_PAIRS, 128, 128), lambda b, c: (b, c, 0, 0, 0))],
        out_shape=[jax.ShapeDtypeStruct((bsz, seq, SSD_INNER), F32),
                   jax.ShapeDtypeStruct((bsz, nc, SSD_PAIRS, 128, 128), F32)],
        scratch_shapes=[pltpu.VMEM((SSD_PAIRS, 128, 128), F32)],
        compiler_params=_cp(("parallel", "arbitrary")))(xbc, xbc, xbc, dt_raw, dtb, alog, dsk)


def ssd_bwd(xbc, dt_raw, dtb, alog, dsk, states, dy, name):
    bsz, seq, _ = xbc.shape
    L = SSD_CHUNK
    nc = seq // L

    def body(x_ref, b_ref, c_ref, dt_ref, dtb_ref, alog_ref, dsk_ref, st_ref, dy_ref,
             dxbc_ref, ddt_ref, gdtb_ref, galog_ref, gdsk_ref, dh_sc):
        first = jnp.logical_and(pl.program_id(0) == 0, pl.program_id(1) == 0)

        @pl.when(first)
        def _():
            gdtb_ref[...] = jnp.zeros(gdtb_ref.shape, F32)
            galog_ref[...] = jnp.zeros(galog_ref.shape, F32)
            gdsk_ref[...] = jnp.zeros(gdsk_ref.shape, F32)

        @pl.when(pl.program_id(1) == 0)
        def _():
            dh_sc[...] = jnp.zeros(dh_sc.shape, F32)

        raw, dt, a_neg, acs, acs_t, tot, r_i, c_i, tri_up = _ssd_chunk_prelude(dt_ref, dtb_ref, alog_ref)
        causal = r_i >= c_i
        causal_t = c_i >= r_i
        lo = lax.broadcasted_iota(jnp.int32, (1, 128), 1) < 64
        hi = jnp.logical_not(lo)
        lane16 = lax.broadcasted_iota(jnp.int32, (L, 16), 1)
        lane16r = lax.broadcasted_iota(jnp.int32, (1, 16), 1)
        last_row = lax.broadcasted_iota(jnp.int32, (L, 1), 0) == L - 1
        dskv = dsk_ref[...]
        ds16 = jnp.zeros((L, 16), F32)
        ddt16 = jnp.zeros((L, 16), F32)
        gdsk = jnp.zeros((1, 16), F32)

        def hsum(t, mask):
            return jnp.sum(jnp.where(mask, t, 0.0), axis=1, keepdims=True)

        for grp in range(2):
            gs = slice(grp * 128, (grp + 1) * 128)
            bg = b_ref[0, :, gs].astype(BF16)
            cg = c_ref[0, :, gs].astype(BF16)
            cb = lax.dot_general(cg, bg, (((1,), (1,)), ((), ())), preferred_element_type=F32)
            cbt = lax.dot_general(bg, cg, (((1,), (1,)), ((), ())), preferred_element_type=F32)
            dcb = jnp.zeros((L, L), F32)
            dcbt = jnp.zeros((L, L), F32)
            dc_g = jnp.zeros((L, 128), F32)
            db_g = jnp.zeros((L, 128), F32)
            for kk in range(4):
                k = grp * 4 + kk
                e0, e1 = 2 * k, 2 * k + 1
                sl = slice(k * 128, (k + 1) * 128)
                xp = x_ref[0, :, sl]
                dyp = dy_ref[0, :, sl]
                h = st_ref[0, 0, k]
                dh = dh_sc[k]
                cols = [_col(acs, e0), _col(acs, e1)]
                tots = [_col(tot, e0), _col(tot, e1)]
                dtp = _pair(lo, _col(dt, e0), _col(dt, e1))
                xdt = xp * dtp
                wp = _pair(lo, jnp.exp(cols[0]), jnp.exp(cols[1]))
                up = _pair(lo, jnp.exp(tots[0] - cols[0]), jnp.exp(tots[1] - cols[1]))
                etot = _pair(lo, jnp.exp(tots[0]), jnp.exp(tots[1]))
                hb = h.astype(BF16)
                dhb = dh.astype(BF16)
                yoff = jnp.dot(cg, hb, preferred_element_type=F32) * wp
                dyw = (dyp * wp).astype(BF16)
                dc_g = dc_g + lax.dot_general(dyw, hb, (((1,), (1,)), ((), ())), preferred_element_type=F32)
                dh_in = lax.dot_general(cg, dyw, (((0,), (0,)), ((), ())), preferred_element_type=F32) + dh * etot
                q_mat = jnp.dot(bg, dhb, preferred_element_type=F32)
                db_g = db_g + lax.dot_general((xdt * up).astype(BF16), dhb, (((1,), (1,)), ((), ())),
                                              preferred_element_type=F32)
                dxdt = up * q_mat
                t_u = q_mat * xdt * up
                t_w = dyp * yoff
                t_h = jnp.sum(dh * h, axis=0, keepdims=True) * etot
                for sub, e in enumerate((e0, e1)):
                    msk = lo if sub == 0 else hi
                    rowv = _rowvec(acs_t, e)
                    lam = jnp.exp(jnp.where(causal, cols[sub] - rowv, NEG))
                    lam_t = jnp.exp(jnp.where(causal_t, rowv - cols[sub], NEG))
                    xm = jnp.where(msk, xdt, 0.0).astype(BF16)
                    dym = jnp.where(msk, dyp, 0.0).astype(BF16)
                    g_mat = lax.dot_general(dym, xm, (((1,), (1,)), ((), ())), preferred_element_type=F32)
                    g_t = lax.dot_general(xm, dym, (((1,), (1,)), ((), ())), preferred_element_type=F32)
                    gl = g_mat * lam
                    glt = g_t * lam_t
                    dcb = dcb + gl
                    dcbt = dcbt + glt
                    e_r = jnp.sum(gl * cb, axis=1, keepdims=True)
                    e_c = jnp.sum(glt * cbt, axis=1, keepdims=True)
                    dxdt = dxdt + jnp.dot((cbt * lam_t).astype(BF16), dym, preferred_element_type=F32)
                    u_rows = hsum(t_u, msk)
                    ds_tot = jnp.sum(u_rows, axis=0, keepdims=True) + hsum(t_h, msk)
                    ds_col = e_r - e_c + hsum(t_w, msk) - u_rows + jnp.where(last_row, ds_tot, 0.0)
                    ds16 = ds16 + jnp.where(lane16 == e, ds_col, 0.0)
                dxp = dxdt * xp
                dyx = dyp * xp
                for sub, e in enumerate((e0, e1)):
                    msk = lo if sub == 0 else hi
                    ddt16 = ddt16 + jnp.where(lane16 == e, hsum(dxp, msk), 0.0)
                    gdsk = gdsk + jnp.where(lane16r == e, jnp.sum(hsum(dyx, msk), axis=0, keepdims=True), 0.0)
                dskp = _pair(lo, _col(dskv, e0), _col(dskv, e1))
                dxbc_ref[0, :, sl] = dxdt * dtp + dyp * dskp
                dh_sc[k] = dh_in
            dc_g = dc_g + jnp.dot(dcb.astype(BF16), bg, preferred_element_type=F32)
            db_g = db_g + jnp.dot(dcbt.astype(BF16), cg, preferred_element_type=F32)
            dxbc_ref[0, :, 1024 + grp * 128:1024 + (grp + 1) * 128] = db_g
            dxbc_ref[0, :, 1280 + grp * 128:1280 + (grp + 1) * 128] = dc_g
        da16 = jnp.dot(tri_up, ds16, precision=HI, preferred_element_type=F32)
        ddt16 = ddt16 + da16 * a_neg
        d_aneg = jnp.sum(da16 * dt, axis=0, keepdims=True)
        ddt_raw = ddt16 * _sigmoid(raw)
        ddt_ref[0] = ddt_raw
        gdtb_ref[...] += jnp.sum(ddt_raw, axis=0, keepdims=True)
        galog_ref[...] += d_aneg * a_neg
        gdsk_ref[...] += gdsk

    p16 = pl.BlockSpec((1, 16), lambda b, c: (0, 0))
    rev = lambda b, c: (b, nc - 1 - c, 0)
    return pl.pallas_call(
        body, name=name, grid=(bsz, nc),
        in_specs=[pl.BlockSpec((1, L, 1024), rev),
                  pl.BlockSpec((1, L, 256), lambda b, c: (b, nc - 1 - c, 4)),
                  pl.BlockSpec((1, L, 256), lambda b, c: (b, nc - 1 - c, 5)),
                  pl.BlockSpec((1, L, 16), rev), p16, p16, p16,
                  pl.BlockSpec((1, 1, SSD_PAIRS, 128, 128), lambda b, c: (b, nc - 1 - c, 0, 0, 0)),
                  pl.BlockSpec((1, L, 1024), rev)],
        out_specs=[pl.BlockSpec((1, L, CONV_DIM), rev), pl.BlockSpec((1, L, 16), rev), p16, p16, p16],
        out_shape=[jax.ShapeDtypeStruct((bsz, seq, CONV_DIM), F32), jax.ShapeDtypeStruct((bsz, seq, 16), F32),
                   jax.ShapeDtypeStruct((1, 16), F32), jax.ShapeDtypeStruct((1, 16), F32),
                   jax.ShapeDtypeStruct((1, 16), F32)],
        scratch_shapes=[pltpu.VMEM((SSD_PAIRS, 128, 128), F32)],
        compiler_params=_cp(("arbitrary", "arbitrary")))(xbc, xbc, xbc, dt_raw, dtb, alog, dsk, states, dy)


TAIL_BLK = TAIL_COL // 128


def _rope_apply(v, c, sa, sb):
    return v * c + pltpu.roll(v, 96, 1) * sa + pltpu.roll(v, 32, 1) * sb


def _rope_transposed(v, c, sa, sb):
    return v * c + pltpu.roll(v * sa, 32, 1) + pltpu.roll(v * sb, 96, 1)


def _head_rstd(lo, hi):
    ss = jnp.sum(lo * lo, axis=-1, keepdims=True) + jnp.sum(hi * hi, axis=-1, keepdims=True)
    return lax.rsqrt(ss * (1.0 / MLA_QK) + EPS)


def qk_prep_fwd(q_raw3, kv3, proj3, gq, gk, tabs, name, ts=256):
    bsz, seq, width = q_raw3.shape
    ts = min(ts, seq)

    def body(q_ref, kv_ref, tail_ref, gq_ref, gk_ref, c_ref, sa_ref, sb_ref, qo_ref, ko_ref):
        c, sa, sb = c_ref[0], sa_ref[0], sb_ref[0]
        k_hi = jnp.where(lax.broadcasted_iota(jnp.int32, (1, 128), 1) < 64, tail_ref[0], 0.0)
        for h in range(MLA_HEADS):
            lo_sl, hi_sl = slice(h * 256, h * 256 + 128), slice(h * 256 + 128, h * 256 + 256)
            for lo, hi, g_ref, o_ref in ((q_ref[0, :, lo_sl], q_ref[0, :, hi_sl], gq_ref, qo_ref),
                                         (kv_ref[0, :, lo_sl], k_hi, gk_ref, ko_ref)):
                r = _head_rstd(lo, hi)
                o_ref[0, :, lo_sl] = ((lo * r) * g_ref[:, lo_sl]).astype(BF16)
                o_ref[0, :, hi_sl] = _rope_apply((hi * r) * g_ref[:, hi_sl], c, sa, sb).astype(BF16)

    wide = pl.BlockSpec((1, ts, width), lambda b, i: (b, i, 0))
    tab = pl.BlockSpec((1, ts, 128), lambda b, i: (b, i, 0))
    gain = pl.BlockSpec((1, width), lambda b, i: (0, 0))
    return pl.pallas_call(
        body, name=name, grid=(bsz, seq // ts),
        in_specs=[wide, wide, pl.BlockSpec((1, ts, 128), lambda b, i: (b, i, TAIL_BLK)), gain, gain, tab, tab, tab],
        out_specs=[wide, wide], out_shape=[jax.ShapeDtypeStruct(q_raw3.shape, BF16)] * 2,
        compiler_params=_cp(("parallel", "parallel")))(q_raw3, kv3, proj3, gq, gk, *tabs)


def qk_prep_bwd(q_raw3, kv3, proj3, gq, gk, tabs, dq, dk, dv, name, ts=256):
    bsz, seq, width = q_raw3.shape
    ts = min(ts, seq)

    def body(q_ref, kv_ref, tail_ref, gq_ref, gk_ref, c_ref, sa_ref, sb_ref, dq_ref, dk_ref, dv_ref,
             dqr_ref, dkv_ref, dkr_ref, dgq_ref, dgk_ref):
        @pl.when(jnp.logical_and(pl.program_id(0) == 0, pl.program_id(1) == 0))
        def _():
            dgq_ref[...] = jnp.zeros(dgq_ref.shape, F32)
            dgk_ref[...] = jnp.zeros(dgk_ref.shape, F32)

        c, sa, sb = c_ref[0], sa_ref[0], sb_ref[0]
        k_hi = jnp.where(lax.broadcasted_iota(jnp.int32, (1, 128), 1) < 64, tail_ref[0], 0.0)
        dk_rope = jnp.zeros(k_hi.shape, F32)
        for h in range(MLA_HEADS):
            lo_sl, hi_sl = slice(h * 256, h * 256 + 128), slice(h * 256 + 128, h * 256 + 256)
            for is_k, lo, hi, g_ref, d_ref, dg_ref in ((False, q_ref[0, :, lo_sl], q_ref[0, :, hi_sl], gq_ref, dq_ref, dgq_ref),
                                                       (True, kv_ref[0, :, lo_sl], k_hi, gk_ref, dk_ref, dgk_ref)):
                r = _head_rstd(lo, hi)
                xh_lo, xh_hi = lo * r, hi * r
                dy_lo = d_ref[0, :, lo_sl]
                dy_hi = _rope_transposed(d_ref[0, :, hi_sl], c, sa, sb)
                dyg_lo, dyg_hi = dy_lo * g_ref[:, lo_sl], dy_hi * g_ref[:, hi_sl]
                cc = (jnp.sum(dyg_lo * xh_lo, axis=-1, keepdims=True)
                      + jnp.sum(dyg_hi * xh_hi, axis=-1, keepdims=True)) * (1.0 / MLA_QK)
                dx_lo, dx_hi = r * (dyg_lo - xh_lo * cc), r * (dyg_hi - xh_hi * cc)
                dg_ref[:, lo_sl] += jnp.sum(dy_lo * xh_lo, axis=0, keepdims=True)
                dg_ref[:, hi_sl] += jnp.sum(dy_hi * xh_hi, axis=0, keepdims=True)
                if is_k:
                    dkv_ref[0, :, lo_sl] = dx_lo
                    dkv_ref[0, :, hi_sl] = dv_ref[0, :, h * 128:(h + 1) * 128]
                    dk_rope = dk_rope + dx_hi
                else:
                    dqr_ref[0, :, lo_sl] = dx_lo
                    dqr_ref[0, :, hi_sl] = dx_hi
        dkr_ref[0] = dk_rope

    wide = pl.BlockSpec((1, ts, width), lambda b, i: (b, i, 0))
    tab = pl.BlockSpec((1, ts, 128), lambda b, i: (b, i, 0))
    gain = pl.BlockSpec((1, width), lambda b, i: (0, 0))
    return pl.pallas_call(
        body, name=name, grid=(bsz, seq // ts),
        in_specs=[wide, wide, pl.BlockSpec((1, ts, 128), lambda b, i: (b, i, TAIL_BLK)), gain, gain, tab, tab, tab,
                  wide, wide, pl.BlockSpec((1, ts, 1024), lambda b, i: (b, i, 0))],
        out_specs=[wide, wide, tab, gain, gain],
        out_shape=[jax.ShapeDtypeStruct(q_raw3.shape, F32), jax.ShapeDtypeStruct(q_raw3.shape, F32),
                   jax.ShapeDtypeStruct((bsz, seq, 128), F32), jax.ShapeDtypeStruct((1, width), F32),
                   jax.ShapeDtypeStruct((1, width), F32)],
        compiler_params=_cp(("arbitrary", "arbitrary")))(q_raw3, kv3, proj3, gq, gk, *tabs, dq, dk, dv)


def _scores(q_ref, k_ref, scale, causal, i, j, tq, tk):
    s = lax.dot_general(q_ref[0].astype(BF16), k_ref[0].astype(BF16), (((1,), (1,)), ((), ())),
                        preferred_element_type=F32) * scale
    if not causal:
        return s, None
    r = i * tq + lax.broadcasted_iota(jnp.int32, (tq, tk), 0)
    c = j * tk + lax.broadcasted_iota(jnp.int32, (tq, tk), 1)
    mask = r >= c
    return jnp.where(mask, s, NEG), mask


def attn_fwd(q, k, v, *, heads, dk, dv, v_mul, v_off, causal, scale, tq, tk, name, rider=None):
    bsz, sq, _ = q.shape
    sk = k.shape[1]
    tq, tk = min(tq, sq), min(tk, sk)
    nq, nk = sq // tq, sk // tk

    def body(q_ref, k_ref, v_ref, o_ref, lse_ref, m_sc, l_sc, acc_sc):
        i, j = pl.program_id(2), pl.program_id(3)

        @pl.when(j == 0)
        def _():
            m_sc[...] = jnp.full(m_sc.shape, NEG, F32)
            l_sc[...] = jnp.zeros(l_sc.shape, F32)
            acc_sc[...] = jnp.zeros(acc_sc.shape, F32)

        def compute(masked=False):
            s, _ = _scores(q_ref, k_ref, scale, masked, i, j, tq, tk)
            m_prev = m_sc[...]
            m_new = jnp.maximum(m_prev, jnp.max(s, axis=1, keepdims=True))
            p = jnp.exp(s - m_new)
            alpha = jnp.exp(m_prev - m_new)
            l_sc[...] = alpha * l_sc[...] + jnp.sum(p, axis=1, keepdims=True)
            acc_sc[...] = alpha * acc_sc[...] + jnp.dot(p.astype(BF16), v_ref[0].astype(BF16),
                                                        preferred_element_type=F32)
            m_sc[...] = m_new

        if causal:
            pl.when(j < i)(compute)
            pl.when(j == i)(functools.partial(compute, True))
        else:
            compute()

        @pl.when(j == nk - 1)
        def _():
            o_ref[0] = acc_sc[...] / l_sc[...]
            lse_ref[0, 0] = m_sc[...] + jnp.log(l_sc[...])

    kvj = (lambda i, j: jnp.minimum(i, j)) if causal else (lambda i, j: j)
    grid = (bsz, heads, nq, nk)
    in_specs = [pl.BlockSpec((1, tq, dk), lambda b, h, i, j: (b, i, h)),
                pl.BlockSpec((1, tk, dk), lambda b, h, i, j: (b, kvj(i, j), h)),
                pl.BlockSpec((1, tk, dv), lambda b, h, i, j: (b, kvj(i, j), v_mul * h + v_off))]
    out_specs = [pl.BlockSpec((1, tq, dv), lambda b, h, i, j: (b, i, h)),
                 pl.BlockSpec((1, 1, tq, 1), lambda b, h, i, j: (b, h, i, 0))]
    out_shape = [jax.ShapeDtypeStruct((bsz, sq, heads * dv), F32), jax.ShapeDtypeStruct((bsz, heads, sq, 1), F32)]
    scratch = [pltpu.VMEM((tq, 1), F32), pltpu.VMEM((tq, 1), F32), pltpu.VMEM((tq, dv), F32)]
    args, sem, io_aliases = [q, k, v], ("parallel", "parallel", "parallel", "arbitrary"), {}
    if rider is not None:
        body, in_specs, out_specs, out_shape, scratch, args, io_aliases = _attach(
            rider, body, grid, in_specs, out_specs, out_shape, scratch, args)
        sem = ("arbitrary",) * 4
    outs = pl.pallas_call(body, name=name, grid=grid, in_specs=in_specs, out_specs=out_specs, out_shape=out_shape,
                          scratch_shapes=scratch, input_output_aliases=io_aliases, compiler_params=_cp(sem))(*args)
    return outs[0], outs[1], outs[2:]


def _attn_grads(q_ref, k_ref, v_ref, do_ref, o_ref, lse_ref, scale, causal, i, j, tq, tk):
    s, mask = _scores(q_ref, k_ref, scale, causal, i, j, tq, tk)
    p = jnp.exp(s - lse_ref[0, 0])
    if mask is not None:
        p = jnp.where(mask, p, 0.0)
    do = do_ref[0]
    dp = lax.dot_general(do.astype(BF16), v_ref[0].astype(BF16), (((1,), (1,)), ((), ())),
                         preferred_element_type=F32)
    delta = jnp.sum(do * o_ref[0], axis=1, keepdims=True)
    ds = p * (dp - delta) * scale
    return p, ds


def attn_bwd(q, k, v, do, do_off, o, lse, *, heads, dk, dv, v_mul, v_off, causal, scale, tq, tk, name,
             rider_dq=None, rider_dkv=None):
    bsz, sq, _ = q.shape
    sk = k.shape[1]
    tq, tk = min(tq, sq), min(tk, sk)
    nq, nk = sq // tq, sk // tk

    def dq_body(q_ref, k_ref, v_ref, do_ref, o_ref, lse_ref, dq_ref, acc_sc):
        i, j = pl.program_id(2), pl.program_id(3)

        @pl.when(j == 0)
        def _():
            acc_sc[...] = jnp.zeros(acc_sc.shape, F32)

        def compute(masked=False):
            _, ds = _attn_grads(q_ref, k_ref, v_ref, do_ref, o_ref, lse_ref, scale, masked, i, j, tq, tk)
            acc_sc[...] += jnp.dot(ds.astype(BF16), k_ref[0].astype(BF16), preferred_element_type=F32)

        if causal:
            pl.when(j < i)(compute)
            pl.when(j == i)(functools.partial(compute, True))
        else:
            compute()

        @pl.when(j == nk - 1)
        def _():
            dq_ref[0] = acc_sc[...]

    kvj = (lambda i, j: jnp.minimum(i, j)) if causal else (lambda i, j: j)
    grid = (bsz, heads, nq, nk)
    in_specs = [pl.BlockSpec((1, tq, dk), lambda b, h, i, j: (b, i, h)),
                pl.BlockSpec((1, tk, dk), lambda b, h, i, j: (b, kvj(i, j), h)),
                pl.BlockSpec((1, tk, dv), lambda b, h, i, j: (b, kvj(i, j), v_mul * h + v_off)),
                pl.BlockSpec((1, tq, dv), lambda b, h, i, j: (b, i, h + do_off)),
                pl.BlockSpec((1, tq, dv), lambda b, h, i, j: (b, i, h)),
                pl.BlockSpec((1, 1, tq, 1), lambda b, h, i, j: (b, h, i, 0))]
    out_specs = [pl.BlockSpec((1, tq, dk), lambda b, h, i, j: (b, i, h))]
    out_shape = [jax.ShapeDtypeStruct((bsz, sq, heads * dk), F32)]
    scratch, args = [pltpu.VMEM((tq, dk), F32)], [q, k, v, do, o, lse]
    sem, io_aliases = ("parallel", "parallel", "parallel", "arbitrary"), {}
    if rider_dq is not None:
        dq_body, in_specs, out_specs, out_shape, scratch, args, io_aliases = _attach(
            rider_dq, dq_body, grid, in_specs, out_specs, out_shape, scratch, args)
        sem = ("arbitrary",) * 4
    dq_outs = pl.pallas_call(dq_body, name=name + "_dq", grid=grid, in_specs=in_specs, out_specs=out_specs,
                             out_shape=out_shape, scratch_shapes=scratch, input_output_aliases=io_aliases,
                             compiler_params=_cp(sem))(*args)
    dq = dq_outs[0]

    def dkv_body(q_ref, k_ref, v_ref, do_ref, o_ref, lse_ref, dk_ref, dv_ref, dk_sc, dv_sc):
        j, i = pl.program_id(2), pl.program_id(3)

        @pl.when(i == 0)
        def _():
            dk_sc[...] = jnp.zeros(dk_sc.shape, F32)
            dv_sc[...] = jnp.zeros(dv_sc.shape, F32)

        def compute(masked=False):
            p, ds = _attn_grads(q_ref, k_ref, v_ref, do_ref, o_ref, lse_ref, scale, masked, i, j, tq, tk)
            dv_sc[...] += lax.dot_general(p.astype(BF16), do_ref[0].astype(BF16), (((0,), (0,)), ((), ())),
                                          preferred_element_type=F32)
            dk_sc[...] += lax.dot_general(ds.astype(BF16), q_ref[0].astype(BF16), (((0,), (0,)), ((), ())),
                                          preferred_element_type=F32)

        if causal:
            pl.when(i > j)(compute)
            pl.when(i == j)(functools.partial(compute, True))
        else:
            compute()

        @pl.when(i == nq - 1)
        def _():
            dk_ref[0] = dk_sc[...]
            dv_ref[0] = dv_sc[...]

    qi = (lambda j, i: jnp.maximum(i, j)) if causal else (lambda j, i: i)
    grid = (bsz, heads, nk, nq)
    in_specs = [pl.BlockSpec((1, tq, dk), lambda b, h, j, i: (b, qi(j, i), h)),
                pl.BlockSpec((1, tk, dk), lambda b, h, j, i: (b, j, h)),
                pl.BlockSpec((1, tk, dv), lambda b, h, j, i: (b, j, v_mul * h + v_off)),
                pl.BlockSpec((1, tq, dv), lambda b, h, j, i: (b, qi(j, i), h + do_off)),
                pl.BlockSpec((1, tq, dv), lambda b, h, j, i: (b, qi(j, i), h)),
                pl.BlockSpec((1, 1, tq, 1), lambda b, h, j, i: (b, h, qi(j, i), 0))]
    out_specs = [pl.BlockSpec((1, tk, dk), lambda b, h, j, i: (b, j, h)),
                 pl.BlockSpec((1, tk, dv), lambda b, h, j, i: (b, j, h))]
    out_shape = [jax.ShapeDtypeStruct((bsz, sk, heads * dk), F32), jax.ShapeDtypeStruct((bsz, sk, heads * dv), F32)]
    scratch, args = [pltpu.VMEM((tk, dk), F32), pltpu.VMEM((tk, dv), F32)], [q, k, v, do, o, lse]
    sem, io_aliases = ("parallel", "parallel", "parallel", "arbitrary"), {}
    if rider_dkv is not None:
        dkv_body, in_specs, out_specs, out_shape, scratch, args, io_aliases = _attach(
            rider_dkv, dkv_body, grid, in_specs, out_specs, out_shape, scratch, args)
        sem = ("arbitrary",) * 4
    dkv_outs = pl.pallas_call(dkv_body, name=name + "_dkv", grid=grid, in_specs=in_specs, out_specs=out_specs,
                              out_shape=out_shape, scratch_shapes=scratch, input_output_aliases=io_aliases,
                              compiler_params=_cp(sem))(*args)
    return dq, dkv_outs[0], dkv_outs[1], dq_outs[1:], dkv_outs[2:]


def loss_head(y, target, name, tr=512):
    rows, width = y.shape
    tr = min(tr, rows)

    def body(y_ref, t_ref, dy_ref, l_ref):
        @pl.when(pl.program_id(0) == 0)
        def _():
            l_ref[...] = jnp.zeros(l_ref.shape, F32)

        err = y_ref[...] - t_ref[...]
        dy_ref[...] = err * (1.0 / width)
        l_ref[...] += jnp.sum(jnp.sum(err * err, axis=1, keepdims=True), axis=0, keepdims=True) * (0.5 / width)

    return pl.pallas_call(
        body, name=name, grid=(rows // tr,),
        in_specs=[pl.BlockSpec((tr, width), lambda i: (i, 0)), pl.BlockSpec((tr, width), lambda i: (i, 0))],
        out_specs=[pl.BlockSpec((tr, width), lambda i: (i, 0)), pl.BlockSpec((1, 1), lambda i: (0, 0))],
        out_shape=[jax.ShapeDtypeStruct((rows, width), F32), jax.ShapeDtypeStruct((1, 1), F32)],
        compiler_params=_cp(("arbitrary",)))(y, target)


def _mesh_pos():
    return lax.axis_index("x"), lax.axis_index("y"), lax.axis_index("c")


def all_gather(blocks, name):
    n_arr = len(blocks)

    def body(*refs):
        x_refs, out_refs = refs[:n_arr], refs[n_arr:2 * n_arr]
        send_sems, recv_sems, local_sems = refs[2 * n_arr:]
        x, y, c = _mesh_pos()
        me, sibling = (x, y, c), (x, y, 1 - c)
        chips = [(1 - x, y), (x, 1 - y), (1 - x, 1 - y)]

        def slot(a, px, py, pc):
            return out_refs[a].at[4 * px + 2 * py + pc]

        def copy(a, k, blk, to, src=None):
            return pltpu.make_async_remote_copy(
                src_ref=slot(a, *blk) if src is None else src, dst_ref=slot(a, *blk),
                send_sem=send_sems.at[7 * a + k], recv_sem=recv_sems.at[7 * a + k],
                device_id=to, device_id_type=pl.DeviceIdType.MESH)

        mine = [pltpu.make_async_copy(x_refs[a], slot(a, *me), local_sems.at[a]) for a in range(n_arr)]
        for cp in mine:
            cp.start()
        first = []
        for a in range(n_arr):
            first.append(copy(a, 0, me, sibling, src=x_refs[a]))
            first += [copy(a, 1 + j, me, (*chip, c), src=x_refs[a]) for j, chip in enumerate(chips)]
        for cp in first:
            cp.start()
        passed = []
        for j, chip in enumerate(chips):
            for a in range(n_arr):
                copy(a, 1 + j, (*chip, c), me).wait_recv()
                passed.append(copy(a, 4 + j, (*chip, c), sibling))
                passed[-1].start()
        for a in range(n_arr):
            copy(a, 0, sibling, me).wait_recv()
            for j, chip in enumerate(chips):
                copy(a, 4 + j, (*chip, 1 - c), me).wait_recv()
        for cp in first + passed:
            cp.wait_send()
        for cp in mine:
            cp.wait()

    return pl.pallas_call(
        body, name=name,
        out_shape=[jax.ShapeDtypeStruct((N_DEV,) + b.shape, b.dtype) for b in blocks],
        in_specs=[pl.BlockSpec(memory_space=pl.ANY)] * n_arr,
        out_specs=[pl.BlockSpec(memory_space=pl.ANY)] * n_arr,
        scratch_shapes=[pltpu.SemaphoreType.DMA((7 * n_arr,)), pltpu.SemaphoreType.DMA((7 * n_arr,)),
                        pltpu.SemaphoreType.DMA((n_arr,))],
    )(*blocks)


def _slot(px, py, pc):
    return 4 * px + 2 * py + pc


def gather_near_rider(blocks):
    n_arr = len(blocks)

    def parts(rins, routs, sems):
        send_sems, recv_sems, local_sems = sems
        x, y, c = _mesh_pos()
        peers = [(x, y, 1 - c), (1 - x, y, c), (x, 1 - y, c)]
        local = [pltpu.make_async_copy(rins[a], routs[a].at[_slot(x, y, c)], local_sems.at[a]) for a in range(n_arr)]
        sends = [pltpu.make_async_remote_copy(
            src_ref=rins[a], dst_ref=routs[a].at[_slot(x, y, c)], send_sem=send_sems.at[3 * a + k],
            recv_sem=recv_sems.at[3 * a + k], device_id=p, device_id_type=pl.DeviceIdType.MESH)
            for a in range(n_arr) for k, p in enumerate(peers)]
        recvs = [pltpu.make_async_remote_copy(
            src_ref=rins[a], dst_ref=routs[a].at[_slot(*p)], send_sem=send_sems.at[3 * a + k],
            recv_sem=recv_sems.at[3 * a + k], device_id=p, device_id_type=pl.DeviceIdType.MESH)
            for a in range(n_arr) for k, p in enumerate(peers)]
        return local, sends, recvs

    def start(rins, routs, sems):
        local, sends, _ = parts(rins, routs, sems)
        for cp in local + sends:
            cp.start()

    def finish(rins, routs, sems):
        local, sends, recvs = parts(rins, routs, sems)
        for cp in recvs:
            cp.wait_recv()
        for cp in sends:
            cp.wait_send()
        for cp in local:
            cp.wait()

    return Rider(blocks, [jax.ShapeDtypeStruct((N_DEV,) + b.shape, b.dtype) for b in blocks],
                 [pltpu.SemaphoreType.DMA((3 * n_arr,)), pltpu.SemaphoreType.DMA((3 * n_arr,)),
                  pltpu.SemaphoreType.DMA((n_arr,))], start, finish)


def gather_diagonal_rider(blocks, gathered):
    n_arr = len(blocks)

    def parts(rins, routs, sems):
        send_sems, recv_sems = sems
        x, y, c = _mesh_pos()
        p = (1 - x, 1 - y, c)
        sends = [pltpu.make_async_remote_copy(
            src_ref=rins[a], dst_ref=routs[a].at[_slot(x, y, c)], send_sem=send_sems.at[a],
            recv_sem=recv_sems.at[a], device_id=p, device_id_type=pl.DeviceIdType.MESH) for a in range(n_arr)]
        recvs = [pltpu.make_async_remote_copy(
            src_ref=rins[a], dst_ref=routs[a].at[_slot(*p)], send_sem=send_sems.at[a],
            recv_sem=recv_sems.at[a], device_id=p, device_id_type=pl.DeviceIdType.MESH) for a in range(n_arr)]
        return sends, recvs

    def start(rins, routs, sems):
        for cp in parts(rins, routs, sems)[0]:
            cp.start()

    def finish(rins, routs, sems):
        sends, recvs = parts(rins, routs, sems)
        for cp in recvs:
            cp.wait_recv()
        for cp in sends:
            cp.wait_send()

    return Rider(list(blocks) + list(gathered), [jax.ShapeDtypeStruct(g.shape, g.dtype) for g in gathered],
                 [pltpu.SemaphoreType.DMA((n_arr,)), pltpu.SemaphoreType.DMA((n_arr,))], start, finish,
                 aliases={n_arr + a: a for a in range(n_arr)})


def gather_forward_rider(gathered):
    n_arr = len(gathered)

    def parts(rins, routs, sems):
        send_sems, recv_sems = sems
        x, y, c = _mesh_pos()
        chips = [(1 - x, y), (x, 1 - y), (1 - x, 1 - y)]
        sends = [pltpu.make_async_remote_copy(
            src_ref=routs[a].at[_slot(px, py, c)], dst_ref=routs[a].at[_slot(px, py, c)],
            send_sem=send_sems.at[3 * a + k], recv_sem=recv_sems.at[3 * a + k],
            device_id=(x, y, 1 - c), device_id_type=pl.DeviceIdType.MESH)
            for a in range(n_arr) for k, (px, py) in enumerate(chips)]
        recvs = [pltpu.make_async_remote_copy(
            src_ref=routs[a].at[_slot(px, py, c)], dst_ref=routs[a].at[_slot(px, py, 1 - c)],
            send_sem=send_sems.at[3 * a + k], recv_sem=recv_sems.at[3 * a + k],
            device_id=(x, y, 1 - c), device_id_type=pl.DeviceIdType.MESH)
            for a in range(n_arr) for k, (px, py) in enumerate(chips)]
        return sends, recvs

    def start(rins, routs, sems):
        for cp in parts(rins, routs, sems)[0]:
            cp.start()

    def finish(rins, routs, sems):
        sends, recvs = parts(rins, routs, sems)
        for cp in recvs:
            cp.wait_recv()
        for cp in sends:
            cp.wait_send()

    return Rider(gathered, [jax.ShapeDtypeStruct(g.shape, g.dtype) for g in gathered],
                 [pltpu.SemaphoreType.DMA((3 * n_arr,)), pltpu.SemaphoreType.DMA((3 * n_arr,))], start, finish,
                 aliases={a: a for a in range(n_arr)})


def sibling_stage_rider(blobs):
    n_arr = len(blobs)

    def copies(rins, routs, sems):
        send_sems, recv_sems = sems
        x, y, c = _mesh_pos()
        return [pltpu.make_async_remote_copy(
            src_ref=rins[a].at[2 * j + (1 - c)], dst_ref=routs[a].at[j],
            send_sem=send_sems.at[4 * a + j], recv_sem=recv_sems.at[4 * a + j],
            device_id=(x, y, 1 - c), device_id_type=pl.DeviceIdType.MESH)
            for a in range(n_arr) for j in range(4)]

    def start(rins, routs, sems):
        for cp in copies(rins, routs, sems):
            cp.start()

    def finish(rins, routs, sems):
        for cp in copies(rins, routs, sems):
            cp.wait()

    return Rider(blobs, [jax.ShapeDtypeStruct((4,) + b.shape[1:], b.dtype) for b in blobs],
                 [pltpu.SemaphoreType.DMA((4 * n_arr,)), pltpu.SemaphoreType.DMA((4 * n_arr,))], start, finish)


def chip_partial(blob, from_sibling, name, tr):
    _, rows, cols = blob.shape
    c_idx = lax.axis_index("c").astype(jnp.int32).reshape(1)

    def body(c_ref, b_ref, s_ref, o_ref):
        o_ref[...] = (b_ref[...].astype(F32) + s_ref[...].astype(F32)).astype(o_ref.dtype)

    return pl.pallas_call(
        body, name=name,
        grid_spec=pltpu.PrefetchScalarGridSpec(
            num_scalar_prefetch=1, grid=(4, rows // tr),
            in_specs=[pl.BlockSpec((1, tr, cols), lambda j, i, c_ref: (2 * j + c_ref[0], i, 0)),
                      pl.BlockSpec((1, tr, cols), lambda j, i, c_ref: (j, i, 0))],
            out_specs=pl.BlockSpec((1, tr, cols), lambda j, i, c_ref: (j, i, 0))),
        out_shape=jax.ShapeDtypeStruct((4, rows, cols), blob.dtype),
        compiler_params=_cp(("parallel", "parallel")))(c_idx, blob, from_sibling)


def chip_stage_rider(partials, diagonal, row_ranges, received=None):
    n_arr = len(partials)
    n_remote = 1 if diagonal else 2

    def copies(rins, routs, sems):
        send_sems, recv_sems = sems[0], sems[1]
        x, y, c = _mesh_pos()
        chips = [(1 - x, 1 - y)] if diagonal else [(1 - x, y), (x, 1 - y)]
        return [pltpu.make_async_remote_copy(
            src_ref=rins[a].at[2 * px + py, pl.ds(*row_ranges[a])],
            dst_ref=routs[a].at[k if diagonal else 1 + k, pl.ds(*row_ranges[a])],
            send_sem=send_sems.at[n_remote * a + k], recv_sem=recv_sems.at[n_remote * a + k],
            device_id=(px, py, c), device_id_type=pl.DeviceIdType.MESH)
            for a in range(n_arr) for k, (px, py) in enumerate(chips)]

    def local(rins, routs, sems):
        x, y, _ = _mesh_pos()
        return [pltpu.make_async_copy(rins[a].at[2 * x + y, pl.ds(*row_ranges[a])],
                                      routs[a].at[0, pl.ds(*row_ranges[a])], sems[2].at[a]) for a in range(n_arr)]

    def start(rins, routs, sems):
        for cp in copies(rins, routs, sems) + ([] if diagonal else local(rins, routs, sems)):
            cp.start()

    def finish(rins, routs, sems):
        for cp in copies(rins, routs, sems) + ([] if diagonal else local(rins, routs, sems)):
            cp.wait()

    sem_list = [pltpu.SemaphoreType.DMA((n_remote * n_arr,)), pltpu.SemaphoreType.DMA((n_remote * n_arr,))]
    if not diagonal:
        sem_list.append(pltpu.SemaphoreType.DMA((n_arr,)))
    out_shapes = [jax.ShapeDtypeStruct((1 if diagonal else 3,) + p.shape[1:], p.dtype) for p in partials]
    if received is None:
        return Rider(partials, out_shapes, sem_list, start, finish)
    return Rider(list(partials) + list(received), out_shapes, sem_list, start, finish,
                 aliases={n_arr + a: a for a in range(n_arr)})


def _halves(partials):
    first, second = [], []
    for p in partials:
        rows = p.shape[1]
        cut = (rows // 32 + (rows // 16) % 2) * 16
        first.append((0, cut))
        second.append((cut, rows - cut))
    return first, second


def adamw(parts, w, m, v, layer, earlier, name, tr=128):
    n_layers, rows, cols = w.shape
    tr = min(tr, rows)
    n_arr = len(parts)
    n_pass = 0 if earlier is None else 4
    c1 = 1.0 - ADAM_B1 ** ADAM_STEP
    c2 = 1.0 - ADAM_B2 ** ADAM_STEP

    def body(*refs):
        p_refs = refs[:n_arr]
        w_ref, m_ref, v_ref = refs[n_arr:n_arr + 3]
        g_ref, d_ref, nm_ref, nv_ref = refs[n_arr + 3 + n_pass:]
        g = None
        for p_ref in p_refs:
            for s in range(p_ref.shape[0]):
                term = p_ref[s].astype(F32)
                g = term if g is None else g + term
        nm = ADAM_B1 * m_ref[0] + (1.0 - ADAM_B1) * g
        nv = ADAM_B2 * v_ref[0] + (1.0 - ADAM_B2) * (g * g)
        m_hat = nm / c1
        v_hat = nv / c2
        g_ref[0] = g
        d_ref[0] = -ADAM_LR * (m_hat / (jnp.sqrt(v_hat) + ADAM_EPS) + ADAM_WD * w_ref[0])
        nm_ref[0] = nm
        nv_ref[0] = nv

    blk = pl.BlockSpec((1, tr, cols), lambda i: (layer, i, 0))
    in_specs = [pl.BlockSpec((p.shape[0], tr, cols), lambda i: (0, i, 0)) for p in parts] + [blk, blk, blk]
    args = list(parts) + [w, m, v]
    aliases = {}
    if earlier is not None:
        in_specs += [pl.BlockSpec(memory_space=pl.ANY)] * 4
        aliases = {n_arr + 3 + k: k for k in range(4)}
        args += list(earlier)
    return pl.pallas_call(
        body, name=name, grid=(rows // tr,), in_specs=in_specs, out_specs=[blk, blk, blk, blk],
        out_shape=[jax.ShapeDtypeStruct((n_layers, rows, cols), F32)] * 4, input_output_aliases=aliases,
        compiler_params=_cp(("parallel",)))(*args)


def _pack_local(fam, shards, dtype):
    parts = []
    for n, rows, prows, transposed in fam:
        a = shards[n].T if transposed else shards[n]
        parts.append(jnp.pad(a.astype(dtype), ((0, prows - rows), (0, 0))))
    return jnp.concatenate(parts, axis=0)


def _unpack_gathered(fam, gathered):
    out, off = {}, 0
    for n, rows, prows, _ in fam:
        out[n] = gathered[:, off:off + rows].reshape(N_DEV * rows, gathered.shape[2])
        off += prows
    return out


def _pack_full_grads(fam, grads):
    segs = []
    for n, rows, prows, _ in fam:
        g = grads[n].astype(BF16)
        g = g.reshape(N_DEV, rows, g.shape[1])
        segs.append(jnp.pad(g, ((0, 0), (0, prows - rows), (0, 0))))
    return jnp.concatenate(segs, axis=1)


def _pack_local_layers(fam, src, dtype):
    parts = []
    for n, rows, prows, transposed in fam:
        a = src[n].transpose(0, 2, 1) if transposed else src[n]
        parts.append(jnp.pad(a.astype(dtype), ((0, 0), (0, prows - rows), (0, 0))))
    return jnp.concatenate(parts, axis=1)


def _unpack_local_layers(fam, blob):
    out, off = {}, 0
    for n, rows, prows, transposed in fam:
        a = blob[:, off:off + rows]
        out[n] = a.transpose(0, 2, 1) if transposed else a
        off += prows
    return out


def _pack_small(vals):
    flat = jnp.concatenate([vals[n].reshape(-1).astype(F32) for n, _ in SMALL])
    flat = jnp.pad(flat, (0, SMALL_ROWS * 128 - flat.shape[0]))
    return flat.reshape(SMALL_ROWS, 128)


def _unpack_small(flat2d):
    flat = flat2d.reshape(-1)
    out, off = {}, 0
    for n, k in SMALL:
        out[n] = flat[off:off + DEPTH * k].reshape(DEPTH, k)
        off += DEPTH * k
    return out


def _pad_w_in_t(w_t):
    return jnp.concatenate([w_t[:2560], w_t[2576:IN_COLS], w_t[2560:2576],
                            jnp.zeros((IN_PAD - IN_COLS, w_t.shape[1]), w_t.dtype)], axis=0)


def _unpad_w_in_t(g_t):
    return jnp.concatenate([g_t[:2560], g_t[3648:3664], g_t[2560:3648]], axis=0)


def _pad_head_rows(w_t):
    k = w_t.shape[1]
    v = jnp.pad(w_t.reshape(MLA_HEADS, MLA_QK, k), ((0, 0), (0, MLA_QK_PAD - MLA_QK), (0, 0)))
    return v.reshape(MLA_HEADS * MLA_QK_PAD, k)


def _unpad_head_rows(g_t):
    k = g_t.shape[1]
    return g_t.reshape(MLA_HEADS, MLA_QK_PAD, k)[:, :MLA_QK].reshape(MLA_HEADS * MLA_QK, k)


def _row(v):
    return v.reshape(1, -1).astype(F32)


def layer_fwd(x, memf, w, p, tabs, bsz, next_blocks):
    t_rows = x.shape[0]
    seq = t_rows // bsz
    r = {}
    r["x"] = x
    h1 = rms_fwd(x, 0, D_MODEL, _row(p["attn_norm_g"]), 1, D_MODEL, BF16, "norm1_fwd")
    proj = mm([(h1, w["w_in"])], "nt", tn=768, name="in_proj")
    r["h1"], r["proj"] = h1, proj
    proj3 = proj.reshape(bsz, seq, IN_PAD)

    w8 = jnp.concatenate([p["conv_w_full"], p["conv_b"].reshape(1, -1), jnp.zeros((3, CONV_DIM), F32)], 0)
    xbc = conv_fwd(proj3, w8, "conv_fwd")
    tail = proj[:, TAIL_COL:TAIL_COL + 128]
    dt_raw = tail[:, 64:80].reshape(bsz, seq, 16)
    y, states = ssd_fwd(xbc, dt_raw, _row(p["dt_bias"]), _row(p["a_log"]), _row(p["d_skip"]), "ssd_fwd")
    y2 = y.reshape(t_rows, SSD_INNER)
    yg = ew(_gate_fwd, [(y2, 0), (proj, 0)], SSD_INNER, (F32,), "ssd_gate_fwd")
    y_ssd = rms_fwd(yg, 0, SSD_INNER, _row(p["ssd_norm_g"]), 2, 512, BF16, "ssd_norm_fwd")
    r.update(w8=w8, xbc=xbc, dt_raw=dt_raw, states=states, y=y2, yg=yg)

    qn = rms_fwd(proj, 5, Q_LORA, _row(p["q_a_norm_g"]), 1, Q_LORA, BF16, "qa_norm_fwd")
    kvn = rms_fwd(proj, 6, Q_LORA, _row(p["kv_a_norm_g"]), 1, Q_LORA, BF16, "kva_norm_fwd")
    q_raw = mm([(qn, w["w_q_b"])], "nt", name="q_b_proj")
    kv = mm([(kvn, w["w_kv_b"])], "nt", name="kv_b_proj")
    gq = _row(jnp.tile(jnp.pad(p["mla_q_norm_g"], (0, 64)), MLA_HEADS))
    gk = _row(jnp.tile(jnp.pad(p["mla_k_norm_g"], (0, 64)), MLA_HEADS))
    q_raw3, kv3 = q_raw.reshape(bsz, seq, 2048), kv.reshape(bsz, seq, 2048)
    q_fin, k_fin = qk_prep_fwd(q_raw3, kv3, proj3, gq, gk, tabs, "qk_prep_fwd")
    riding = next_blocks is not None
    o, lse, gathered = attn_fwd(q_fin, k_fin, kv3, heads=MLA_HEADS, dk=MLA_QK_PAD, dv=MLA_V, v_mul=2, v_off=1,
                                causal=True, scale=MLA_QK ** -0.5, tq=512, tk=512,
                                name="mla_attn_fwd_gather_near" if riding else "mla_attn_fwd",
                                rider=gather_near_rider(next_blocks) if riding else None)
    r.update(qn=qn, kvn=kvn, q_raw3=q_raw3, gq=gq, gk=gk, q_fin=q_fin, k_fin=k_fin, kv3=kv3, o=o, lse=lse)
    x2 = mm([(y_ssd, w["w_out"][:SSD_INNER]), (o.reshape(t_rows, SSD_INNER), w["w_out"][SSD_INNER:])], "nn",
            extras=(x,), epilogue=_add_res, name="out_proj")
    r.update(y_ssd=y_ssd, x2=x2)

    h2 = rms_fwd(x2, 0, D_MODEL, _row(p["xattn_norm_g"]), 1, D_MODEL, BF16, "norm2_fwd")
    mn = rms_fwd(memf, 0, D_MODEL, _row(p["mem_norm_g"]), 1, D_MODEL, BF16, "mem_norm_fwd")
    xq_raw = mm([(h2, w["w_xq"])], "nn", name="xq_proj")
    xk_raw = mm([(mn, w["w_xk"])], "nn", name="xk_proj")
    xv = mm([(mn, w["w_xv"])], "nn", name="xv_proj")
    gxq = _row(jnp.tile(p["xq_norm_g"], X_HEADS))
    gxk = _row(jnp.tile(p["xk_norm_g"], X_HEADS))
    xq = rms_fwd(xq_raw, 0, X_INNER, gxq, X_HEADS, X_HEAD_DIM, BF16, "xq_norm_fwd")
    xk = rms_fwd(xk_raw, 0, X_INNER, gxk, X_HEADS, X_HEAD_DIM, BF16, "xk_norm_fwd")
    mlen = memf.shape[0] // bsz
    xq3, xk3, xv3 = xq.reshape(bsz, seq, X_INNER), xk.reshape(bsz, mlen, X_INNER), xv.reshape(bsz, mlen, X_INNER)
    xo, xlse, _ = attn_fwd(xq3, xk3, xv3, heads=X_HEADS, dk=X_HEAD_DIM, dv=X_HEAD_DIM, v_mul=1, v_off=0,
                           causal=False, scale=X_HEAD_DIM ** -0.5, tq=512, tk=256, name="xattn_fwd")
    xo2 = xo.reshape(t_rows, X_INNER)
    x3 = mm([(xo2, w["w_xo"])], "nt", extras=(x2,), epilogue=_add_res, name="xo_proj")
    r.update(h2=h2, mn=mn, xq_raw=xq_raw, xk_raw=xk_raw, gxq=gxq, gxk=gxk, xq3=xq3, xk3=xk3, xv3=xv3,
             xo=xo, xlse=xlse, x3=x3)

    h3 = rms_fwd(x3, 0, D_MODEL, _row(p["ffn_norm_g"]), 1, D_MODEL, BF16, "norm3_fwd")
    if riding:
        (gate, up, act), gathered = mm(
            [(h3, w["w_gate"]), (h3, w["w_up"])], "nt", epilogue=_swiglu_fwd, separate=True,
            out_dtypes=(BF16, BF16, BF16), name="gate_up_proj_gather_diagonal",
            rider=gather_diagonal_rider(next_blocks, gathered))
        x4, gathered = mm([(act, w["w_down"])], "nn", extras=(x3,), epilogue=_add_res, tm=512,
                          name="down_proj_gather_forward", rider=gather_forward_rider(gathered))
    else:
        gate, up, act = mm([(h3, w["w_gate"]), (h3, w["w_up"])], "nt", epilogue=_swiglu_fwd, separate=True,
                           out_dtypes=(BF16, BF16, BF16), name="gate_up_proj")
        x4 = mm([(act, w["w_down"])], "nn", extras=(x3,), epilogue=_add_res, tm=512, name="down_proj")
        gathered = None
    r.update(h3=h3, gate=gate, up=up, act=act)
    return x4, r, gathered


def layer_bwd(dx4, r, memf, w, p, tabs, bsz, pending):
    t_rows = dx4.shape[0]
    seq = t_rows // bsz
    big, small = {}, {}
    rider_xy = rider_diag = None

    if pending is None:
        dgate, dup = mm([(dx4, w["w_down"])], "nt", extras=(r["gate"], r["up"]), epilogue=_swiglu_bwd,
                        out_dtypes=(BF16, BF16), name="down_proj_bwd")
    else:
        (dgate, dup), from_sib = mm([(dx4, w["w_down"])], "nt", extras=(r["gate"], r["up"]), epilogue=_swiglu_bwd,
                                    out_dtypes=(BF16, BF16), name="down_proj_bwd_sibling_exchange",
                                    rider=sibling_stage_rider(pending))
        partials = [chip_partial(b, s, "grad_chip_partial", tr) for b, s, tr in zip(pending, from_sib, PARTIAL_TR)]
        first_half, second_half = _halves(partials)
    if pending is None:
        big["w_down"] = mm([(r["act"], dx4)], "tn", out_dtypes=(BF16,), name="down_proj_wgrad")
    else:
        big["w_down"], recv_diag = mm([(r["act"], dx4)], "tn", out_dtypes=(BF16,),
                                      name="down_proj_wgrad_diagonal_exchange",
                                      rider=chip_stage_rider(partials, True, first_half))
    big["w_gate"] = mm([(dgate, r["h3"])], "tn", out_dtypes=(BF16,), tm=512, tn=1024, name="gate_proj_wgrad")
    big["w_up"] = mm([(dup, r["h3"])], "tn", out_dtypes=(BF16,), tm=512, tn=1024, name="gate_proj_wgrad")
    if pending is None:
        dh3 = mm([(dgate, w["w_gate"]), (dup, w["w_up"])], "nn", tm=512, tn=256, name="gate_up_proj_bwd")
    else:
        dh3, recv_xy = mm([(dgate, w["w_gate"]), (dup, w["w_up"])], "nn", tm=512, tn=256,
                          name="gate_up_proj_bwd_near_exchange",
                          rider=chip_stage_rider(partials, False, first_half))
        rider_xy = chip_stage_rider(partials, False, second_half, received=recv_xy)
        rider_diag = chip_stage_rider(partials, True, second_half, received=recv_diag)
    dx3, g = rms_bwd(r["x3"], 0, D_MODEL, _row(p["ffn_norm_g"]), 1, D_MODEL, dh3, dx4, "norm3_bwd")
    small["ffn_norm_g"] = g[0]

    xo2 = r["xo"].reshape(t_rows, X_INNER)
    dxo = mm([(dx3, w["w_xo"])], "nn", name="xo_proj_bwd")
    big["w_xo"] = mm([(dx3, xo2)], "tn", out_dtypes=(BF16,), tm=512, name="xo_proj_wgrad")
    dxq, dxk, dxv, _, _ = attn_bwd(r["xq3"], r["xk3"], r["xv3"], dxo.reshape(bsz, seq, X_INNER), 0, r["xo"],
                                   r["xlse"], heads=X_HEADS, dk=X_HEAD_DIM, dv=X_HEAD_DIM, v_mul=1, v_off=0,
                                   causal=False, scale=X_HEAD_DIM ** -0.5, tq=512, tk=256, name="xattn_bwd")
    mrows = memf.shape[0]
    dxq_raw, g = rms_bwd(r["xq_raw"], 0, X_INNER, r["gxq"], X_HEADS, X_HEAD_DIM, dxq.reshape(t_rows, X_INNER),
                         None, "xq_norm_bwd")
    small["xq_norm_g"] = g.reshape(X_HEADS, X_HEAD_DIM).sum(0)
    dxk_raw, g = rms_bwd(r["xk_raw"], 0, X_INNER, r["gxk"], X_HEADS, X_HEAD_DIM, dxk.reshape(mrows, X_INNER),
                         None, "xk_norm_bwd")
    small["xk_norm_g"] = g.reshape(X_HEADS, X_HEAD_DIM).sum(0)
    dxv2 = dxv.reshape(mrows, X_INNER)
    big["w_xq"] = mm([(r["h2"], dxq_raw)], "tn", out_dtypes=(BF16,), name="xq_proj_wgrad")
    big["w_xk"] = mm([(r["mn"], dxk_raw)], "tn", out_dtypes=(BF16,), name="xkv_proj_wgrad")
    big["w_xv"] = mm([(r["mn"], dxv2)], "tn", out_dtypes=(BF16,), name="xkv_proj_wgrad")
    dh2 = mm([(dxq_raw, w["w_xq"])], "nt", name="xq_proj_bwd")
    dmn = mm([(dxk_raw, w["w_xk"]), (dxv2, w["w_xv"])], "nt", name="xkv_proj_bwd")
    _, g = rms_bwd(memf, 0, D_MODEL, _row(p["mem_norm_g"]), 1, D_MODEL, dmn, None, "mem_norm_bwd")
    small["mem_norm_g"] = g[0]
    dx2, g = rms_bwd(r["x2"], 0, D_MODEL, _row(p["xattn_norm_g"]), 1, D_MODEL, dh2, dx3, "norm2_bwd")
    small["xattn_norm_g"] = g[0]

    dmixed = mm([(dx2, w["w_out"])], "nt", name="out_proj_bwd")
    big["w_out"] = jnp.concatenate(
        [mm([(r["y_ssd"], dx2)], "tn", out_dtypes=(BF16,), tm=512, name="out_proj_wgrad"),
         mm([(r["o"].reshape(t_rows, SSD_INNER), dx2)], "tn", out_dtypes=(BF16,), tm=512, name="out_proj_wgrad")],
        axis=0)
    dmixed3 = dmixed.reshape(bsz, seq, D_MODEL)

    dq_fin, dk_fin, dv, recv_xy, recv_diag = attn_bwd(
        r["q_fin"], r["k_fin"], r["kv3"], dmixed3, MLA_HEADS, r["o"], r["lse"], heads=MLA_HEADS, dk=MLA_QK_PAD,
        dv=MLA_V, v_mul=2, v_off=1, causal=True, scale=MLA_QK ** -0.5, tq=512, tk=512,
        name="mla_attn_bwd" if pending is None else "mla_attn_bwd_chip_exchange",
        rider_dq=rider_xy, rider_dkv=rider_diag)
    dq_raw, dkv, dkrope, g_q, g_k = qk_prep_bwd(r["q_raw3"], r["kv3"], r["proj"].reshape(bsz, seq, IN_PAD), r["gq"],
                                                r["gk"], tabs, dq_fin, dk_fin, dv, "qk_prep_bwd")
    dq_raw, dkv, dkrope = dq_raw.reshape(t_rows, 2048), dkv.reshape(t_rows, 2048), dkrope.reshape(t_rows, 128)
    small["mla_q_norm_g"] = g_q.reshape(MLA_HEADS, MLA_QK_PAD)[:, :MLA_QK].sum(0)
    small["mla_k_norm_g"] = g_k.reshape(MLA_HEADS, MLA_QK_PAD)[:, :MLA_QK].sum(0)
    big["w_q_b"] = _unpad_head_rows(mm([(dq_raw, r["qn"])], "tn", out_dtypes=(BF16,), tm=512,
                                       name="qkv_b_proj_wgrad"))
    dqn = mm([(dq_raw, w["w_q_b"])], "nn", name="qkv_b_proj_bwd")
    big["w_kv_b"] = mm([(dkv, r["kvn"])], "tn", out_dtypes=(BF16,), tm=512, name="qkv_b_proj_wgrad")
    dkvn = mm([(dkv, w["w_kv_b"])], "nn", name="qkv_b_proj_bwd")
    dq_a, g = rms_bwd(r["proj"], 5, Q_LORA, _row(p["q_a_norm_g"]), 1, Q_LORA, dqn, None, "qa_norm_bwd")
    small["q_a_norm_g"] = g[0]
    dkv_a, g = rms_bwd(r["proj"], 6, Q_LORA, _row(p["kv_a_norm_g"]), 1, Q_LORA, dkvn, None, "qa_norm_bwd")
    small["kv_a_norm_g"] = g[0]

    dyg, g = rms_bwd(r["yg"], 0, SSD_INNER, _row(p["ssd_norm_g"]), 2, 512, dmixed, None, "ssd_norm_bwd")
    small["ssd_norm_g"] = g[0]
    dy, dz = ew(_gate_bwd, [(dyg, 0), (r["y"], 0), (r["proj"], 0)], SSD_INNER, (F32, F32), "ssd_gate_bwd")
    dxbc_c, ddt_raw, g_dtb, g_alog, g_dsk = ssd_bwd(
        r["xbc"], r["dt_raw"], _row(p["dt_bias"]), _row(p["a_log"]), _row(p["d_skip"]), r["states"],
        dy.reshape(bsz, seq, SSD_INNER), "ssd_bwd")
    small["dt_bias"], small["a_log"], small["d_skip"] = g_dtb[0], g_alog[0], g_dsk[0]
    dxbc, dw8 = conv_bwd(r["proj"].reshape(bsz, seq, IN_PAD), r["w8"], dxbc_c, "conv_bwd")
    small["conv_w"] = dw8[:4]
    small["conv_b"] = dw8[4]

    dtail = jnp.concatenate([dkrope[:, :64], ddt_raw.reshape(t_rows, 16), jnp.zeros((t_rows, 48 + 128), F32)], axis=1)
    dproj = jnp.concatenate([dz.astype(BF16), dxbc.reshape(t_rows, CONV_DIM).astype(BF16), dq_a.astype(BF16),
                             dkv_a.astype(BF16), dtail.astype(BF16)], axis=1)
    big["w_in"] = _unpad_w_in_t(mm([(dproj, r["h1"])], "tn", out_dtypes=(BF16,), tm=768, tn=1024,
                                   name="in_proj_wgrad"))
    dh1 = mm([(dproj, w["w_in"])], "nn", name="in_proj_bwd")
    dx, g = rms_bwd(r["x"], 0, D_MODEL, _row(p["attn_norm_g"]), 1, D_MODEL, dh1, dx2, "norm1_bwd")
    small["attn_norm_g"] = g[0]
    if pending is None:
        recv_xy = recv_diag = None
    return dx, big, small, recv_xy, recv_diag


def _rope_tables(positions):
    inv_freq = 1.0 / (ROPE_THETA ** (jnp.arange(0, 64, 2, dtype=F32) / 64))
    ang = positions.astype(F32)[..., None] * inv_freq
    c, s = jnp.cos(ang), jnp.sin(ang)
    z32, z64 = jnp.zeros_like(c), jnp.zeros(c.shape[:-1] + (64,), F32)
    return (jnp.concatenate([c, c, z64], -1), jnp.concatenate([-s, z32, z64], -1),
            jnp.concatenate([z32, s, z64], -1))


def kernel(x, mem, positions, attn_norm_g, w_in, conv_w, conv_b, dt_bias, a_log, d_skip, ssd_norm_g, q_a_norm_g, w_q_b, kv_a_norm_g, w_kv_b, mla_q_norm_g, mla_k_norm_g, w_out, xattn_norm_g, mem_norm_g, w_xq, w_xk, w_xv, xq_norm_g, xk_norm_g, w_xo, ffn_norm_g, w_gate, w_up, w_down, loss_target, m_attn_norm_g, m_w_in, m_conv_w, m_conv_b, m_dt_bias, m_a_log, m_d_skip, m_ssd_norm_g, m_q_a_norm_g, m_w_q_b, m_kv_a_norm_g, m_w_kv_b, m_mla_q_norm_g, m_mla_k_norm_g, m_w_out, m_xattn_norm_g, m_mem_norm_g, m_w_xq, m_w_xk, m_w_xv, m_xq_norm_g, m_xk_norm_g, m_w_xo, m_ffn_norm_g, m_w_gate, m_w_up, m_w_down, v_attn_norm_g, v_w_in, v_conv_w, v_conv_b, v_dt_bias, v_a_log, v_d_skip, v_ssd_norm_g, v_q_a_norm_g, v_w_q_b, v_kv_a_norm_g, v_w_kv_b, v_mla_q_norm_g, v_mla_k_norm_g, v_w_out, v_xattn_norm_g, v_mem_norm_g, v_w_xq, v_w_xk, v_w_xv, v_xq_norm_g, v_xk_norm_g, v_w_xo, v_ffn_norm_g, v_w_gate, v_w_up, v_w_down):
    args = dict(locals())
    weights = {n: args[n] for n in WEIGHT_ORDER}
    mom_m = {n: args["m_" + n] for n in WEIGHT_ORDER}
    mom_v = {n: args["v_" + n] for n in WEIGHT_ORDER}
    bsz, seq, _ = x.shape
    t_rows = bsz * seq
    xf = x.reshape(t_rows, D_MODEL)
    memf = mem.reshape(-1, D_MODEL)
    tabs = _rope_tables(positions)

    full = []
    def local_blocks(l):
        return [_pack_local(fam, {n: weights[n][l] for n, _, _, _ in fam}, BF16) for fam, _ in FAMS]

    def full_weights(gathered):
        wl = {}
        for (fam, _), g in zip(FAMS, gathered):
            wl.update(_unpack_gathered(fam, g))
        wl["w_in"] = _pad_w_in_t(wl["w_in"])
        wl["w_q_b"] = _pad_head_rows(wl["w_q_b"])
        return wl

    full.append(full_weights(all_gather(local_blocks(0), "weight_all_gather")))
    small_p = [{n: weights[n][l] for n, _ in SMALL} for l in range(DEPTH)]
    cw = all_gather([conv_w.reshape(-1, 128)], "conv_w_all_gather")[0].reshape(N_DEV, DEPTH, 4, -1)
    cw = cw.transpose(1, 2, 0, 3).reshape(DEPTH, 4, CONV_DIM)
    for l in range(DEPTH):
        small_p[l]["conv_w_full"] = cw[l]

    saved = []
    h = xf
    for l in range(DEPTH):
        h, res, gathered = layer_fwd(h, memf, full[l], small_p[l], tabs, bsz,
                                     local_blocks(l + 1) if l + 1 < DEPTH else None)
        saved.append(res)
        if gathered is not None:
            full.append(full_weights(gathered))
    dy, loss_part = loss_head(h, loss_target.reshape(t_rows, D_MODEL), "loss_head")
    loss = lax.psum(loss_part[0, 0], ("x", "y", "c"))

    small_g = [None] * DEPTH
    outs = {k: {} for k in ("g", "d", "m", "v")}
    results = {}

    def update_layer(l, recv_xy, recv_diag):
        for a, (fam, _) in enumerate(FAMS):
            off = 0
            for n, rows, prows, transposed in fam:
                part = jnp.concatenate([recv_xy[a][:, off:off + rows], recv_diag[a][:, off:off + rows]], axis=0)
                off += prows
                if transposed:
                    part = part.transpose(0, 2, 1)
                results[n] = adamw([part], weights[n], mom_m[n], mom_v[n], l, results.get(n), "adamw_" + n,
                                   tr=ADAMW_TR[n])

    dh, pending = dy, None
    for l in reversed(range(DEPTH)):
        dh, big_g, small_g[l], recv_xy, recv_diag = layer_bwd(dh, saved[l], memf, full[l], small_p[l], tabs, bsz,
                                                             pending)
        if pending is not None:
            update_layer(l + 1, recv_xy, recv_diag)
        pending = [_pack_full_grads(fam, big_g) for fam, _ in FAMS]
    from_sib = comm_call([sibling_stage_rider(pending)], "grad_sibling_exchange")[0]
    partials = [chip_partial(b, s, "grad_chip_partial", tr) for b, s, tr in zip(pending, from_sib, PARTIAL_TR)]
    whole = [(0, part.shape[1]) for part in partials]
    recv_xy, recv_diag = comm_call([chip_stage_rider(partials, False, whole), chip_stage_rider(partials, True, whole)],
                                   "grad_chip_exchange")
    update_layer(0, recv_xy, recv_diag)
    for ki, key in enumerate(("g", "d", "m", "v")):
        for n, res in results.items():
            outs[key][n] = res[ki]

    sg = _pack_small({n: jnp.stack([small_g[l][n] for l in range(DEPTH)]) for n, _ in SMALL})
    cg = jnp.stack([small_g[l]["conv_w"] for l in range(DEPTH)]).reshape(CONVW_ROWS, 128)
    parts = all_gather([jnp.concatenate([sg, cg], axis=0)], "small_grad_all_gather")[0]
    sm = adamw([parts[:, :SMALL_ROWS]], _pack_small({n: weights[n] for n, _ in SMALL})[None],
               _pack_small({n: mom_m[n] for n, _ in SMALL})[None], _pack_small({n: mom_v[n] for n, _ in SMALL})[None],
               0, None, "adamw_small", tr=SMALL_ROWS)
    for ki, key in enumerate(("g", "d", "m", "v")):
        outs[key].update(_unpack_small(sm[ki][0]))
    my_idx = 4 * lax.axis_index("x") + 2 * lax.axis_index("y") + lax.axis_index("c")
    cparts = parts[:, SMALL_ROWS:].reshape(N_DEV, DEPTH, 4, N_DEV, -1)
    cparts = lax.dynamic_index_in_dim(cparts, my_idx, axis=3, keepdims=False).reshape(N_DEV, -1, 128)
    cres = adamw([cparts], conv_w.reshape(1, -1, 128), m_conv_w.reshape(1, -1, 128), v_conv_w.reshape(1, -1, 128),
                 0, None, "adamw_conv_w")
    for ki, key in enumerate(("g", "d", "m", "v")):
        outs[key]["conv_w"] = cres[ki].reshape(conv_w.shape)

    grad_x = dh.reshape(bsz, seq, D_MODEL)
    return (loss, grad_x, *[outs["g"][n] for n in WEIGHT_ORDER], *[outs["d"][n] for n in WEIGHT_ORDER],
            *[outs["m"][n] for n in WEIGHT_ORDER], *[outs["v"][n] for n in WEIGHT_ORDER])
```

```python
import functools

import jax
import jax.numpy as jnp
from jax import lax
from jax.experimental import pallas as pl
from jax.experimental.pallas import tpu as pltpu

F32 = jnp.float32
BF16 = jnp.bfloat16

D_MODEL = 2048
DEPTH = 4
N_DEV = 8
SSD_INNER = 1024
SSD_HEADS = 16
SSD_PAIRS = 8
SSD_STATE = 128
SSD_CHUNK = 128
CONV_DIM = 1536
MLA_HEADS = 8
MLA_QK = 192
MLA_QK_PAD = 256
MLA_V = 128
Q_LORA = 512
X_HEADS = 4
X_HEAD_DIM = 128
X_INNER = 512
FFN = 5632
IN_COLS = 3664
IN_PAD = 3840
TAIL_COL = 3584
ROPE_THETA = 10000.0
EPS = 1e-6
NEG = -1e30
VMEM_LIMIT = 56 * 1024 * 1024
HI = lax.Precision.HIGHEST

ADAM_LR = 0.001
ADAM_B1 = 0.9
ADAM_B2 = 0.999
ADAM_EPS = 1e-08
ADAM_WD = 0.01
ADAM_STEP = 10

SMALL_ROWS = 392
CONVW_ROWS = 192

FAM_A = (("w_in", 458, 464, True), ("w_out", 256, 256, False))
FAM_G = (("w_gate", 704, 704, True),)
FAM_U = (("w_up", 704, 704, True),)
FAM_D = (("w_down", 704, 704, False),)
FAM_B = (("w_q_b", 192, 192, True), ("w_kv_b", 256, 256, True), ("w_xq", 256, 256, False),
         ("w_xk", 256, 256, False), ("w_xv", 256, 256, False), ("w_xo", 256, 256, True))
FAMS = ((FAM_A, 2048), (FAM_G, 2048), (FAM_U, 2048), (FAM_D, 2048), (FAM_B, 512))
PARTIAL_TR = (720, 704, 704, 704, 1472)
ADAMW_TRANSPOSED = ("w_gate", "w_up", "w_q_b")
ADAMW_TR = {"w_in": 512, "w_gate": 176, "w_up": 176, "w_down": 176, "w_out": 128, "w_q_b": 192, "w_kv_b": 512,
            "w_xq": 256, "w_xk": 256, "w_xv": 256, "w_xo": 512}
SMALL = (
    ("attn_norm_g", 2048), ("conv_b", 1536), ("dt_bias", 16), ("a_log", 16), ("d_skip", 16),
    ("ssd_norm_g", 1024), ("q_a_norm_g", 512), ("kv_a_norm_g", 512), ("mla_q_norm_g", 192),
    ("mla_k_norm_g", 192), ("xattn_norm_g", 2048), ("mem_norm_g", 2048), ("xq_norm_g", 128),
    ("xk_norm_g", 128), ("ffn_norm_g", 2048),
)
WEIGHT_ORDER = ("attn_norm_g", "w_in", "conv_w", "conv_b", "dt_bias", "a_log", "d_skip", "ssd_norm_g",
                "q_a_norm_g", "w_q_b", "kv_a_norm_g", "w_kv_b", "mla_q_norm_g", "mla_k_norm_g", "w_out",
                "xattn_norm_g", "mem_norm_g", "w_xq", "w_xk", "w_xv", "xq_norm_g", "xk_norm_g", "w_xo",
                "ffn_norm_g", "w_gate", "w_up", "w_down")


def _cp(sem=None):
    if sem is None:
        return pltpu.CompilerParams(vmem_limit_bytes=VMEM_LIMIT)
    return pltpu.CompilerParams(vmem_limit_bytes=VMEM_LIMIT, dimension_semantics=sem)


def _tile(n, pref):
    if n <= pref:
        return n
    t = (pref // 128) * 128
    while t > 128 and n % t:
        t -= 128
    return t


def _sigmoid(v):
    return 1.0 / (1.0 + jnp.exp(-v))


class Rider:
    def __init__(self, ins, out_shapes, sems, start, finish, aliases=None):
        self.ins, self.out_shapes, self.sems = list(ins), list(out_shapes), list(sems)
        self.start, self.finish = start, finish
        self.aliases = dict(aliases or {})


def _attach(rider, body, grid, in_specs, out_specs, out_shape, scratch, args):
    n_in, n_out, n_scr = len(in_specs), len(out_specs), len(scratch)
    r_in, r_out = len(rider.ins), len(rider.out_shapes)

    def wrapped(*refs):
        ins, rins = refs[:n_in], refs[n_in:n_in + r_in]
        o0 = n_in + r_in
        outs, routs = refs[o0:o0 + n_out], refs[o0 + n_out:o0 + n_out + r_out]
        s0 = o0 + n_out + r_out
        scr, rsems = refs[s0:s0 + n_scr], refs[s0 + n_scr:]
        first = last = None
        for d, g in enumerate(grid):
            i = pl.program_id(d)
            first = (i == 0) if first is None else jnp.logical_and(first, i == 0)
            last = (i == g - 1) if last is None else jnp.logical_and(last, i == g - 1)
        if first is None:
            rider.start(rins, routs, rsems)
            body(*ins, *outs, *scr)
            rider.finish(rins, routs, rsems)
            return
        pl.when(first)(lambda: rider.start(rins, routs, rsems))
        body(*ins, *outs, *scr)
        pl.when(last)(lambda: rider.finish(rins, routs, rsems))

    hbm = pl.BlockSpec(memory_space=pl.ANY)
    io_aliases = {n_in + ri: n_out + ro for ri, ro in rider.aliases.items()}
    return (wrapped, list(in_specs) + [hbm] * r_in, list(out_specs) + [hbm] * r_out,
            list(out_shape) + rider.out_shapes, list(scratch) + rider.sems, list(args) + rider.ins, io_aliases)


def comm_call(riders, name):
    def start(rins, routs, sems):
        for r, (i0, i1, o0, o1, s0, s1) in zip(riders, spans):
            r.start(rins[i0:i1], routs[o0:o1], sems[s0:s1])

    def finish(rins, routs, sems):
        for r, (i0, i1, o0, o1, s0, s1) in zip(riders, spans):
            r.finish(rins[i0:i1], routs[o0:o1], sems[s0:s1])

    spans, i, o, s = [], 0, 0, 0
    for r in riders:
        spans.append((i, i + len(r.ins), o, o + len(r.out_shapes), s, s + len(r.sems)))
        i, o, s = i + len(r.ins), o + len(r.out_shapes), s + len(r.sems)
    aliases = {i0 + ri: o0 + ro for r, (i0, _, o0, _, _, _) in zip(riders, spans) for ri, ro in r.aliases.items()}
    both = Rider([a for r in riders for a in r.ins], [a for r in riders for a in r.out_shapes],
                 [a for r in riders for a in r.sems], start, finish, aliases)
    body, in_specs, out_specs, out_shape, scratch, args, io_aliases = _attach(
        both, lambda: None, (), [], [], [], [], [])
    outs = pl.pallas_call(body, name=name, in_specs=in_specs, out_specs=out_specs, out_shape=out_shape,
                          scratch_shapes=scratch, input_output_aliases=io_aliases)(*args)
    return [outs[a:b] for (_, _, a, b, _, _) in spans]


def rms_fwd(x, col_blk, width, g, groups, n_valid, out_dtype, name, tr=512):
    rows = x.shape[0]
    tr = min(tr, rows)
    wg = width // groups

    def body(x_ref, g_ref, y_ref):
        for gi in range(groups):
            sl = slice(gi * wg, (gi + 1) * wg)
            xv = x_ref[:, sl].astype(F32)
            r = lax.rsqrt(jnp.sum(xv * xv, axis=-1, keepdims=True) * (1.0 / n_valid) + EPS)
            y_ref[:, sl] = ((xv * r) * g_ref[:, sl]).astype(out_dtype)

    return pl.pallas_call(
        body, name=name, grid=(rows // tr,),
        in_specs=[pl.BlockSpec((tr, width), lambda i: (i, col_blk)), pl.BlockSpec((1, width), lambda i: (0, 0))],
        out_specs=pl.BlockSpec((tr, width), lambda i: (i, 0)),
        out_shape=jax.ShapeDtypeStruct((rows, width), out_dtype), compiler_params=_cp(("parallel",)))(x, g)


def rms_bwd(x, col_blk, width, g, groups, n_valid, dy, add, name, tr=512):
    rows = x.shape[0]
    tr = min(tr, rows)
    wg = width // groups
    has_add = add is not None

    def body(*refs):
        if has_add:
            x_ref, g_ref, dy_ref, add_ref, dx_ref, dg_ref = refs
        else:
            x_ref, g_ref, dy_ref, dx_ref, dg_ref = refs

        @pl.when(pl.program_id(0) == 0)
        def _():
            dg_ref[...] = jnp.zeros(dg_ref.shape, F32)

        for gi in range(groups):
            sl = slice(gi * wg, (gi + 1) * wg)
            xv = x_ref[:, sl].astype(F32)
            dyv = dy_ref[:, sl].astype(F32)
            r = lax.rsqrt(jnp.sum(xv * xv, axis=-1, keepdims=True) * (1.0 / n_valid) + EPS)
            xh = xv * r
            dyg = dyv * g_ref[:, sl]
            c = jnp.sum(dyg * xh, axis=-1, keepdims=True) * (1.0 / n_valid)
            dx = r * (dyg - xh * c)
            if has_add:
                dx = dx + add_ref[:, sl]
            dx_ref[:, sl] = dx
            dg_ref[:, sl] += jnp.sum(dyv * xh, axis=0, keepdims=True)

    in_specs = [pl.BlockSpec((tr, width), lambda i: (i, col_blk)), pl.BlockSpec((1, width), lambda i: (0, 0)),
                pl.BlockSpec((tr, width), lambda i: (i, 0))]
    args = [x, g, dy]
    if has_add:
        in_specs.append(pl.BlockSpec((tr, width), lambda i: (i, 0)))
        args.append(add)
    return pl.pallas_call(
        body, name=name, grid=(rows // tr,), in_specs=in_specs,
        out_specs=[pl.BlockSpec((tr, width), lambda i: (i, 0)), pl.BlockSpec((1, width), lambda i: (0, 0))],
        out_shape=[jax.ShapeDtypeStruct((rows, width), F32), jax.ShapeDtypeStruct((1, width), F32)],
        compiler_params=_cp(("arbitrary",)))(*args)


def mm(pairs, mode, *, extras=(), epilogue=None, out_dtypes=(F32,), separate=False, tm=1024, tn=512, name,
       rider=None):
    a0, b0 = pairs[0]
    if mode == "nn":
        m_dim, n_dim = a0.shape[0], b0.shape[1]
        dn = (((1,), (0,)), ((), ()))
    elif mode == "nt":
        m_dim, n_dim = a0.shape[0], b0.shape[0]
        dn = (((1,), (1,)), ((), ()))
    else:
        m_dim, n_dim = a0.shape[1], b0.shape[1]
        dn = (((0,), (0,)), ((), ()))
    tm = _tile(m_dim, tm)
    tn = _tile(n_dim, tn)
    n_pairs = len(pairs)
    n_extra = len(extras)

    def body(*refs):
        ins = refs[:2 * n_pairs]
        ex = refs[2 * n_pairs:2 * n_pairs + n_extra]
        outs = refs[2 * n_pairs + n_extra:]
        accs = []
        for k in range(n_pairs):
            a = ins[2 * k][...].astype(BF16)
            b = ins[2 * k + 1][...].astype(BF16)
            accs.append(lax.dot_general(a, b, dn, preferred_element_type=F32))
        if not separate:
            total = accs[0]
            for extra_acc in accs[1:]:
                total = total + extra_acc
            accs = [total]
        res = epilogue(*accs, *[e[...] for e in ex]) if epilogue is not None else tuple(accs)
        for o_ref, r in zip(outs, res):
            o_ref[...] = r.astype(o_ref.dtype)

    in_specs, args = [], []
    for a, b in pairs:
        if mode == "nn":
            in_specs += [pl.BlockSpec((tm, a.shape[1]), lambda i, j: (i, 0)),
                         pl.BlockSpec((b.shape[0], tn), lambda i, j: (0, j))]
        elif mode == "nt":
            in_specs += [pl.BlockSpec((tm, a.shape[1]), lambda i, j: (i, 0)),
                         pl.BlockSpec((tn, b.shape[1]), lambda i, j: (j, 0))]
        else:
            in_specs += [pl.BlockSpec((a.shape[0], tm), lambda i, j: (0, i)),
                         pl.BlockSpec((b.shape[0], tn), lambda i, j: (0, j))]
        args += [a, b]
    for e in extras:
        in_specs.append(pl.BlockSpec((tm, tn), lambda i, j: (i, j)))
        args.append(e)
    grid = (m_dim // tm, n_dim // tn)
    out_specs = [pl.BlockSpec((tm, tn), lambda i, j: (i, j)) for _ in out_dtypes]
    out_shape = [jax.ShapeDtypeStruct((m_dim, n_dim), dt) for dt in out_dtypes]
    scratch, sem, io_aliases = [], ("parallel", "parallel"), {}
    if rider is not None:
        body, in_specs, out_specs, out_shape, scratch, args, io_aliases = _attach(
            rider, body, grid, in_specs, out_specs, out_shape, scratch, args)
        sem = ("arbitrary", "arbitrary")
    outs = pl.pallas_call(body, name=name, grid=grid, in_specs=in_specs, out_specs=out_specs, out_shape=out_shape,
                          scratch_shapes=scratch, input_output_aliases=io_aliases, compiler_params=_cp(sem))(*args)
    main = outs[0] if len(out_dtypes) == 1 else outs[:len(out_dtypes)]
    return main if rider is None else (main, outs[len(out_dtypes):])


def _add_res(acc, res):
    return (acc + res,)


def _swiglu_fwd(gate, up):
    gb = gate.astype(BF16).astype(F32)
    ub = up.astype(BF16).astype(F32)
    return gate, up, gb * _sigmoid(gb) * ub


def _swiglu_bwd(dact, gate, up):
    gv = gate.astype(F32)
    uv = up.astype(F32)
    sg = _sigmoid(gv)
    return dact * uv * (sg * (1.0 + gv * (1.0 - sg))), dact * (gv * sg)


def ew(fn, ins, width, out_dtypes, name, tr=512):
    rows = ins[0][0].shape[0]
    tr = min(tr, rows)
    n_in = len(ins)

    def body(*refs):
        res = fn(*[r[...].astype(F32) for r in refs[:n_in]])
        for o_ref, r in zip(refs[n_in:], res):
            o_ref[...] = r.astype(o_ref.dtype)

    in_specs = [pl.BlockSpec((tr, width), functools.partial(lambda i, cb: (i, cb), cb=cb)) for _, cb in ins]
    outs = pl.pallas_call(
        body, name=name, grid=(rows // tr,), in_specs=in_specs,
        out_specs=[pl.BlockSpec((tr, width), lambda i: (i, 0)) for _ in out_dtypes],
        out_shape=[jax.ShapeDtypeStruct((rows, width), dt) for dt in out_dtypes],
        compiler_params=_cp(("parallel",)))(*[a for a, _ in ins])
    return outs[0] if len(out_dtypes) == 1 else outs


def _gate_fwd(y, z):
    return (y * (z * _sigmoid(z)),)


def _gate_bwd(dyg, y, z):
    sg = _sigmoid(z)
    return dyg * (z * sg), dyg * y * (sg * (1.0 + z * (1.0 - sg)))


CONV_CB = 512
CONV_OFF = 2


def _conv_pre(x, w_ref, row):
    acc = x * w_ref[pl.ds(3, 1), :] + w_ref[pl.ds(4, 1), :]
    shifted = []
    for j in (1, 2, 3):
        xs = jnp.where(row >= j, pltpu.roll(x, j, 0), 0.0)
        shifted.append(xs)
        acc = acc + xs * w_ref[pl.ds(3 - j, 1), :]
    return acc, shifted


def conv_fwd(proj3, w8, name):
    bsz, seq, _ = proj3.shape

    def body(x_ref, w_ref, o_ref):
        x = x_ref[0]
        row = lax.broadcasted_iota(jnp.int32, x.shape, 0)
        pre, _ = _conv_pre(x, w_ref, row)
        o_ref[0] = pre * _sigmoid(pre)

    return pl.pallas_call(
        body, name=name, grid=(bsz, CONV_DIM // CONV_CB),
        in_specs=[pl.BlockSpec((1, seq, CONV_CB), lambda b, c: (b, 0, c + CONV_OFF)),
                  pl.BlockSpec((8, CONV_CB), lambda b, c: (0, c))],
        out_specs=pl.BlockSpec((1, seq, CONV_CB), lambda b, c: (b, 0, c)),
        out_shape=jax.ShapeDtypeStruct((bsz, seq, CONV_DIM), F32),
        compiler_params=_cp(("parallel", "parallel")))(proj3, w8)


def conv_bwd(proj3, w8, dout, name):
    bsz, seq, _ = proj3.shape

    def body(x_ref, w_ref, do_ref, dx_ref, dw_ref):
        @pl.when(pl.program_id(1) == 0)
        def _():
            dw_ref[...] = jnp.zeros(dw_ref.shape, F32)

        x = x_ref[0]
        row = lax.broadcasted_iota(jnp.int32, x.shape, 0)
        pre, shifted = _conv_pre(x, w_ref, row)
        sg = _sigmoid(pre)
        dpre = do_ref[0] * (sg * (1.0 + pre * (1.0 - sg)))
        dx = dpre * w_ref[pl.ds(3, 1), :]
        for j in (1, 2, 3):
            fut = jnp.where(row < seq - j, pltpu.roll(dpre, seq - j, 0), 0.0)
            dx = dx + fut * w_ref[pl.ds(3 - j, 1), :]
            dw_ref[pl.ds(3 - j, 1), :] += jnp.sum(dpre * shifted[j - 1], axis=0, keepdims=True)
        dw_ref[pl.ds(3, 1), :] += jnp.sum(dpre * x, axis=0, keepdims=True)
        dw_ref[pl.ds(4, 1), :] += jnp.sum(dpre, axis=0, keepdims=True)
        dx_ref[0] = dx

    return pl.pallas_call(
        body, name=name, grid=(CONV_DIM // CONV_CB, bsz),
        in_specs=[pl.BlockSpec((1, seq, CONV_CB), lambda c, b: (b, 0, c + CONV_OFF)),
                  pl.BlockSpec((8, CONV_CB), lambda c, b: (0, c)),
                  pl.BlockSpec((1, seq, CONV_CB), lambda c, b: (b, 0, c))],
        out_specs=[pl.BlockSpec((1, seq, CONV_CB), lambda c, b: (b, 0, c)),
                   pl.BlockSpec((8, CONV_CB), lambda c, b: (0, c))],
        out_shape=[jax.ShapeDtypeStruct((bsz, seq, CONV_DIM), F32), jax.ShapeDtypeStruct((8, CONV_DIM), F32)],
        compiler_params=_cp(("parallel", "arbitrary")))(proj3, w8, dout)


def _softplus(v):
    t = jnp.exp(-jnp.abs(v))
    small = t * (1.0 - t * (0.5 - t * (1.0 / 3.0)))
    return jnp.maximum(v, 0.0) + jnp.where(t < 0.01, small, jnp.log(1.0 + t))


def _ssd_chunk_prelude(dt_ref, dtb_ref, alog_ref):
    L = SSD_CHUNK
    raw = dt_ref[0] + dtb_ref[...]
    dt = _softplus(raw)
    a_neg = -jnp.exp(alog_ref[...])
    a = dt * a_neg
    r_i = lax.broadcasted_iota(jnp.int32, (L, L), 0)
    c_i = lax.broadcasted_iota(jnp.int32, (L, L), 1)
    tri_low = (r_i >= c_i).astype(F32)
    tri_up = (r_i <= c_i).astype(F32)
    acs = jnp.dot(tri_low, a, precision=HI, preferred_element_type=F32)
    acs_t = lax.dot_general(a, tri_up, (((0,), (0,)), ((), ())), precision=HI,
                            preferred_element_type=F32)
    tot = jnp.sum(a, axis=0, keepdims=True)
    return raw, dt, a_neg, acs, acs_t, tot, r_i, c_i, tri_up


def _col(arr16, e):
    lane = lax.broadcasted_iota(jnp.int32, arr16.shape, 1)
    return jnp.sum(jnp.where(lane == e, arr16, 0.0), axis=1, keepdims=True)


def _rowvec(arr_t, e):
    sub = lax.broadcasted_iota(jnp.int32, arr_t.shape, 0)
    return jnp.sum(jnp.where(sub == e, arr_t, 0.0), axis=0, keepdims=True)


def _pair(lo_mask, v0, v1):
    return jnp.where(lo_mask, v0, v1)


def ssd_fwd(xbc, dt_raw, dtb, alog, dsk, name):
    bsz, seq, _ = xbc.shape
    L = SSD_CHUNK
    nc = seq // L

    def body(x_ref, b_ref, c_ref, dt_ref, dtb_ref, alog_ref, dsk_ref, y_ref, st_ref, h_sc):
        @pl.when(pl.program_id(1) == 0)
        def _():
            h_sc[...] = jnp.zeros(h_sc.shape, F32)

        _, dt, _, acs, acs_t, tot, r_i, c_i, _ = _ssd_chunk_prelude(dt_ref, dtb_ref, alog_ref)
        causal = r_i >= c_i
        lo = lax.broadcasted_iota(jnp.int32, (1, 128), 1) < 64
        dskv = dsk_ref[...]
        for grp in range(2):
            bg = b_ref[0, :, grp * 128:(grp + 1) * 128].astype(BF16)
            cg = c_ref[0, :, grp * 128:(grp + 1) * 128].astype(BF16)
            cb = lax.dot_general(cg, bg, (((1,), (1,)), ((), ())), preferred_element_type=F32)
            for kk in range(4):
                k = grp * 4 + kk
                e0, e1 = 2 * k, 2 * k + 1
                sl = slice(k * 128, (k + 1) * 128)
                xp = x_ref[0, :, sl]
                cols = [_col(acs, e0), _col(acs, e1)]
                dtp = _pair(lo, _col(dt, e0), _col(dt, e1))
                xdt = xp * dtp
                ydiag = jnp.zeros((L, 128), F32)
                for sub, e in enumerate((e0, e1)):
                    diff = cols[sub] - _rowvec(acs_t, e)
                    lam = jnp.exp(jnp.where(causal, diff, NEG))
                    m_e = (cb * lam).astype(BF16)
                    xm = jnp.where(lo if sub == 0 else jnp.logical_not(lo), xdt, 0.0).astype(BF16)
                    ydiag = ydiag + jnp.dot(m_e, xm, preferred_element_type=F32)
                h = h_sc[k]
                st_ref[0, 0, k] = h
                wp = _pair(lo, jnp.exp(cols[0]), jnp.exp(cols[1]))
                yoff = jnp.dot(cg, h.astype(BF16), preferred_element_type=F32) * wp
                tot0, tot1 = _col(tot, e0), _col(tot, e1)
                up = _pair(lo, jnp.exp(tot0 - cols[0]), jnp.exp(tot1 - cols[1]))
                etot = _pair(lo, jnp.exp(tot0), jnp.exp(tot1))
                h_sc[k] = h * etot + lax.dot_general(bg, (xdt * up).astype(BF16), (((0,), (0,)), ((), ())),
                                                     preferred_element_type=F32)
                dskp = _pair(lo, _col(dskv, e0), _col(dskv, e1))
                y_ref[0, :, sl] = ydiag + yoff + xp * dskp

    p16 = pl.BlockSpec((1, 16), lambda b, c: (0, 0))
    return pl.pallas_call(
        body, name=name, grid=(bsz, nc),
        in_specs=[pl.BlockSpec((1, L, 1024), lambda b, c: (b, c, 0)),
                  pl.BlockSpec((1, L, 256), lambda b, c: (b, c, 4)),
                  pl.BlockSpec((1, L, 256), lambda b, c: (b, c, 5)),
                  pl.BlockSpec((1, L, 16), lambda b, c: (b, c, 0)), p16, p16, p16],
        out_specs=[pl.BlockSpec((1, L, 1024), lambda b, c: (b, c, 0)),
                   pl.BlockSpec((1, 1, SSD_PAIRS, 128, 128), lambda b, c: (b, c, 0, 0, 0))],
        out_shape=[jax.ShapeDtypeStruct((bsz, seq, SSD_INNER), F32),
                   jax.ShapeDtypeStruct((bsz, nc, SSD_PAIRS, 128, 128), F32)],
        scratch_shapes=[pltpu.VMEM((SSD_PAIRS, 128, 128), F32)],
        compiler_params=_cp(("parallel", "arbitrary")))(xbc, xbc, xbc, dt_raw, dtb, alog, dsk)


def ssd_bwd(xbc, dt_raw, dtb, alog, dsk, states, dy, name):
    bsz, seq, _ = xbc.shape
    L = SSD_CHUNK
    nc = seq // L

    def body(x_ref, b_ref, c_ref, dt_ref, dtb_ref, alog_ref, dsk_ref, st_ref, dy_ref,
             dxbc_ref, ddt_ref, gdtb_ref, galog_ref, gdsk_ref, dh_sc):
        first = jnp.logical_and(pl.program_id(0) == 0, pl.program_id(1) == 0)

        @pl.when(first)
        def _():
            gdtb_ref[...] = jnp.zeros(gdtb_ref.shape, F32)
            galog_ref[...] = jnp.zeros(galog_ref.shape, F32)
            gdsk_ref[...] = jnp.zeros(gdsk_ref.shape, F32)

        @pl.when(pl.program_id(1) == 0)
        def _():
            dh_sc[...] = jnp.zeros(dh_sc.shape, F32)

        raw, dt, a_neg, acs, acs_t, tot, r_i, c_i, tri_up = _ssd_chunk_prelude(dt_ref, dtb_ref, alog_ref)
        causal = r_i >= c_i
        causal_t = c_i >= r_i
        lo = lax.broadcasted_iota(jnp.int32, (1, 128), 1) < 64
        hi = jnp.logical_not(lo)
        lane16 = lax.broadcasted_iota(jnp.int32, (L, 16), 1)
        lane16r = lax.broadcasted_iota(jnp.int32, (1, 16), 1)
        last_row = lax.broadcasted_iota(jnp.int32, (L, 1), 0) == L - 1
        dskv = dsk_ref[...]
        ds16 = jnp.zeros((L, 16), F32)
        ddt16 = jnp.zeros((L, 16), F32)
        gdsk = jnp.zeros((1, 16), F32)

        def hsum(t, mask):
            return jnp.sum(jnp.where(mask, t, 0.0), axis=1, keepdims=True)

        for grp in range(2):
            gs = slice(grp * 128, (grp + 1) * 128)
            bg = b_ref[0, :, gs].astype(BF16)
            cg = c_ref[0, :, gs].astype(BF16)
            cb = lax.dot_general(cg, bg, (((1,), (1,)), ((), ())), preferred_element_type=F32)
            cbt = lax.dot_general(bg, cg, (((1,), (1,)), ((), ())), preferred_element_type=F32)
            dcb = jnp.zeros((L, L), F32)
            dcbt = jnp.zeros((L, L), F32)
            dc_g = jnp.zeros((L, 128), F32)
            db_g = jnp.zeros((L, 128), F32)
            for kk in range(4):
                k = grp * 4 + kk
                e0, e1 = 2 * k, 2 * k + 1
                sl = slice(k * 128, (k + 1) * 128)
                xp = x_ref[0, :, sl]
                dyp = dy_ref[0, :, sl]
                h = st_ref[0, 0, k]
                dh = dh_sc[k]
                cols = [_col(acs, e0), _col(acs, e1)]
                tots = [_col(tot, e0), _col(tot, e1)]
                dtp = _pair(lo, _col(dt, e0), _col(dt, e1))
                xdt = xp * dtp
                wp = _pair(lo, jnp.exp(cols[0]), jnp.exp(cols[1]))
                up = _pair(lo, jnp.exp(tots[0] - cols[0]), jnp.exp(tots[1] - cols[1]))
                etot = _pair(lo, jnp.exp(tots[0]), jnp.exp(tots[1]))
                hb = h.astype(BF16)
                dhb = dh.astype(BF16)
                yoff = jnp.dot(cg, hb, preferred_element_type=F32) * wp
                dyw = (dyp * wp).astype(BF16)
                dc_g = dc_g + lax.dot_general(dyw, hb, (((1,), (1,)), ((), ())), preferred_element_type=F32)
                dh_in = lax.dot_general(cg, dyw, (((0,), (0,)), ((), ())), preferred_element_type=F32) + dh * etot
                q_mat = jnp.dot(bg, dhb, preferred_element_type=F32)
                db_g = db_g + lax.dot_general((xdt * up).astype(BF16), dhb, (((1,), (1,)), ((), ())),
                                              preferred_element_type=F32)
                dxdt = up * q_mat
                t_u = q_mat * xdt * up
                t_w = dyp * yoff
                t_h = jnp.sum(dh * h, axis=0, keepdims=True) * etot
                for sub, e in enumerate((e0, e1)):
                    msk = lo if sub == 0 else hi
                    rowv = _rowvec(acs_t, e)
                    lam = jnp.exp(jnp.where(causal, cols[sub] - rowv, NEG))
                    lam_t = jnp.exp(jnp.where(causal_t, rowv - cols[sub], NEG))
                    xm = jnp.where(msk, xdt, 0.0).astype(BF16)
                    dym = jnp.where(msk, dyp, 0.0).astype(BF16)
                    g_mat = lax.dot_general(dym, xm, (((1,), (1,)), ((), ())), preferred_element_type=F32)
                    g_t = lax.dot_general(xm, dym, (((1,), (1,)), ((), ())), preferred_element_type=F32)
                    gl = g_mat * lam
                    glt = g_t * lam_t
                    dcb = dcb + gl
                    dcbt = dcbt + glt
                    e_r = jnp.sum(gl * cb, axis=1, keepdims=True)
                    e_c = jnp.sum(glt * cbt, axis=1, keepdims=True)
                    dxdt = dxdt + jnp.dot((cbt * lam_t).astype(BF16), dym, preferred_element_type=F32)
                    u_rows = hsum(t_u, msk)
                    ds_tot = jnp.sum(u_rows, axis=0, keepdims=True) + hsum(t_h, msk)
                    ds_col = e_r - e_c + hsum(t_w, msk) - u_rows + jnp.where(last_row, ds_tot, 0.0)
                    ds16 = ds16 + jnp.where(lane16 == e, ds_col, 0.0)
                dxp = dxdt * xp
                dyx = dyp * xp
                for sub, e in enumerate((e0, e1)):
                    msk = lo if sub == 0 else hi
                    ddt16 = ddt16 + jnp.where(lane16 == e, hsum(dxp, msk), 0.0)
                    gdsk = gdsk + jnp.where(lane16r == e, jnp.sum(hsum(dyx, msk), axis=0, keepdims=True), 0.0)
                dskp = _pair(lo, _col(dskv, e0), _col(dskv, e1))
                dxbc_ref[0, :, sl] = dxdt * dtp + dyp * dskp
                dh_sc[k] = dh_in
            dc_g = dc_g + jnp.dot(dcb.astype(BF16), bg, preferred_element_type=F32)
            db_g = db_g + jnp.dot(dcbt.astype(BF16), cg, preferred_element_type=F32)
            dxbc_ref[0, :, 1024 + grp * 128:1024 + (grp + 1) * 128] = db_g
            dxbc_ref[0, :, 1280 + grp * 128:1280 + (grp + 1) * 128] = dc_g
        da16 = jnp.dot(tri_up, ds16, precision=HI, preferred_element_type=F32)
        ddt16 = ddt16 + da16 * a_neg
        d_aneg = jnp.sum(da16 * dt, axis=0, keepdims=True)
        ddt_raw = ddt16 * _sigmoid(raw)
        ddt_ref[0] = ddt_raw
        gdtb_ref[...] += jnp.sum(ddt_raw, axis=0, keepdims=True)
        galog_ref[...] += d_aneg * a_neg
        gdsk_ref[...] += gdsk

    p16 = pl.BlockSpec((1, 16), lambda b, c: (0, 0))
    rev = lambda b, c: (b, nc - 1 - c, 0)
    return pl.pallas_call(
        body, name=name, grid=(bsz, nc),
        in_specs=[pl.BlockSpec((1, L, 1024), rev),
                  pl.BlockSpec((1, L, 256), lambda b, c: (b, nc - 1 - c, 4)),
                  pl.BlockSpec((1, L, 256), lambda b, c: (b, nc - 1 - c, 5)),
                  pl.BlockSpec((1, L, 16), rev), p16, p16, p16,
                  pl.BlockSpec((1, 1, SSD_PAIRS, 128, 128), lambda b, c: (b, nc - 1 - c, 0, 0, 0)),
                  pl.BlockSpec((1, L, 1024), rev)],
        out_specs=[pl.BlockSpec((1, L, CONV_DIM), rev), pl.BlockSpec((1, L, 16), rev), p16, p16, p16],
        out_shape=[jax.ShapeDtypeStruct((bsz, seq, CONV_DIM), F32), jax.ShapeDtypeStruct((bsz, seq, 16), F32),
                   jax.ShapeDtypeStruct((1, 16), F32), jax.ShapeDtypeStruct((1, 16), F32),
                   jax.ShapeDtypeStruct((1, 16), F32)],
        scratch_shapes=[pltpu.VMEM((SSD_PAIRS, 128, 128), F32)],
        compiler_params=_cp(("arbitrary", "arbitrary")))(xbc, xbc, xbc, dt_raw, dtb, alog, dsk, states, dy)


TAIL_BLK = TAIL_COL // 128


def _rope_apply(v, c, sa, sb):
    return v * c + pltpu.roll(v, 96, 1) * sa + pltpu.roll(v, 32, 1) * sb


def _rope_transposed(v, c, sa, sb):
    return v * c + pltpu.roll(v * sa, 32, 1) + pltpu.roll(v * sb, 96, 1)


def _head_rstd(lo, hi):
    ss = jnp.sum(lo * lo, axis=-1, keepdims=True) + jnp.sum(hi * hi, axis=-1, keepdims=True)
    return lax.rsqrt(ss * (1.0 / MLA_QK) + EPS)


def qk_prep_fwd(q_raw3, kv3, proj3, gq, gk, tabs, name, ts=256):
    bsz, seq, width = q_raw3.shape
    ts = min(ts, seq)

    def body(q_ref, kv_ref, tail_ref, gq_ref, gk_ref, c_ref, sa_ref, sb_ref, qo_ref, ko_ref):
        c, sa, sb = c_ref[0], sa_ref[0], sb_ref[0]
        k_hi = jnp.where(lax.broadcasted_iota(jnp.int32, (1, 128), 1) < 64, tail_ref[0], 0.0)
        for h in range(MLA_HEADS):
            lo_sl, hi_sl = slice(h * 256, h * 256 + 128), slice(h * 256 + 128, h * 256 + 256)
            for lo, hi, g_ref, o_ref in ((q_ref[0, :, lo_sl], q_ref[0, :, hi_sl], gq_ref, qo_ref),
                                         (kv_ref[0, :, lo_sl], k_hi, gk_ref, ko_ref)):
                r = _head_rstd(lo, hi)
                o_ref[0, :, lo_sl] = ((lo * r) * g_ref[:, lo_sl]).astype(BF16)
                o_ref[0, :, hi_sl] = _rope_apply((hi * r) * g_ref[:, hi_sl], c, sa, sb).astype(BF16)

    wide = pl.BlockSpec((1, ts, width), lambda b, i: (b, i, 0))
    tab = pl.BlockSpec((1, ts, 128), lambda b, i: (b, i, 0))
    gain = pl.BlockSpec((1, width), lambda b, i: (0, 0))
    return pl.pallas_call(
        body, name=name, grid=(bsz, seq // ts),
        in_specs=[wide, wide, pl.BlockSpec((1, ts, 128), lambda b, i: (b, i, TAIL_BLK)), gain, gain, tab, tab, tab],
        out_specs=[wide, wide], out_shape=[jax.ShapeDtypeStruct(q_raw3.shape, BF16)] * 2,
        compiler_params=_cp(("parallel", "parallel")))(q_raw3, kv3, proj3, gq, gk, *tabs)


def qk_prep_bwd(q_raw3, kv3, proj3, gq, gk, tabs, dq, dk, dv, name, ts=256):
    bsz, seq, width = q_raw3.shape
    ts = min(ts, seq)

    def body(q_ref, kv_ref, tail_ref, gq_ref, gk_ref, c_ref, sa_ref, sb_ref, dq_ref, dk_ref, dv_ref,
             dqr_ref, dkv_ref, dkr_ref, dgq_ref, dgk_ref):
        @pl.when(jnp.logical_and(pl.program_id(0) == 0, pl.program_id(1) == 0))
        def _():
            dgq_ref[...] = jnp.zeros(dgq_ref.shape, F32)
            dgk_ref[...] = jnp.zeros(dgk_ref.shape, F32)

        c, sa, sb = c_ref[0], sa_ref[0], sb_ref[0]
        k_hi = jnp.where(lax.broadcasted_iota(jnp.int32, (1, 128), 1) < 64, tail_ref[0], 0.0)
        dk_rope = jnp.zeros(k_hi.shape, F32)
        for h in range(MLA_HEADS):
            lo_sl, hi_sl = slice(h * 256, h * 256 + 128), slice(h * 256 + 128, h * 256 + 256)
            for is_k, lo, hi, g_ref, d_ref, dg_ref in ((False, q_ref[0, :, lo_sl], q_ref[0, :, hi_sl], gq_ref, dq_ref, dgq_ref),
                                                       (True, kv_ref[0, :, lo_sl], k_hi, gk_ref, dk_ref, dgk_ref)):
                r = _head_rstd(lo, hi)
                xh_lo, xh_hi = lo * r, hi * r
                dy_lo = d_ref[0, :, lo_sl]
                dy_hi = _rope_transposed(d_ref[0, :, hi_sl], c, sa, sb)
                dyg_lo, dyg_hi = dy_lo * g_ref[:, lo_sl], dy_hi * g_ref[:, hi_sl]
                cc = (jnp.sum(dyg_lo * xh_lo, axis=-1, keepdims=True)
                      + jnp.sum(dyg_hi * xh_hi, axis=-1, keepdims=True)) * (1.0 / MLA_QK)
                dx_lo, dx_hi = r * (dyg_lo - xh_lo * cc), r * (dyg_hi - xh_hi * cc)
                dg_ref[:, lo_sl] += jnp.sum(dy_lo * xh_lo, axis=0, keepdims=True)
                dg_ref[:, hi_sl] += jnp.sum(dy_hi * xh_hi, axis=0, keepdims=True)
                if is_k:
                    dkv_ref[0, :, lo_sl] = dx_lo
                    dkv_ref[0, :, hi_sl] = dv_ref[0, :, h * 128:(h + 1) * 128]
                    dk_rope = dk_rope + dx_hi
                else:
                    dqr_ref[0, :, lo_sl] = dx_lo
                    dqr_ref[0, :, hi_sl] = dx_hi
        dkr_ref[0] = dk_rope

    wide = pl.BlockSpec((1, ts, width), lambda b, i: (b, i, 0))
    tab = pl.BlockSpec((1, ts, 128), lambda b, i: (b, i, 0))
    gain = pl.BlockSpec((1, width), lambda b, i: (0, 0))
    return pl.pallas_call(
        body, name=name, grid=(bsz, seq // ts),
        in_specs=[wide, wide, pl.BlockSpec((1, ts, 128), lambda b, i: (b, i, TAIL_BLK)), gain, gain, tab, tab, tab,
                  wide, wide, pl.BlockSpec((1, ts, 1024), lambda b, i: (b, i, 0))],
        out_specs=[wide, wide, tab, gain, gain],
        out_shape=[jax.ShapeDtypeStruct(q_raw3.shape, F32), jax.ShapeDtypeStruct(q_raw3.shape, F32),
                   jax.ShapeDtypeStruct((bsz, seq, 128), F32), jax.ShapeDtypeStruct((1, width), F32),
                   jax.ShapeDtypeStruct((1, width), F32)],
        compiler_params=_cp(("arbitrary", "arbitrary")))(q_raw3, kv3, proj3, gq, gk, *tabs, dq, dk, dv)


def _scores(q_ref, k_ref, scale, causal, i, j, tq, tk):
    s = lax.dot_general(q_ref[0].astype(BF16), k_ref[0].astype(BF16), (((1,), (1,)), ((), ())),
                        preferred_element_type=F32) * scale
    if not causal:
        return s, None
    r = i * tq + lax.broadcasted_iota(jnp.int32, (tq, tk), 0)
    c = j * tk + lax.broadcasted_iota(jnp.int32, (tq, tk), 1)
    mask = r >= c
    return jnp.where(mask, s, NEG), mask


def attn_fwd(q, k, v, *, heads, dk, dv, v_mul, v_off, causal, scale, tq, tk, name, rider=None):
    bsz, sq, _ = q.shape
    sk = k.shape[1]
    tq, tk = min(tq, sq), min(tk, sk)
    nq, nk = sq // tq, sk // tk

    def body(q_ref, k_ref, v_ref, o_ref, lse_ref, m_sc, l_sc, acc_sc):
        i, j = pl.program_id(2), pl.program_id(3)

        @pl.when(j == 0)
        def _():
            m_sc[...] = jnp.full(m_sc.shape, NEG, F32)
            l_sc[...] = jnp.zeros(l_sc.shape, F32)
            acc_sc[...] = jnp.zeros(acc_sc.shape, F32)

        def compute(masked=False):
            s, _ = _scores(q_ref, k_ref, scale, masked, i, j, tq, tk)
            m_prev = m_sc[...]
            m_new = jnp.maximum(m_prev, jnp.max(s, axis=1, keepdims=True))
            p = jnp.exp(s - m_new)
            alpha = jnp.exp(m_prev - m_new)
            l_sc[...] = alpha * l_sc[...] + jnp.sum(p, axis=1, keepdims=True)
            acc_sc[...] = alpha * acc_sc[...] + jnp.dot(p.astype(BF16), v_ref[0].astype(BF16),
                                                        preferred_element_type=F32)
            m_sc[...] = m_new

        if causal:
            pl.when(j < i)(compute)
            pl.when(j == i)(functools.partial(compute, True))
        else:
            compute()

        @pl.when(j == nk - 1)
        def _():
            o_ref[0] = acc_sc[...] / l_sc[...]
            lse_ref[0, 0] = m_sc[...] + jnp.log(l_sc[...])

    kvj = (lambda i, j: jnp.minimum(i, j)) if causal else (lambda i, j: j)
    grid = (bsz, heads, nq, nk)
    in_specs = [pl.BlockSpec((1, tq, dk), lambda b, h, i, j: (b, i, h)),
                pl.BlockSpec((1, tk, dk), lambda b, h, i, j: (b, kvj(i, j), h)),
                pl.BlockSpec((1, tk, dv), lambda b, h, i, j: (b, kvj(i, j), v_mul * h + v_off))]
    out_specs = [pl.BlockSpec((1, tq, dv), lambda b, h, i, j: (b, i, h)),
                 pl.BlockSpec((1, 1, tq, 1), lambda b, h, i, j: (b, h, i, 0))]
    out_shape = [jax.ShapeDtypeStruct((bsz, sq, heads * dv), F32), jax.ShapeDtypeStruct((bsz, heads, sq, 1), F32)]
    scratch = [pltpu.VMEM((tq, 1), F32), pltpu.VMEM((tq, 1), F32), pltpu.VMEM((tq, dv), F32)]
    args, sem, io_aliases = [q, k, v], ("parallel", "parallel", "parallel", "arbitrary"), {}
    if rider is not None:
        body, in_specs, out_specs, out_shape, scratch, args, io_aliases = _attach(
            rider, body, grid, in_specs, out_specs, out_shape, scratch, args)
        sem = ("arbitrary",) * 4
    outs = pl.pallas_call(body, name=name, grid=grid, in_specs=in_specs, out_specs=out_specs, out_shape=out_shape,
                          scratch_shapes=scratch, input_output_aliases=io_aliases, compiler_params=_cp(sem))(*args)
    return outs[0], outs[1], outs[2:]


def _attn_grads(q_ref, k_ref, v_ref, do_ref, o_ref, lse_ref, scale, causal, i, j, tq, tk):
    s, mask = _scores(q_ref, k_ref, scale, causal, i, j, tq, tk)
    p = jnp.exp(s - lse_ref[0, 0])
    if mask is not None:
        p = jnp.where(mask, p, 0.0)
    do = do_ref[0]
    dp = lax.dot_general(do.astype(BF16), v_ref[0].astype(BF16), (((1,), (1,)), ((), ())),
                         preferred_element_type=F32)
    delta = jnp.sum(do * o_ref[0], axis=1, keepdims=True)
    ds = p * (dp - delta) * scale
    return p, ds


def attn_bwd(q, k, v, do, do_off, o, lse, *, heads, dk, dv, v_mul, v_off, causal, scale, tq, tk, name,
             rider_dq=None, rider_dkv=None):
    bsz, sq, _ = q.shape
    sk = k.shape[1]
    tq, tk = min(tq, sq), min(tk, sk)
    nq, nk = sq // tq, sk // tk

    def dq_body(q_ref, k_ref, v_ref, do_ref, o_ref, lse_ref, dq_ref, acc_sc):
        i, j = pl.program_id(2), pl.program_id(3)

        @pl.when(j == 0)
        def _():
            acc_sc[...] = jnp.zeros(acc_sc.shape, F32)

        def compute(masked=False):
            _, ds = _attn_grads(q_ref, k_ref, v_ref, do_ref, o_ref, lse_ref, scale, masked, i, j, tq, tk)
            acc_sc[...] += jnp.dot(ds.astype(BF16), k_ref[0].astype(BF16), preferred_element_type=F32)

        if causal:
            pl.when(j < i)(compute)
            pl.when(j == i)(functools.partial(compute, True))
        else:
            compute()

        @pl.when(j == nk - 1)
        def _():
            dq_ref[0] = acc_sc[...]

    kvj = (lambda i, j: jnp.minimum(i, j)) if causal else (lambda i, j: j)
    grid = (bsz, heads, nq, nk)
    in_specs = [pl.BlockSpec((1, tq, dk), lambda b, h, i, j: (b, i, h)),
                pl.BlockSpec((1, tk, dk), lambda b, h, i, j: (b, kvj(i, j), h)),
                pl.BlockSpec((1, tk, dv), lambda b, h, i, j: (b, kvj(i, j), v_mul * h + v_off)),
                pl.BlockSpec((1, tq, dv), lambda b, h, i, j: (b, i, h + do_off)),
                pl.BlockSpec((1, tq, dv), lambda b, h, i, j: (b, i, h)),
                pl.BlockSpec((1, 1, tq, 1), lambda b, h, i, j: (b, h, i, 0))]
    out_specs = [pl.BlockSpec((1, tq, dk), lambda b, h, i, j: (b, i, h))]
    out_shape = [jax.ShapeDtypeStruct((bsz, sq, heads * dk), F32)]
    scratch, args = [pltpu.VMEM((tq, dk), F32)], [q, k, v, do, o, lse]
    sem, io_aliases = ("parallel", "parallel", "parallel", "arbitrary"), {}
    if rider_dq is not None:
        dq_body, in_specs, out_specs, out_shape, scratch, args, io_aliases = _attach(
            rider_dq, dq_body, grid, in_specs, out_specs, out_shape, scratch, args)
        sem = ("arbitrary",) * 4
    dq_outs = pl.pallas_call(dq_body, name=name + "_dq", grid=grid, in_specs=in_specs, out_specs=out_specs,
                             out_shape=out_shape, scratch_shapes=scratch, input_output_aliases=io_aliases,
                             compiler_params=_cp(sem))(*args)
    dq = dq_outs[0]

    def dkv_body(q_ref, k_ref, v_ref, do_ref, o_ref, lse_ref, dk_ref, dv_ref, dk_sc, dv_sc):
        j, i = pl.program_id(2), pl.program_id(3)

        @pl.when(i == 0)
        def _():
            dk_sc[...] = jnp.zeros(dk_sc.shape, F32)
            dv_sc[...] = jnp.zeros(dv_sc.shape, F32)

        def compute(masked=False):
            p, ds = _attn_grads(q_ref, k_ref, v_ref, do_ref, o_ref, lse_ref, scale, masked, i, j, tq, tk)
            dv_sc[...] += lax.dot_general(p.astype(BF16), do_ref[0].astype(BF16), (((0,), (0,)), ((), ())),
                                          preferred_element_type=F32)
            dk_sc[...] += lax.dot_general(ds.astype(BF16), q_ref[0].astype(BF16), (((0,), (0,)), ((), ())),
                                          preferred_element_type=F32)

        if causal:
            pl.when(i > j)(compute)
            pl.when(i == j)(functools.partial(compute, True))
        else:
            compute()

        @pl.when(i == nq - 1)
        def _():
            dk_ref[0] = dk_sc[...]
            dv_ref[0] = dv_sc[...]

    qi = (lambda j, i: jnp.maximum(i, j)) if causal else (lambda j, i: i)
    grid = (bsz, heads, nk, nq)
    in_specs = [pl.BlockSpec((1, tq, dk), lambda b, h, j, i: (b, qi(j, i), h)),
                pl.BlockSpec((1, tk, dk), lambda b, h, j, i: (b, j, h)),
                pl.BlockSpec((1, tk, dv), lambda b, h, j, i: (b, j, v_mul * h + v_off)),
                pl.BlockSpec((1, tq, dv), lambda b, h, j, i: (b, qi(j, i), h + do_off)),
                pl.BlockSpec((1, tq, dv), lambda b, h, j, i: (b, qi(j, i), h)),
                pl.BlockSpec((1, 1, tq, 1), lambda b, h, j, i: (b, h, qi(j, i), 0))]
    out_specs = [pl.BlockSpec((1, tk, dk), lambda b, h, j, i: (b, j, h)),
                 pl.BlockSpec((1, tk, dv), lambda b, h, j, i: (b, j, h))]
    out_shape = [jax.ShapeDtypeStruct((bsz, sk, heads * dk), F32), jax.ShapeDtypeStruct((bsz, sk, heads * dv), F32)]
    scratch, args = [pltpu.VMEM((tk, dk), F32), pltpu.VMEM((tk, dv), F32)], [q, k, v, do, o, lse]
    sem, io_aliases = ("parallel", "parallel", "parallel", "arbitrary"), {}
    if rider_dkv is not None:
        dkv_body, in_specs, out_specs, out_shape, scratch, args, io_aliases = _attach(
            rider_dkv, dkv_body, grid, in_specs, out_specs, out_shape, scratch, args)
        sem = ("arbitrary",) * 4
    dkv_outs = pl.pallas_call(dkv_body, name=name + "_dkv", grid=grid, in_specs=in_specs, out_specs=out_specs,
                              out_shape=out_shape, scratch_shapes=scratch, input_output_aliases=io_aliases,
                              compiler_params=_cp(sem))(*args)
    return dq, dkv_outs[0], dkv_outs[1], dq_outs[1:], dkv_outs[2:]


def loss_head(y, target, name, tr=512):
    rows, width = y.shape
    tr = min(tr, rows)

    def body(y_ref, t_ref, dy_ref, l_ref):
        @pl.when(pl.program_id(0) == 0)
        def _():
            l_ref[...] = jnp.zeros(l_ref.shape, F32)

        err = y_ref[...] - t_ref[...]
        dy_ref[...] = err * (1.0 / width)
        l_ref[...] += jnp.sum(jnp.sum(err * err, axis=1, keepdims=True), axis=0, keepdims=True) * (0.5 / width)

    return pl.pallas_call(
        body, name=name, grid=(rows // tr,),
        in_specs=[pl.BlockSpec((tr, width), lambda i: (i, 0)), pl.BlockSpec((tr, width), lambda i: (i, 0))],
        out_specs=[pl.BlockSpec((tr, width), lambda i: (i, 0)), pl.BlockSpec((1, 1), lambda i: (0, 0))],
        out_shape=[jax.ShapeDtypeStruct((rows, width), F32), jax.ShapeDtypeStruct((1, 1), F32)],
        compiler_params=_cp(("arbitrary",)))(y, target)


def _mesh_pos():
    return lax.axis_index("x"), lax.axis_index("y"), lax.axis_index("c")


def all_gather(blocks, name):
    n_arr = len(blocks)

    def body(*refs):
        x_refs, out_refs = refs[:n_arr], refs[n_arr:2 * n_arr]
        send_sems, recv_sems, local_sems = refs[2 * n_arr:]
        x, y, c = _mesh_pos()
        me, sibling = (x, y, c), (x, y, 1 - c)
        chips = [(1 - x, y), (x, 1 - y), (1 - x, 1 - y)]

        def slot(a, px, py, pc):
            return out_refs[a].at[4 * px + 2 * py + pc]

        def copy(a, k, blk, to, src=None):
            return pltpu.make_async_remote_copy(
                src_ref=slot(a, *blk) if src is None else src, dst_ref=slot(a, *blk),
                send_sem=send_sems.at[7 * a + k], recv_sem=recv_sems.at[7 * a + k],
                device_id=to, device_id_type=pl.DeviceIdType.MESH)

        mine = [pltpu.make_async_copy(x_refs[a], slot(a, *me), local_sems.at[a]) for a in range(n_arr)]
        for cp in mine:
            cp.start()
        first = []
        for a in range(n_arr):
            first.append(copy(a, 0, me, sibling, src=x_refs[a]))
            first += [copy(a, 1 + j, me, (*chip, c), src=x_refs[a]) for j, chip in enumerate(chips)]
        for cp in first:
            cp.start()
        passed = []
        for j, chip in enumerate(chips):
            for a in range(n_arr):
                copy(a, 1 + j, (*chip, c), me).wait_recv()
                passed.append(copy(a, 4 + j, (*chip, c), sibling))
                passed[-1].start()
        for a in range(n_arr):
            copy(a, 0, sibling, me).wait_recv()
            for j, chip in enumerate(chips):
                copy(a, 4 + j, (*chip, 1 - c), me).wait_recv()
        for cp in first + passed:
            cp.wait_send()
        for cp in mine:
            cp.wait()

    return pl.pallas_call(
        body, name=name,
        out_shape=[jax.ShapeDtypeStruct((N_DEV,) + b.shape, b.dtype) for b in blocks],
        in_specs=[pl.BlockSpec(memory_space=pl.ANY)] * n_arr,
        out_specs=[pl.BlockSpec(memory_space=pl.ANY)] * n_arr,
        scratch_shapes=[pltpu.SemaphoreType.DMA((7 * n_arr,)), pltpu.SemaphoreType.DMA((7 * n_arr,)),
                        pltpu.SemaphoreType.DMA((n_arr,))],
    )(*blocks)


def _slot(px, py, pc):
    return 4 * px + 2 * py + pc


def gather_near_rider(blocks):
    n_arr = len(blocks)

    def parts(rins, routs, sems):
        send_sems, recv_sems, local_sems = sems
        x, y, c = _mesh_pos()
        peers = [(x, y, 1 - c), (1 - x, y, c), (x, 1 - y, c)]
        local = [pltpu.make_async_copy(rins[a], routs[a].at[_slot(x, y, c)], local_sems.at[a]) for a in range(n_arr)]
        sends = [pltpu.make_async_remote_copy(
            src_ref=rins[a], dst_ref=routs[a].at[_slot(x, y, c)], send_sem=send_sems.at[3 * a + k],
            recv_sem=recv_sems.at[3 * a + k], device_id=p, device_id_type=pl.DeviceIdType.MESH)
            for a in range(n_arr) for k, p in enumerate(peers)]
        recvs = [pltpu.make_async_remote_copy(
            src_ref=rins[a], dst_ref=routs[a].at[_slot(*p)], send_sem=send_sems.at[3 * a + k],
            recv_sem=recv_sems.at[3 * a + k], device_id=p, device_id_type=pl.DeviceIdType.MESH)
            for a in range(n_arr) for k, p in enumerate(peers)]
        return local, sends, recvs

    def start(rins, routs, sems):
        local, sends, _ = parts(rins, routs, sems)
        for cp in local + sends:
            cp.start()

    def finish(rins, routs, sems):
        local, sends, recvs = parts(rins, routs, sems)
        for cp in recvs:
            cp.wait_recv()
        for cp in sends:
            cp.wait_send()
        for cp in local:
            cp.wait()

    return Rider(blocks, [jax.ShapeDtypeStruct((N_DEV,) + b.shape, b.dtype) for b in blocks],
                 [pltpu.SemaphoreType.DMA((3 * n_arr,)), pltpu.SemaphoreType.DMA((3 * n_arr,)),
                  pltpu.SemaphoreType.DMA((n_arr,))], start, finish)


def gather_diagonal_rider(blocks, gathered):
    n_arr = len(blocks)

    def parts(rins, routs, sems):
        send_sems, recv_sems = sems
        x, y, c = _mesh_pos()
        p = (1 - x, 1 - y, c)
        sends = [pltpu.make_async_remote_copy(
            src_ref=rins[a], dst_ref=routs[a].at[_slot(x, y, c)], send_sem=send_sems.at[a],
            recv_sem=recv_sems.at[a], device_id=p, device_id_type=pl.DeviceIdType.MESH) for a in range(n_arr)]
        recvs = [pltpu.make_async_remote_copy(
            src_ref=rins[a], dst_ref=routs[a].at[_slot(*p)], send_sem=send_sems.at[a],
            recv_sem=recv_sems.at[a], device_id=p, device_id_type=pl.DeviceIdType.MESH) for a in range(n_arr)]
        return sends, recvs

    def start(rins, routs, sems):
        for cp in parts(rins, routs, sems)[0]:
            cp.start()

    def finish(rins, routs, sems):
        sends, recvs = parts(rins, routs, sems)
        for cp in recvs:
            cp.wait_recv()
        for cp in sends:
            cp.wait_send()

    return Rider(list(blocks) + list(gathered), [jax.ShapeDtypeStruct(g.shape, g.dtype) for g in gathered],
                 [pltpu.SemaphoreType.DMA((n_arr,)), pltpu.SemaphoreType.DMA((n_arr,))], start, finish,
                 aliases={n_arr + a: a for a in range(n_arr)})


def gather_forward_rider(gathered):
    n_arr = len(gathered)

    def parts(rins, routs, sems):
        send_sems, recv_sems = sems
        x, y, c = _mesh_pos()
        chips = [(1 - x, y), (x, 1 - y), (1 - x, 1 - y)]
        sends = [pltpu.make_async_remote_copy(
            src_ref=routs[a].at[_slot(px, py, c)], dst_ref=routs[a].at[_slot(px, py, c)],
            send_sem=send_sems.at[3 * a + k], recv_sem=recv_sems.at[3 * a + k],
            device_id=(x, y, 1 - c), device_id_type=pl.DeviceIdType.MESH)
            for a in range(n_arr) for k, (px, py) in enumerate(chips)]
        recvs = [pltpu.make_async_remote_copy(
            src_ref=routs[a].at[_slot(px, py, c)], dst_ref=routs[a].at[_slot(px, py, 1 - c)],
            send_sem=send_sems.at[3 * a + k], recv_sem=recv_sems.at[3 * a + k],
            device_id=(x, y, 1 - c), device_id_type=pl.DeviceIdType.MESH)
            for a in range(n_arr) for k, (px, py) in enumerate(chips)]
        return sends, recvs

    def start(rins, routs, sems):
        for cp in parts(rins, routs, sems)[0]:
            cp.start()

    def finish(rins, routs, sems):
        sends, recvs = parts(rins, routs, sems)
        for cp in recvs:
            cp.wait_recv()
        for cp in sends:
            cp.wait_send()

    return Rider(gathered, [jax.ShapeDtypeStruct(g.shape, g.dtype) for g in gathered],
                 [pltpu.SemaphoreType.DMA((3 * n_arr,)), pltpu.SemaphoreType.DMA((3 * n_arr,))], start, finish,
                 aliases={a: a for a in range(n_arr)})


def sibling_stage_rider(blobs):
    n_arr = len(blobs)

    def copies(rins, routs, sems):
        send_sems, recv_sems = sems
        x, y, c = _mesh_pos()
        return [pltpu.make_async_remote_copy(
            src_ref=rins[a].at[2 * j + (1 - c)], dst_ref=routs[a].at[j],
            send_sem=send_sems.at[4 * a + j], recv_sem=recv_sems.at[4 * a + j],
            device_id=(x, y, 1 - c), device_id_type=pl.DeviceIdType.MESH)
            for a in range(n_arr) for j in range(4)]

    def start(rins, routs, sems):
        for cp in copies(rins, routs, sems):
            cp.start()

    def finish(rins, routs, sems):
        for cp in copies(rins, routs, sems):
            cp.wait()

    return Rider(blobs, [jax.ShapeDtypeStruct((4,) + b.shape[1:], b.dtype) for b in blobs],
                 [pltpu.SemaphoreType.DMA((4 * n_arr,)), pltpu.SemaphoreType.DMA((4 * n_arr,))], start, finish)


def chip_partial(blob, from_sibling, name, tr):
    _, rows, cols = blob.shape
    c_idx = lax.axis_index("c").astype(jnp.int32).reshape(1)

    def body(c_ref, b_ref, s_ref, o_ref):
        o_ref[...] = (b_ref[...].astype(F32) + s_ref[...].astype(F32)).astype(o_ref.dtype)

    return pl.pallas_call(
        body, name=name,
        grid_spec=pltpu.PrefetchScalarGridSpec(
            num_scalar_prefetch=1, grid=(4, rows // tr),
            in_specs=[pl.BlockSpec((1, tr, cols), lambda j, i, c_ref: (2 * j + c_ref[0], i, 0)),
                      pl.BlockSpec((1, tr, cols), lambda j, i, c_ref: (j, i, 0))],
            out_specs=pl.BlockSpec((1, tr, cols), lambda j, i, c_ref: (j, i, 0))),
        out_shape=jax.ShapeDtypeStruct((4, rows, cols), blob.dtype),
        compiler_params=_cp(("parallel", "parallel")))(c_idx, blob, from_sibling)


def chip_stage_rider(partials, diagonal, row_ranges, received=None):
    n_arr = len(partials)
    n_remote = 1 if diagonal else 2

    def copies(rins, routs, sems):
        send_sems, recv_sems = sems[0], sems[1]
        x, y, c = _mesh_pos()
        chips = [(1 - x, 1 - y)] if diagonal else [(1 - x, y), (x, 1 - y)]
        return [pltpu.make_async_remote_copy(
            src_ref=rins[a].at[2 * px + py, pl.ds(*row_ranges[a])],
            dst_ref=routs[a].at[k if diagonal else 1 + k, pl.ds(*row_ranges[a])],
            send_sem=send_sems.at[n_remote * a + k], recv_sem=recv_sems.at[n_remote * a + k],
            device_id=(px, py, c), device_id_type=pl.DeviceIdType.MESH)
            for a in range(n_arr) for k, (px, py) in enumerate(chips)]

    def local(rins, routs, sems):
        x, y, _ = _mesh_pos()
        return [pltpu.make_async_copy(rins[a].at[2 * x + y, pl.ds(*row_ranges[a])],
                                      routs[a].at[0, pl.ds(*row_ranges[a])], sems[2].at[a]) for a in range(n_arr)]

    def start(rins, routs, sems):
        for cp in copies(rins, routs, sems) + ([] if diagonal else local(rins, routs, sems)):
            cp.start()

    def finish(rins, routs, sems):
        for cp in copies(rins, routs, sems) + ([] if diagonal else local(rins, routs, sems)):
            cp.wait()

    sem_list = [pltpu.SemaphoreType.DMA((n_remote * n_arr,)), pltpu.SemaphoreType.DMA((n_remote * n_arr,))]
    if not diagonal:
        sem_list.append(pltpu.SemaphoreType.DMA((n_arr,)))
    out_shapes = [jax.ShapeDtypeStruct((1 if diagonal else 3,) + p.shape[1:], p.dtype) for p in partials]
    if received is None:
        return Rider(partials, out_shapes, sem_list, start, finish)
    return Rider(list(partials) + list(received), out_shapes, sem_list, start, finish,
                 aliases={n_arr + a: a for a in range(n_arr)})


def _halves(partials):
    first, second = [], []
    for p in partials:
        rows = p.shape[1]
        cut = (rows // 32 + (rows // 16) % 2) * 16
        first.append((0, cut))
        second.append((cut, rows - cut))
    return first, second


def adamw(parts, w, m, v, layer, earlier, name, tr=128):
    n_layers, rows, cols = w.shape
    tr = min(tr, rows)
    n_arr = len(parts)
    n_pass = 0 if earlier is None else 4
    c1 = 1.0 - ADAM_B1 ** ADAM_STEP
    c2 = 1.0 - ADAM_B2 ** ADAM_STEP

    def body(*refs):
        p_refs = refs[:n_arr]
        w_ref, m_ref, v_ref = refs[n_arr:n_arr + 3]
        g_ref, d_ref, nm_ref, nv_ref = refs[n_arr + 3 + n_pass:]
        g = None
        for p_ref in p_refs:
            for s in range(p_ref.shape[0]):
                term = p_ref[s].astype(F32)
                g = term if g is None else g + term
        nm = ADAM_B1 * m_ref[0] + (1.0 - ADAM_B1) * g
        nv = ADAM_B2 * v_ref[0] + (1.0 - ADAM_B2) * (g * g)
        m_hat = nm / c1
        v_hat = nv / c2
        g_ref[0] = g
        d_ref[0] = -ADAM_LR * (m_hat / (jnp.sqrt(v_hat) + ADAM_EPS) + ADAM_WD * w_ref[0])
        nm_ref[0] = nm
        nv_ref[0] = nv

    blk = pl.BlockSpec((1, tr, cols), lambda i: (layer, i, 0))
    in_specs = [pl.BlockSpec((p.shape[0], tr, cols), lambda i: (0, i, 0)) for p in parts] + [blk, blk, blk]
    args = list(parts) + [w, m, v]
    aliases = {}
    if earlier is not None:
        in_specs += [pl.BlockSpec(memory_space=pl.ANY)] * 4
        aliases = {n_arr + 3 + k: k for k in range(4)}
        args += list(earlier)
    return pl.pallas_call(
        body, name=name, grid=(rows // tr,), in_specs=in_specs, out_specs=[blk, blk, blk, blk],
        out_shape=[jax.ShapeDtypeStruct((n_layers, rows, cols), F32)] * 4, input_output_aliases=aliases,
        compiler_params=_cp(("parallel",)))(*args)


def _pack_local(fam, shards, dtype):
    parts = []
    for n, rows, prows, transposed in fam:
        a = shards[n].T if transposed else shards[n]
        parts.append(jnp.pad(a.astype(dtype), ((0, prows - rows), (0, 0))))
    return jnp.concatenate(parts, axis=0)


def _unpack_gathered(fam, gathered):
    out, off = {}, 0
    for n, rows, prows, _ in fam:
        out[n] = gathered[:, off:off + rows].reshape(N_DEV * rows, gathered.shape[2])
        off += prows
    return out


def _pack_full_grads(fam, grads):
    segs = []
    for n, rows, prows, _ in fam:
        g = grads[n].astype(BF16)
        g = g.reshape(N_DEV, rows, g.shape[1])
        segs.append(jnp.pad(g, ((0, 0), (0, prows - rows), (0, 0))))
    return jnp.concatenate(segs, axis=1)


def _pack_local_layers(fam, src, dtype):
    parts = []
    for n, rows, prows, transposed in fam:
        a = src[n].transpose(0, 2, 1) if transposed else src[n]
        parts.append(jnp.pad(a.astype(dtype), ((0, 0), (0, prows - rows), (0, 0))))
    return jnp.concatenate(parts, axis=1)


def _unpack_local_layers(fam, blob):
    out, off = {}, 0
    for n, rows, prows, transposed in fam:
        a = blob[:, off:off + rows]
        out[n] = a.transpose(0, 2, 1) if transposed else a
        off += prows
    return out


def _pack_small(vals):
    flat = jnp.concatenate([vals[n].reshape(-1).astype(F32) for n, _ in SMALL])
    flat = jnp.pad(flat, (0, SMALL_ROWS * 128 - flat.shape[0]))
    return flat.reshape(SMALL_ROWS, 128)


def _unpack_small(flat2d):
    flat = flat2d.reshape(-1)
    out, off = {}, 0
    for n, k in SMALL:
        out[n] = flat[off:off + DEPTH * k].reshape(DEPTH, k)
        off += DEPTH * k
    return out


def _pad_w_in_t(w_t):
    return jnp.concatenate([w_t[:2560], w_t[2576:IN_COLS], w_t[2560:2576],
                            jnp.zeros((IN_PAD - IN_COLS, w_t.shape[1]), w_t.dtype)], axis=0)


def _unpad_w_in_t(g_t):
    return jnp.concatenate([g_t[:2560], g_t[3648:3664], g_t[2560:3648]], axis=0)


def _pad_head_rows(w_t):
    k = w_t.shape[1]
    v = jnp.pad(w_t.reshape(MLA_HEADS, MLA_QK, k), ((0, 0), (0, MLA_QK_PAD - MLA_QK), (0, 0)))
    return v.reshape(MLA_HEADS * MLA_QK_PAD, k)


def _unpad_head_rows(g_t):
    k = g_t.shape[1]
    return g_t.reshape(MLA_HEADS, MLA_QK_PAD, k)[:, :MLA_QK].reshape(MLA_HEADS * MLA_QK, k)


def _row(v):
    return v.reshape(1, -1).astype(F32)


def layer_fwd(x, memf, w, p, tabs, bsz, next_blocks):
    t_rows = x.shape[0]
    seq = t_rows // bsz
    r = {}
    r["x"] = x
    h1 = rms_fwd(x, 0, D_MODEL, _row(p["attn_norm_g"]), 1, D_MODEL, BF16, "norm1_fwd")
    proj = mm([(h1, w["w_in"])], "nt", tn=768, name="in_proj")
    r["h1"], r["proj"] = h1, proj
    proj3 = proj.reshape(bsz, seq, IN_PAD)

    w8 = jnp.concatenate([p["conv_w_full"], p["conv_b"].reshape(1, -1), jnp.zeros((3, CONV_DIM), F32)], 0)
    xbc = conv_fwd(proj3, w8, "conv_fwd")
    tail = proj[:, TAIL_COL:TAIL_COL + 128]
    dt_raw = tail[:, 64:80].reshape(bsz, seq, 16)
    y, states = ssd_fwd(xbc, dt_raw, _row(p["dt_bias"]), _row(p["a_log"]), _row(p["d_skip"]), "ssd_fwd")
    y2 = y.reshape(t_rows, SSD_INNER)
    yg = ew(_gate_fwd, [(y2, 0), (proj, 0)], SSD_INNER, (F32,), "ssd_gate_fwd")
    y_ssd = rms_fwd(yg, 0, SSD_INNER, _row(p["ssd_norm_g"]), 2, 512, BF16, "ssd_norm_fwd")
    r.update(w8=w8, xbc=xbc, dt_raw=dt_raw, states=states, y=y2, yg=yg)

    qn = rms_fwd(proj, 5, Q_LORA, _row(p["q_a_norm_g"]), 1, Q_LORA, BF16, "qa_norm_fwd")
    kvn = rms_fwd(proj, 6, Q_LORA, _row(p["kv_a_norm_g"]), 1, Q_LORA, BF16, "kva_norm_fwd")
    q_raw = mm([(qn, w["w_q_b"])], "nt", name="q_b_proj")
    kv = mm([(kvn, w["w_kv_b"])], "nt", name="kv_b_proj")
    gq = _row(jnp.tile(jnp.pad(p["mla_q_norm_g"], (0, 64)), MLA_HEADS))
    gk = _row(jnp.tile(jnp.pad(p["mla_k_norm_g"], (0, 64)), MLA_HEADS))
    q_raw3, kv3 = q_raw.reshape(bsz, seq, 2048), kv.reshape(bsz, seq, 2048)
    q_fin, k_fin = qk_prep_fwd(q_raw3, kv3, proj3, gq, gk, tabs, "qk_prep_fwd")
    riding = next_blocks is not None
    o, lse, gathered = attn_fwd(q_fin, k_fin, kv3, heads=MLA_HEADS, dk=MLA_QK_PAD, dv=MLA_V, v_mul=2, v_off=1,
                                causal=True, scale=MLA_QK ** -0.5, tq=512, tk=512,
                                name="mla_attn_fwd_gather_near" if riding else "mla_attn_fwd",
                                rider=gather_near_rider(next_blocks) if riding else None)
    r.update(qn=qn, kvn=kvn, q_raw3=q_raw3, gq=gq, gk=gk, q_fin=q_fin, k_fin=k_fin, kv3=kv3, o=o, lse=lse)
    x2 = mm([(y_ssd, w["w_out"][:SSD_INNER]), (o.reshape(t_rows, SSD_INNER), w["w_out"][SSD_INNER:])], "nn",
            extras=(x,), epilogue=_add_res, name="out_proj")
    r.update(y_ssd=y_ssd, x2=x2)

    h2 = rms_fwd(x2, 0, D_MODEL, _row(p["xattn_norm_g"]), 1, D_MODEL, BF16, "norm2_fwd")
    mn = rms_fwd(memf, 0, D_MODEL, _row(p["mem_norm_g"]), 1, D_MODEL, BF16, "mem_norm_fwd")
    xq_raw = mm([(h2, w["w_xq"])], "nn", name="xq_proj")
    xk_raw = mm([(mn, w["w_xk"])], "nn", name="xk_proj")
    xv = mm([(mn, w["w_xv"])], "nn", name="xv_proj")
    gxq = _row(jnp.tile(p["xq_norm_g"], X_HEADS))
    gxk = _row(jnp.tile(p["xk_norm_g"], X_HEADS))
    xq = rms_fwd(xq_raw, 0, X_INNER, gxq, X_HEADS, X_HEAD_DIM, BF16, "xq_norm_fwd")
    xk = rms_fwd(xk_raw, 0, X_INNER, gxk, X_HEADS, X_HEAD_DIM, BF16, "xk_norm_fwd")
    mlen = memf.shape[0] // bsz
    xq3, xk3, xv3 = xq.reshape(bsz, seq, X_INNER), xk.reshape(bsz, mlen, X_INNER), xv.reshape(bsz, mlen, X_INNER)
    xo, xlse, _ = attn_fwd(xq3, xk3, xv3, heads=X_HEADS, dk=X_HEAD_DIM, dv=X_HEAD_DIM, v_mul=1, v_off=0,
                           causal=False, scale=X_HEAD_DIM ** -0.5, tq=512, tk=256, name="xattn_fwd")
    xo2 = xo.reshape(t_rows, X_INNER)
    x3 = mm([(xo2, w["w_xo"])], "nt", extras=(x2,), epilogue=_add_res, name="xo_proj")
    r.update(h2=h2, mn=mn, xq_raw=xq_raw, xk_raw=xk_raw, gxq=gxq, gxk=gxk, xq3=xq3, xk3=xk3, xv3=xv3,
             xo=xo, xlse=xlse, x3=x3)

    h3 = rms_fwd(x3, 0, D_MODEL, _row(p["ffn_norm_g"]), 1, D_MODEL, BF16, "norm3_fwd")
    if riding:
        (gate, up, act), gathered = mm(
            [(h3, w["w_gate"]), (h3, w["w_up"])], "nt", epilogue=_swiglu_fwd, separate=True,
            out_dtypes=(BF16, BF16, BF16), name="gate_up_proj_gather_diagonal",
            rider=gather_diagonal_rider(next_blocks, gathered))
        x4, gathered = mm([(act, w["w_down"])], "nn", extras=(x3,), epilogue=_add_res, tm=512,
                          name="down_proj_gather_forward", rider=gather_forward_rider(gathered))
    else:
        gate, up, act = mm([(h3, w["w_gate"]), (h3, w["w_up"])], "nt", epilogue=_swiglu_fwd, separate=True,
                           out_dtypes=(BF16, BF16, BF16), name="gate_up_proj")
        x4 = mm([(act, w["w_down"])], "nn", extras=(x3,), epilogue=_add_res, tm=512, name="down_proj")
        gathered = None
    r.update(h3=h3, gate=gate, up=up, act=act)
    return x4, r, gathered


def layer_bwd(dx4, r, memf, w, p, tabs, bsz, pending):
    t_rows = dx4.shape[0]
    seq = t_rows // bsz
    big, small = {}, {}
    rider_xy = rider_diag = None

    if pending is None:
        dgate, dup = mm([(dx4, w["w_down"])], "nt", extras=(r["gate"], r["up"]), epilogue=_swiglu_bwd,
                        out_dtypes=(BF16, BF16), name="down_proj_bwd")
    else:
        (dgate, dup), from_sib = mm([(dx4, w["w_down"])], "nt", extras=(r["gate"], r["up"]), epilogue=_swiglu_bwd,
                                    out_dtypes=(BF16, BF16), name="down_proj_bwd_sibling_exchange",
                                    rider=sibling_stage_rider(pending))
        partials = [chip_partial(b, s, "grad_chip_partial", tr) for b, s, tr in zip(pending, from_sib, PARTIAL_TR)]
        first_half, second_half = _halves(partials)
    if pending is None:
        big["w_down"] = mm([(r["act"], dx4)], "tn", out_dtypes=(BF16,), name="down_proj_wgrad")
    else:
        big["w_down"], recv_diag = mm([(r["act"], dx4)], "tn", out_dtypes=(BF16,),
                                      name="down_proj_wgrad_diagonal_exchange",
                                      rider=chip_stage_rider(partials, True, first_half))
    big["w_gate"] = mm([(dgate, r["h3"])], "tn", out_dtypes=(BF16,), tm=512, tn=1024, name="gate_proj_wgrad")
    big["w_up"] = mm([(dup, r["h3"])], "tn", out_dtypes=(BF16,), tm=512, tn=1024, name="gate_proj_wgrad")
    if pending is None:
        dh3 = mm([(dgate, w["w_gate"]), (dup, w["w_up"])], "nn", tm=512, tn=256, name="gate_up_proj_bwd")
    else:
        dh3, recv_xy = mm([(dgate, w["w_gate"]), (dup, w["w_up"])], "nn", tm=512, tn=256,
                          name="gate_up_proj_bwd_near_exchange",
                          rider=chip_stage_rider(partials, False, first_half))
        rider_xy = chip_stage_rider(partials, False, second_half, received=recv_xy)
        rider_diag = chip_stage_rider(partials, True, second_half, received=recv_diag)
    dx3, g = rms_bwd(r["x3"], 0, D_MODEL, _row(p["ffn_norm_g"]), 1, D_MODEL, dh3, dx4, "norm3_bwd")
    small["ffn_norm_g"] = g[0]

    xo2 = r["xo"].reshape(t_rows, X_INNER)
    dxo = mm([(dx3, w["w_xo"])], "nn", name="xo_proj_bwd")
    big["w_xo"] = mm([(dx3, xo2)], "tn", out_dtypes=(BF16,), tm=512, name="xo_proj_wgrad")
    dxq, dxk, dxv, _, _ = attn_bwd(r["xq3"], r["xk3"], r["xv3"], dxo.reshape(bsz, seq, X_INNER), 0, r["xo"],
                                   r["xlse"], heads=X_HEADS, dk=X_HEAD_DIM, dv=X_HEAD_DIM, v_mul=1, v_off=0,
                                   causal=False, scale=X_HEAD_DIM ** -0.5, tq=512, tk=256, name="xattn_bwd")
    mrows = memf.shape[0]
    dxq_raw, g = rms_bwd(r["xq_raw"], 0, X_INNER, r["gxq"], X_HEADS, X_HEAD_DIM, dxq.reshape(t_rows, X_INNER),
                         None, "xq_norm_bwd")
    small["xq_norm_g"] = g.reshape(X_HEADS, X_HEAD_DIM).sum(0)
    dxk_raw, g = rms_bwd(r["xk_raw"], 0, X_INNER, r["gxk"], X_HEADS, X_HEAD_DIM, dxk.reshape(mrows, X_INNER),
                         None, "xk_norm_bwd")
    small["xk_norm_g"] = g.reshape(X_HEADS, X_HEAD_DIM).sum(0)
    dxv2 = dxv.reshape(mrows, X_INNER)
    big["w_xq"] = mm([(r["h2"], dxq_raw)], "tn", out_dtypes=(BF16,), name="xq_proj_wgrad")
    big["w_xk"] = mm([(r["mn"], dxk_raw)], "tn", out_dtypes=(BF16,), name="xkv_proj_wgrad")
    big["w_xv"] = mm([(r["mn"], dxv2)], "tn", out_dtypes=(BF16,), name="xkv_proj_wgrad")
    dh2 = mm([(dxq_raw, w["w_xq"])], "nt", name="xq_proj_bwd")
    dmn = mm([(dxk_raw, w["w_xk"]), (dxv2, w["w_xv"])], "nt", name="xkv_proj_bwd")
    _, g = rms_bwd(memf, 0, D_MODEL, _row(p["mem_norm_g"]), 1, D_MODEL, dmn, None, "mem_norm_bwd")
    small["mem_norm_g"] = g[0]
    dx2, g = rms_bwd(r["x2"], 0, D_MODEL, _row(p["xattn_norm_g"]), 1, D_MODEL, dh2, dx3, "norm2_bwd")
    small["xattn_norm_g"] = g[0]

    dmixed = mm([(dx2, w["w_out"])], "nt", name="out_proj_bwd")
    big["w_out"] = jnp.concatenate(
        [mm([(r["y_ssd"], dx2)], "tn", out_dtypes=(BF16,), tm=512, name="out_proj_wgrad"),
         mm([(r["o"].reshape(t_rows, SSD_INNER), dx2)], "tn", out_dtypes=(BF16,), tm=512, name="out_proj_wgrad")],
        axis=0)
    dmixed3 = dmixed.reshape(bsz, seq, D_MODEL)

    dq_fin, dk_fin, dv, recv_xy, recv_diag = attn_bwd(
        r["q_fin"], r["k_fin"], r["kv3"], dmixed3, MLA_HEADS, r["o"], r["lse"], heads=MLA_HEADS, dk=MLA_QK_PAD,
        dv=MLA_V, v_mul=2, v_off=1, causal=True, scale=MLA_QK ** -0.5, tq=512, tk=512,
        name="mla_attn_bwd" if pending is None else "mla_attn_bwd_chip_exchange",
        rider_dq=rider_xy, rider_dkv=rider_diag)
    dq_raw, dkv, dkrope, g_q, g_k = qk_prep_bwd(r["q_raw3"], r["kv3"], r["proj"].reshape(bsz, seq, IN_PAD), r["gq"],
                                                r["gk"], tabs, dq_fin, dk_fin, dv, "qk_prep_bwd")
    dq_raw, dkv, dkrope = dq_raw.reshape(t_rows, 2048), dkv.reshape(t_rows, 2048), dkrope.reshape(t_rows, 128)
    small["mla_q_norm_g"] = g_q.reshape(MLA_HEADS, MLA_QK_PAD)[:, :MLA_QK].sum(0)
    small["mla_k_norm_g"] = g_k.reshape(MLA_HEADS, MLA_QK_PAD)[:, :MLA_QK].sum(0)
    big["w_q_b"] = _unpad_head_rows(mm([(dq_raw, r["qn"])], "tn", out_dtypes=(BF16,), tm=512,
                                       name="qkv_b_proj_wgrad"))
    dqn = mm([(dq_raw, w["w_q_b"])], "nn", name="qkv_b_proj_bwd")
    big["w_kv_b"] = mm([(dkv, r["kvn"])], "tn", out_dtypes=(BF16,), tm=512, name="qkv_b_proj_wgrad")
    dkvn = mm([(dkv, w["w_kv_b"])], "nn", name="qkv_b_proj_bwd")
    dq_a, g = rms_bwd(r["proj"], 5, Q_LORA, _row(p["q_a_norm_g"]), 1, Q_LORA, dqn, None, "qa_norm_bwd")
    small["q_a_norm_g"] = g[0]
    dkv_a, g = rms_bwd(r["proj"], 6, Q_LORA, _row(p["kv_a_norm_g"]), 1, Q_LORA, dkvn, None, "qa_norm_bwd")
    small["kv_a_norm_g"] = g[0]

    dyg, g = rms_bwd(r["yg"], 0, SSD_INNER, _row(p["ssd_norm_g"]), 2, 512, dmixed, None, "ssd_norm_bwd")
    small["ssd_norm_g"] = g[0]
    dy, dz = ew(_gate_bwd, [(dyg, 0), (r["y"], 0), (r["proj"], 0)], SSD_INNER, (F32, F32), "ssd_gate_bwd")
    dxbc_c, ddt_raw, g_dtb, g_alog, g_dsk = ssd_bwd(
        r["xbc"], r["dt_raw"], _row(p["dt_bias"]), _row(p["a_log"]), _row(p["d_skip"]), r["states"],
        dy.reshape(bsz, seq, SSD_INNER), "ssd_bwd")
    small["dt_bias"], small["a_log"], small["d_skip"] = g_dtb[0], g_alog[0], g_dsk[0]
    dxbc, dw8 = conv_bwd(r["proj"].reshape(bsz, seq, IN_PAD), r["w8"], dxbc_c, "conv_bwd")
    small["conv_w"] = dw8[:4]
    small["conv_b"] = dw8[4]

    dtail = jnp.concatenate([dkrope[:, :64], ddt_raw.reshape(t_rows, 16), jnp.zeros((t_rows, 48 + 128), F32)], axis=1)
    dproj = jnp.concatenate([dz.astype(BF16), dxbc.reshape(t_rows, CONV_DIM).astype(BF16), dq_a.astype(BF16),
                             dkv_a.astype(BF16), dtail.astype(BF16)], axis=1)
    big["w_in"] = _unpad_w_in_t(mm([(dproj, r["h1"])], "tn", out_dtypes=(BF16,), tm=768, tn=1024,
                                   name="in_proj_wgrad"))
    dh1 = mm([(dproj, w["w_in"])], "nn", name="in_proj_bwd")
    dx, g = rms_bwd(r["x"], 0, D_MODEL, _row(p["attn_norm_g"]), 1, D_MODEL, dh1, dx2, "norm1_bwd")
    small["attn_norm_g"] = g[0]
    if pending is None:
        recv_xy = recv_diag = None
    return dx, big, small, recv_xy, recv_diag


def _rope_tables(positions):
    inv_freq = 1.0 / (ROPE_THETA ** (jnp.arange(0, 64, 2, dtype=F32) / 64))
    ang = positions.astype(F32)[..., None] * inv_freq
    c, s = jnp.cos(ang), jnp.sin(ang)
    z32, z64 = jnp.zeros_like(c), jnp.zeros(c.shape[:-1] + (64,), F32)
    return (jnp.concatenate([c, c, z64], -1), jnp.concatenate([-s, z32, z64], -1),
            jnp.concatenate([z32, s, z64], -1))


def kernel(x, mem, positions, attn_norm_g, w_in, conv_w, conv_b, dt_bias, a_log, d_skip, ssd_norm_g, q_a_norm_g, w_q_b, kv_a_norm_g, w_kv_b, mla_q_norm_g, mla_k_norm_g, w_out, xattn_norm_g, mem_norm_g, w_xq, w_xk, w_xv, xq_norm_g, xk_norm_g, w_xo, ffn_norm_g, w_gate, w_up, w_down, loss_target, m_attn_norm_g, m_w_in, m_conv_w, m_conv_b, m_dt_bias, m_a_log, m_d_skip, m_ssd_norm_g, m_q_a_norm_g, m_w_q_b, m_kv_a_norm_g, m_w_kv_b, m_mla_q_norm_g, m_mla_k_norm_g, m_w_out, m_xattn_norm_g, m_mem_norm_g, m_w_xq, m_w_xk, m_w_xv, m_xq_norm_g, m_xk_norm_g, m_w_xo, m_ffn_norm_g, m_w_gate, m_w_up, m_w_down, v_attn_norm_g, v_w_in, v_conv_w, v_conv_b, v_dt_bias, v_a_log, v_d_skip, v_ssd_norm_g, v_q_a_norm_g, v_w_q_b, v_kv_a_norm_g, v_w_kv_b, v_mla_q_norm_g, v_mla_k_norm_g, v_w_out, v_xattn_norm_g, v_mem_norm_g, v_w_xq, v_w_xk, v_w_xv, v_xq_norm_g, v_xk_norm_g, v_w_xo, v_ffn_norm_g, v_w_gate, v_w_up, v_w_down):
    args = dict(locals())
    weights = {n: args[n] for n in WEIGHT_ORDER}
    mom_m = {n: args["m_" + n] for n in WEIGHT_ORDER}
    mom_v = {n: args["v_" + n] for n in WEIGHT_ORDER}
    bsz, seq, _ = x.shape
    t_rows = bsz * seq
    xf = x.reshape(t_rows, D_MODEL)
    memf = mem.reshape(-1, D_MODEL)
    tabs = _rope_tables(positions)

    full = []
    def local_blocks(l):
        return [_pack_local(fam, {n: weights[n][l] for n, _, _, _ in fam}, BF16) for fam, _ in FAMS]

    def full_weights(gathered):
        wl = {}
        for (fam, _), g in zip(FAMS, gathered):
            wl.update(_unpack_gathered(fam, g))
        wl["w_in"] = _pad_w_in_t(wl["w_in"])
        wl["w_q_b"] = _pad_head_rows(wl["w_q_b"])
        return wl

    full.append(full_weights(all_gather(local_blocks(0), "weight_all_gather")))
    small_p = [{n: weights[n][l] for n, _ in SMALL} for l in range(DEPTH)]
    cw = all_gather([conv_w.reshape(-1, 128)], "conv_w_all_gather")[0].reshape(N_DEV, DEPTH, 4, -1)
    cw = cw.transpose(1, 2, 0, 3).reshape(DEPTH, 4, CONV_DIM)
    for l in range(DEPTH):
        small_p[l]["conv_w_full"] = cw[l]

    saved = []
    h = xf
    for l in range(DEPTH):
        h, res, gathered = layer_fwd(h, memf, full[l], small_p[l], tabs, bsz,
                                     local_blocks(l + 1) if l + 1 < DEPTH else None)
        saved.append(res)
        if gathered is not None:
            full.append(full_weights(gathered))
    dy, loss_part = loss_head(h, loss_target.reshape(t_rows, D_MODEL), "loss_head")
    loss = lax.psum(loss_part[0, 0], ("x", "y", "c"))

    small_g = [None] * DEPTH
    outs = {k: {} for k in ("g", "d", "m", "v")}
    results = {}

    def update_layer(l, recv_xy, recv_diag):
        for a, (fam, _) in enumerate(FAMS):
            off = 0
            for n, rows, prows, transposed in fam:
                part = jnp.concatenate([recv_xy[a][:, off:off + rows], recv_diag[a][:, off:off + rows]], axis=0)
                off += prows
                if n in ADAMW_TRANSPOSED:
                    state = [src[n].transpose(0, 2, 1) for src in (weights, mom_m, mom_v)]
                else:
                    state = [weights[n], mom_m[n], mom_v[n]]
                    if transposed:
                        part = part.transpose(0, 2, 1)
                results[n] = adamw([part], *state, l, results.get(n), "adamw_" + n, tr=ADAMW_TR[n])

    dh, pending = dy, None
    for l in reversed(range(DEPTH)):
        dh, big_g, small_g[l], recv_xy, recv_diag = layer_bwd(dh, saved[l], memf, full[l], small_p[l], tabs, bsz,
                                                             pending)
        if pending is not None:
            update_layer(l + 1, recv_xy, recv_diag)
        pending = [_pack_full_grads(fam, big_g) for fam, _ in FAMS]
    from_sib = comm_call([sibling_stage_rider(pending)], "grad_sibling_exchange")[0]
    partials = [chip_partial(b, s, "grad_chip_partial", tr) for b, s, tr in zip(pending, from_sib, PARTIAL_TR)]
    whole = [(0, part.shape[1]) for part in partials]
    recv_xy, recv_diag = comm_call([chip_stage_rider(partials, False, whole), chip_stage_rider(partials, True, whole)],
                                   "grad_chip_exchange")
    update_layer(0, recv_xy, recv_diag)
    for ki, key in enumerate(("g", "d", "m", "v")):
        for n, res in results.items():
            outs[key][n] = res[ki].transpose(0, 2, 1) if n in ADAMW_TRANSPOSED else res[ki]

    sg = _pack_small({n: jnp.stack([small_g[l][n] for l in range(DEPTH)]) for n, _ in SMALL})
    cg = jnp.stack([small_g[l]["conv_w"] for l in range(DEPTH)]).reshape(CONVW_ROWS, 128)
    parts = all_gather([jnp.concatenate([sg, cg], axis=0)], "small_grad_all_gather")[0]
    sm = adamw([parts[:, :SMALL_ROWS]], _pack_small({n: weights[n] for n, _ in SMALL})[None],
               _pack_small({n: mom_m[n] for n, _ in SMALL})[None], _pack_small({n: mom_v[n] for n, _ in SMALL})[None],
               0, None, "adamw_small", tr=SMALL_ROWS)
    for ki, key in enumerate(("g", "d", "m", "v")):
        outs[key].update(_unpack_small(sm[ki][0]))
    my_idx = 4 * lax.axis_index("x") + 2 * lax.axis_index("y") + lax.axis_index("c")
    cparts = parts[:, SMALL_ROWS:].reshape(N_DEV, DEPTH, 4, N_DEV, -1)
    cparts = lax.dynamic_index_in_dim(cparts, my_idx, axis=3, keepdims=False).reshape(N_DEV, -1, 128)
    cres = adamw([cparts], conv_w.reshape(1, -1, 128), m_conv_w.reshape(1, -1, 128), v_conv_w.reshape(1, -1, 128),
                 0, None, "adamw_conv_w")
    for ki, key in enumerate(("g", "d", "m", "v")):
        outs[key]["conv_w"] = cres[ki].reshape(conv_w.shape)

    grad_x = dh.reshape(bsz, seq, D_MODEL)
    return (loss, grad_x, *[outs["g"][n] for n in WEIGHT_ORDER], *[outs["d"][n] for n in WEIGHT_ORDER],
            *[outs["m"][n] for n in WEIGHT_ORDER], *[outs["v"][n] for n in WEIGHT_ORDER])
```

```python
import functools

import jax
import jax.numpy as jnp
from jax import lax
from jax.experimental import pallas as pl
from jax.experimental.pallas import tpu as pltpu

F32 = jnp.float32
BF16 = jnp.bfloat16

D_MODEL = 2048
DEPTH = 4
N_DEV = 8
SSD_INNER = 1024
SSD_HEADS = 16
SSD_PAIRS = 8
SSD_STATE = 128
SSD_CHUNK = 128
CONV_DIM = 1536
MLA_HEADS = 8
MLA_QK = 192
MLA_QK_PAD = 256
MLA_V = 128
Q_LORA = 512
X_HEADS = 4
X_HEAD_DIM = 128
X_INNER = 512
FFN = 5632
IN_COLS = 3664
IN_PAD = 3840
TAIL_COL = 3584
ROPE_THETA = 10000.0
EPS = 1e-6
NEG = -1e30
VMEM_LIMIT = 56 * 1024 * 1024
HI = lax.Precision.HIGHEST

ADAM_LR = 0.001
ADAM_B1 = 0.9
ADAM_B2 = 0.999
ADAM_EPS = 1e-08
ADAM_WD = 0.01
ADAM_STEP = 10

SMALL_ROWS = 392
CONVW_ROWS = 192

FAM_A = (("w_in", 458, 464, True), ("w_out", 256, 256, False))
FAM_G = (("w_gate", 704, 704, True),)
FAM_U = (("w_up", 704, 704, True),)
FAM_D = (("w_down", 704, 704, False),)
FAM_B = (("w_q_b", 192, 192, True), ("w_kv_b", 256, 256, True), ("w_xq", 256, 256, False),
         ("w_xk", 256, 256, False), ("w_xv", 256, 256, False), ("w_xo", 256, 256, True))
FAMS = ((FAM_A, 2048), (FAM_G, 2048), (FAM_U, 2048), (FAM_D, 2048), (FAM_B, 512))
PARTIAL_TR = (720, 704, 704, 704, 1472)
ADAMW_TRANSPOSED = ("w_gate", "w_up", "w_q_b")
ADAMW_TR = {"w_in": 512, "w_gate": 176, "w_up": 176, "w_down": 176, "w_out": 128, "w_q_b": 192, "w_kv_b": 512,
            "w_xq": 256, "w_xk": 256, "w_xv": 256, "w_xo": 512}
SMALL = (
    ("attn_norm_g", 2048), ("conv_b", 1536), ("dt_bias", 16), ("a_log", 16), ("d_skip", 16),
    ("ssd_norm_g", 1024), ("q_a_norm_g", 512), ("kv_a_norm_g", 512), ("mla_q_norm_g", 192),
    ("mla_k_norm_g", 192), ("xattn_norm_g", 2048), ("mem_norm_g", 2048), ("xq_norm_g", 128),
    ("xk_norm_g", 128), ("ffn_norm_g", 2048),
)
WEIGHT_ORDER = ("attn_norm_g", "w_in", "conv_w", "conv_b", "dt_bias", "a_log", "d_skip", "ssd_norm_g",
                "q_a_norm_g", "w_q_b", "kv_a_norm_g", "w_kv_b", "mla_q_norm_g", "mla_k_norm_g", "w_out",
                "xattn_norm_g", "mem_norm_g", "w_xq", "w_xk", "w_xv", "xq_norm_g", "xk_norm_g", "w_xo",
                "ffn_norm_g", "w_gate", "w_up", "w_down")


def _cp(sem=None):
    if sem is None:
        return pltpu.CompilerParams(vmem_limit_bytes=VMEM_LIMIT)
    return pltpu.CompilerParams(vmem_limit_bytes=VMEM_LIMIT, dimension_semantics=sem)


def _tile(n, pref):
    if n <= pref:
        return n
    t = (pref // 128) * 128
    while t > 128 and n % t:
        t -= 128
    return t


def _sigmoid(v):
    return 1.0 / (1.0 + jnp.exp(-v))


class Rider:
    def __init__(self, ins, out_shapes, sems, start, finish, aliases=None):
        self.ins, self.out_shapes, self.sems = list(ins), list(out_shapes), list(sems)
        self.start, self.finish = start, finish
        self.aliases = dict(aliases or {})


def _attach(rider, body, grid, in_specs, out_specs, out_shape, scratch, args):
    n_in, n_out, n_scr = len(in_specs), len(out_specs), len(scratch)
    r_in, r_out = len(rider.ins), len(rider.out_shapes)

    def wrapped(*refs):
        ins, rins = refs[:n_in], refs[n_in:n_in + r_in]
        o0 = n_in + r_in
        outs, routs = refs[o0:o0 + n_out], refs[o0 + n_out:o0 + n_out + r_out]
        s0 = o0 + n_out + r_out
        scr, rsems = refs[s0:s0 + n_scr], refs[s0 + n_scr:]
        first = last = None
        for d, g in enumerate(grid):
            i = pl.program_id(d)
            first = (i == 0) if first is None else jnp.logical_and(first, i == 0)
            last = (i == g - 1) if last is None else jnp.logical_and(last, i == g - 1)
        if first is None:
            rider.start(rins, routs, rsems)
            body(*ins, *outs, *scr)
            rider.finish(rins, routs, rsems)
            return
        pl.when(first)(lambda: rider.start(rins, routs, rsems))
        body(*ins, *outs, *scr)
        pl.when(last)(lambda: rider.finish(rins, routs, rsems))

    hbm = pl.BlockSpec(memory_space=pl.ANY)
    io_aliases = {n_in + ri: n_out + ro for ri, ro in rider.aliases.items()}
    return (wrapped, list(in_specs) + [hbm] * r_in, list(out_specs) + [hbm] * r_out,
            list(out_shape) + rider.out_shapes, list(scratch) + rider.sems, list(args) + rider.ins, io_aliases)


def join_riders(riders):
    def start(rins, routs, sems):
        for r, (i0, i1, o0, o1, s0, s1) in zip(riders, spans):
            r.start(rins[i0:i1], routs[o0:o1], sems[s0:s1])

    def finish(rins, routs, sems):
        for r, (i0, i1, o0, o1, s0, s1) in zip(riders, spans):
            r.finish(rins[i0:i1], routs[o0:o1], sems[s0:s1])

    spans, i, o, s = [], 0, 0, 0
    for r in riders:
        spans.append((i, i + len(r.ins), o, o + len(r.out_shapes), s, s + len(r.sems)))
        i, o, s = i + len(r.ins), o + len(r.out_shapes), s + len(r.sems)
    aliases = {i0 + ri: o0 + ro for r, (i0, _, o0, _, _, _) in zip(riders, spans) for ri, ro in r.aliases.items()}
    joined = Rider([a for r in riders for a in r.ins], [a for r in riders for a in r.out_shapes],
                   [a for r in riders for a in r.sems], start, finish, aliases)
    return joined, lambda outs: [outs[a:b] for (_, _, a, b, _, _) in spans]


def comm_call(riders, name):
    joined, split = join_riders(riders)
    body, in_specs, out_specs, out_shape, scratch, args, io_aliases = _attach(
        joined, lambda: None, (), [], [], [], [], [])
    outs = pl.pallas_call(body, name=name, in_specs=in_specs, out_specs=out_specs, out_shape=out_shape,
                          scratch_shapes=scratch, input_output_aliases=io_aliases)(*args)
    return split(outs)


def rms_fwd(x, col_blk, width, g, groups, n_valid, out_dtype, name, tr=512):
    rows = x.shape[0]
    tr = min(tr, rows)
    wg = width // groups

    def body(x_ref, g_ref, y_ref):
        for gi in range(groups):
            sl = slice(gi * wg, (gi + 1) * wg)
            xv = x_ref[:, sl].astype(F32)
            r = lax.rsqrt(jnp.sum(xv * xv, axis=-1, keepdims=True) * (1.0 / n_valid) + EPS)
            y_ref[:, sl] = ((xv * r) * g_ref[:, sl]).astype(out_dtype)

    return pl.pallas_call(
        body, name=name, grid=(rows // tr,),
        in_specs=[pl.BlockSpec((tr, width), lambda i: (i, col_blk)), pl.BlockSpec((1, width), lambda i: (0, 0))],
        out_specs=pl.BlockSpec((tr, width), lambda i: (i, 0)),
        out_shape=jax.ShapeDtypeStruct((rows, width), out_dtype), compiler_params=_cp(("parallel",)))(x, g)


def rms_bwd(x, col_blk, width, g, groups, n_valid, dy, add, name, tr=512):
    rows = x.shape[0]
    tr = min(tr, rows)
    wg = width // groups
    has_add = add is not None

    def body(*refs):
        if has_add:
            x_ref, g_ref, dy_ref, add_ref, dx_ref, dg_ref = refs
        else:
            x_ref, g_ref, dy_ref, dx_ref, dg_ref = refs

        @pl.when(pl.program_id(0) == 0)
        def _():
            dg_ref[...] = jnp.zeros(dg_ref.shape, F32)

        for gi in range(groups):
            sl = slice(gi * wg, (gi + 1) * wg)
            xv = x_ref[:, sl].astype(F32)
            dyv = dy_ref[:, sl].astype(F32)
            r = lax.rsqrt(jnp.sum(xv * xv, axis=-1, keepdims=True) * (1.0 / n_valid) + EPS)
            xh = xv * r
            dyg = dyv * g_ref[:, sl]
            c = jnp.sum(dyg * xh, axis=-1, keepdims=True) * (1.0 / n_valid)
            dx = r * (dyg - xh * c)
            if has_add:
                dx = dx + add_ref[:, sl]
            dx_ref[:, sl] = dx
            dg_ref[:, sl] += jnp.sum(dyv * xh, axis=0, keepdims=True)

    in_specs = [pl.BlockSpec((tr, width), lambda i: (i, col_blk)), pl.BlockSpec((1, width), lambda i: (0, 0)),
                pl.BlockSpec((tr, width), lambda i: (i, 0))]
    args = [x, g, dy]
    if has_add:
        in_specs.append(pl.BlockSpec((tr, width), lambda i: (i, 0)))
        args.append(add)
    return pl.pallas_call(
        body, name=name, grid=(rows // tr,), in_specs=in_specs,
        out_specs=[pl.BlockSpec((tr, width), lambda i: (i, 0)), pl.BlockSpec((1, width), lambda i: (0, 0))],
        out_shape=[jax.ShapeDtypeStruct((rows, width), F32), jax.ShapeDtypeStruct((1, width), F32)],
        compiler_params=_cp(("arbitrary",)))(*args)


def mm(pairs, mode, *, extras=(), epilogue=None, out_dtypes=(F32,), separate=False, tm=1024, tn=512, name,
       rider=None):
    a0, b0 = pairs[0]
    if mode == "nn":
        m_dim, n_dim = a0.shape[0], b0.shape[1]
        dn = (((1,), (0,)), ((), ()))
    elif mode == "nt":
        m_dim, n_dim = a0.shape[0], b0.shape[0]
        dn = (((1,), (1,)), ((), ()))
    else:
        m_dim, n_dim = a0.shape[1], b0.shape[1]
        dn = (((0,), (0,)), ((), ()))
    tm = _tile(m_dim, tm)
    tn = _tile(n_dim, tn)
    n_pairs = len(pairs)
    n_extra = len(extras)

    def body(*refs):
        ins = refs[:2 * n_pairs]
        ex = refs[2 * n_pairs:2 * n_pairs + n_extra]
        outs = refs[2 * n_pairs + n_extra:]
        accs = []
        for k in range(n_pairs):
            a = ins[2 * k][...].astype(BF16)
            b = ins[2 * k + 1][...].astype(BF16)
            accs.append(lax.dot_general(a, b, dn, preferred_element_type=F32))
        if not separate:
            total = accs[0]
            for extra_acc in accs[1:]:
                total = total + extra_acc
            accs = [total]
        res = epilogue(*accs, *[e[...] for e in ex]) if epilogue is not None else tuple(accs)
        for o_ref, r in zip(outs, res):
            o_ref[...] = r.astype(o_ref.dtype)

    in_specs, args = [], []
    for a, b in pairs:
        if mode == "nn":
            in_specs += [pl.BlockSpec((tm, a.shape[1]), lambda i, j: (i, 0)),
                         pl.BlockSpec((b.shape[0], tn), lambda i, j: (0, j))]
        elif mode == "nt":
            in_specs += [pl.BlockSpec((tm, a.shape[1]), lambda i, j: (i, 0)),
                         pl.BlockSpec((tn, b.shape[1]), lambda i, j: (j, 0))]
        else:
            in_specs += [pl.BlockSpec((a.shape[0], tm), lambda i, j: (0, i)),
                         pl.BlockSpec((b.shape[0], tn), lambda i, j: (0, j))]
        args += [a, b]
    for e in extras:
        in_specs.append(pl.BlockSpec((tm, tn), lambda i, j: (i, j)))
        args.append(e)
    grid = (m_dim // tm, n_dim // tn)
    out_specs = [pl.BlockSpec((tm, tn), lambda i, j: (i, j)) for _ in out_dtypes]
    out_shape = [jax.ShapeDtypeStruct((m_dim, n_dim), dt) for dt in out_dtypes]
    scratch, sem, io_aliases = [], ("parallel", "parallel"), {}
    if rider is not None:
        body, in_specs, out_specs, out_shape, scratch, args, io_aliases = _attach(
            rider, body, grid, in_specs, out_specs, out_shape, scratch, args)
        sem = ("arbitrary", "arbitrary")
    outs = pl.pallas_call(body, name=name, grid=grid, in_specs=in_specs, out_specs=out_specs, out_shape=out_shape,
                          scratch_shapes=scratch, input_output_aliases=io_aliases, compiler_params=_cp(sem))(*args)
    main = outs[0] if len(out_dtypes) == 1 else outs[:len(out_dtypes)]
    return main if rider is None else (main, outs[len(out_dtypes):])


def _add_res(acc, res):
    return (acc + res,)


def _swiglu_fwd(gate, up):
    gb = gate.astype(BF16).astype(F32)
    ub = up.astype(BF16).astype(F32)
    return gate, up, gb * _sigmoid(gb) * ub


def _swiglu_bwd(dact, gate, up):
    gv = gate.astype(F32)
    uv = up.astype(F32)
    sg = _sigmoid(gv)
    return dact * uv * (sg * (1.0 + gv * (1.0 - sg))), dact * (gv * sg)


def ew(fn, ins, width, out_dtypes, name, tr=512):
    rows = ins[0][0].shape[0]
    tr = min(tr, rows)
    n_in = len(ins)

    def body(*refs):
        res = fn(*[r[...].astype(F32) for r in refs[:n_in]])
        for o_ref, r in zip(refs[n_in:], res):
            o_ref[...] = r.astype(o_ref.dtype)

    in_specs = [pl.BlockSpec((tr, width), functools.partial(lambda i, cb: (i, cb), cb=cb)) for _, cb in ins]
    outs = pl.pallas_call(
        body, name=name, grid=(rows // tr,), in_specs=in_specs,
        out_specs=[pl.BlockSpec((tr, width), lambda i: (i, 0)) for _ in out_dtypes],
        out_shape=[jax.ShapeDtypeStruct((rows, width), dt) for dt in out_dtypes],
        compiler_params=_cp(("parallel",)))(*[a for a, _ in ins])
    return outs[0] if len(out_dtypes) == 1 else outs


def _gate_fwd(y, z):
    return (y * (z * _sigmoid(z)),)


def _gate_bwd(dyg, y, z):
    sg = _sigmoid(z)
    return dyg * (z * sg), dyg * y * (sg * (1.0 + z * (1.0 - sg)))


CONV_CB = 512
CONV_OFF = 2


def _conv_pre(x, w_ref, row):
    acc = x * w_ref[pl.ds(3, 1), :] + w_ref[pl.ds(4, 1), :]
    shifted = []
    for j in (1, 2, 3):
        xs = jnp.where(row >= j, pltpu.roll(x, j, 0), 0.0)
        shifted.append(xs)
        acc = acc + xs * w_ref[pl.ds(3 - j, 1), :]
    return acc, shifted


def conv_fwd(proj3, w8, name):
    bsz, seq, _ = proj3.shape

    def body(x_ref, w_ref, o_ref):
        x = x_ref[0]
        row = lax.broadcasted_iota(jnp.int32, x.shape, 0)
        pre, _ = _conv_pre(x, w_ref, row)
        o_ref[0] = pre * _sigmoid(pre)

    return pl.pallas_call(
        body, name=name, grid=(bsz, CONV_DIM // CONV_CB),
        in_specs=[pl.BlockSpec((1, seq, CONV_CB), lambda b, c: (b, 0, c + CONV_OFF)),
                  pl.BlockSpec((8, CONV_CB), lambda b, c: (0, c))],
        out_specs=pl.BlockSpec((1, seq, CONV_CB), lambda b, c: (b, 0, c)),
        out_shape=jax.ShapeDtypeStruct((bsz, seq, CONV_DIM), F32),
        compiler_params=_cp(("parallel", "parallel")))(proj3, w8)


def conv_bwd(proj3, w8, dout, name):
    bsz, seq, _ = proj3.shape

    def body(x_ref, w_ref, do_ref, dx_ref, dw_ref):
        @pl.when(pl.program_id(1) == 0)
        def _():
            dw_ref[...] = jnp.zeros(dw_ref.shape, F32)

        x = x_ref[0]
        row = lax.broadcasted_iota(jnp.int32, x.shape, 0)
        pre, shifted = _conv_pre(x, w_ref, row)
        sg = _sigmoid(pre)
        dpre = do_ref[0] * (sg * (1.0 + pre * (1.0 - sg)))
        dx = dpre * w_ref[pl.ds(3, 1), :]
        for j in (1, 2, 3):
            fut = jnp.where(row < seq - j, pltpu.roll(dpre, seq - j, 0), 0.0)
            dx = dx + fut * w_ref[pl.ds(3 - j, 1), :]
            dw_ref[pl.ds(3 - j, 1), :] += jnp.sum(dpre * shifted[j - 1], axis=0, keepdims=True)
        dw_ref[pl.ds(3, 1), :] += jnp.sum(dpre * x, axis=0, keepdims=True)
        dw_ref[pl.ds(4, 1), :] += jnp.sum(dpre, axis=0, keepdims=True)
        dx_ref[0] = dx

    return pl.pallas_call(
        body, name=name, grid=(CONV_DIM // CONV_CB, bsz),
        in_specs=[pl.BlockSpec((1, seq, CONV_CB), lambda c, b: (b, 0, c + CONV_OFF)),
                  pl.BlockSpec((8, CONV_CB), lambda c, b: (0, c)),
                  pl.BlockSpec((1, seq, CONV_CB), lambda c, b: (b, 0, c))],
        out_specs=[pl.BlockSpec((1, seq, CONV_CB), lambda c, b: (b, 0, c)),
                   pl.BlockSpec((8, CONV_CB), lambda c, b: (0, c))],
        out_shape=[jax.ShapeDtypeStruct((bsz, seq, CONV_DIM), F32), jax.ShapeDtypeStruct((8, CONV_DIM), F32)],
        compiler_params=_cp(("parallel", "arbitrary")))(proj3, w8, dout)


def _softplus(v):
    t = jnp.exp(-jnp.abs(v))
    small = t * (1.0 - t * (0.5 - t * (1.0 / 3.0)))
    return jnp.maximum(v, 0.0) + jnp.where(t < 0.01, small, jnp.log(1.0 + t))


def _ssd_chunk_prelude(dt_ref, dtb_ref, alog_ref):
    L = SSD_CHUNK
    raw = dt_ref[0] + dtb_ref[...]
    dt = _softplus(raw)
    a_neg = -jnp.exp(alog_ref[...])
    a = dt * a_neg
    r_i = lax.broadcasted_iota(jnp.int32, (L, L), 0)
    c_i = lax.broadcasted_iota(jnp.int32, (L, L), 1)
    tri_low = (r_i >= c_i).astype(F32)
    tri_up = (r_i <= c_i).astype(F32)
    acs = jnp.dot(tri_low, a, precision=HI, preferred_element_type=F32)
    acs_t = lax.dot_general(a, tri_up, (((0,), (0,)), ((), ())), precision=HI,
                            preferred_element_type=F32)
    tot = jnp.sum(a, axis=0, keepdims=True)
    return raw, dt, a_neg, acs, acs_t, tot, r_i, c_i, tri_up


def _col(arr16, e):
    lane = lax.broadcasted_iota(jnp.int32, arr16.shape, 1)
    return jnp.sum(jnp.where(lane == e, arr16, 0.0), axis=1, keepdims=True)


def _rowvec(arr_t, e):
    sub = lax.broadcasted_iota(jnp.int32, arr_t.shape, 0)
    return jnp.sum(jnp.where(sub == e, arr_t, 0.0), axis=0, keepdims=True)


def _pair(lo_mask, v0, v1):
    return jnp.where(lo_mask, v0, v1)


def ssd_fwd(xbc, dt_raw, dtb, alog, dsk, name):
    bsz, seq, _ = xbc.shape
    L = SSD_CHUNK
    nc = seq // L

    def body(x_ref, b_ref, c_ref, dt_ref, dtb_ref, alog_ref, dsk_ref, y_ref, st_ref, h_sc):
        @pl.when(pl.program_id(1) == 0)
        def _():
            h_sc[...] = jnp.zeros(h_sc.shape, F32)

        _, dt, _, acs, acs_t, tot, r_i, c_i, _ = _ssd_chunk_prelude(dt_ref, dtb_ref, alog_ref)
        causal = r_i >= c_i
        lo = lax.broadcasted_iota(jnp.int32, (1, 128), 1) < 64
        dskv = dsk_ref[...]
        for grp in range(2):
            bg = b_ref[0, :, grp * 128:(grp + 1) * 128].astype(BF16)
            cg = c_ref[0, :, grp * 128:(grp + 1) * 128].astype(BF16)
            cb = lax.dot_general(cg, bg, (((1,), (1,)), ((), ())), preferred_element_type=F32)
            for kk in range(4):
                k = grp * 4 + kk
                e0, e1 = 2 * k, 2 * k + 1
                sl = slice(k * 128, (k + 1) * 128)
                xp = x_ref[0, :, sl]
                cols = [_col(acs, e0), _col(acs, e1)]
                dtp = _pair(lo, _col(dt, e0), _col(dt, e1))
                xdt = xp * dtp
                ydiag = jnp.zeros((L, 128), F32)
                for sub, e in enumerate((e0, e1)):
                    diff = cols[sub] - _rowvec(acs_t, e)
                    lam = jnp.exp(jnp.where(causal, diff, NEG))
                    m_e = (cb * lam).astype(BF16)
                    xm = jnp.where(lo if sub == 0 else jnp.logical_not(lo), xdt, 0.0).astype(BF16)
                    ydiag = ydiag + jnp.dot(m_e, xm, preferred_element_type=F32)
                h = h_sc[k]
                st_ref[0, 0, k] = h
                wp = _pair(lo, jnp.exp(cols[0]), jnp.exp(cols[1]))
                yoff = jnp.dot(cg, h.astype(BF16), preferred_element_type=F32) * wp
                tot0, tot1 = _col(tot, e0), _col(tot, e1)
                up = _pair(lo, jnp.exp(tot0 - cols[0]), jnp.exp(tot1 - cols[1]))
                etot = _pair(lo, jnp.exp(tot0), jnp.exp(tot1))
                h_sc[k] = h * etot + lax.dot_general(bg, (xdt * up).astype(BF16), (((0,), (0,)), ((), ())),
                                                     preferred_element_type=F32)
                dskp = _pair(lo, _col(dskv, e0), _col(dskv, e1))
                y_ref[0, :, sl] = ydiag + yoff + xp * dskp

    p16 = pl.BlockSpec((1, 16), lambda b, c: (0, 0))
    return pl.pallas_call(
        body, name=name, grid=(bsz, nc),
        in_specs=[pl.BlockSpec((1, L, 1024), lambda b, c: (b, c, 0)),
                  pl.BlockSpec((1, L, 256), lambda b, c: (b, c, 4)),
                  pl.BlockSpec((1, L, 256), lambda b, c: (b, c, 5)),
                  pl.BlockSpec((1, L, 16), lambda b, c: (b, c, 0)), p16, p16, p16],
        out_specs=[pl.BlockSpec((1, L, 1024), lambda b, c: (b, c, 0)),
                   pl.BlockSpec((1, 1, SSD_PAIRS, 128, 128), lambda b, c: (b, c, 0, 0, 0))],
        out_shape=[jax.ShapeDtypeStruct((bsz, seq, SSD_INNER), F32),
                   jax.ShapeDtypeStruct((bsz, nc, SSD_PAIRS, 128, 128), F32)],
        scratch_shapes=[pltpu.VMEM((SSD_PAIRS, 128, 128), F32)],
        compiler_params=_cp(("parallel", "arbitrary")))(xbc, xbc, xbc, dt_raw, dtb, alog, dsk)


def ssd_bwd(xbc, dt_raw, dtb, alog, dsk, states, dy, name):
    bsz, seq, _ = xbc.shape
    L = SSD_CHUNK
    nc = seq // L

    def body(x_ref, b_ref, c_ref, dt_ref, dtb_ref, alog_ref, dsk_ref, st_ref, dy_ref,
             dxbc_ref, ddt_ref, gdtb_ref, galog_ref, gdsk_ref, dh_sc):
        first = jnp.logical_and(pl.program_id(0) == 0, pl.program_id(1) == 0)

        @pl.when(first)
        def _():
            gdtb_ref[...] = jnp.zeros(gdtb_ref.shape, F32)
            galog_ref[...] = jnp.zeros(galog_ref.shape, F32)
            gdsk_ref[...] = jnp.zeros(gdsk_ref.shape, F32)

        @pl.when(pl.program_id(1) == 0)
        def _():
            dh_sc[...] = jnp.zeros(dh_sc.shape, F32)

        raw, dt, a_neg, acs, acs_t, tot, r_i, c_i, tri_up = _ssd_chunk_prelude(dt_ref, dtb_ref, alog_ref)
        causal = r_i >= c_i
        causal_t = c_i >= r_i
        lo = lax.broadcasted_iota(jnp.int32, (1, 128), 1) < 64
        hi = jnp.logical_not(lo)
        lane16 = lax.broadcasted_iota(jnp.int32, (L, 16), 1)
        lane16r = lax.broadcasted_iota(jnp.int32, (1, 16), 1)
        last_row = lax.broadcasted_iota(jnp.int32, (L, 1), 0) == L - 1
        dskv = dsk_ref[...]
        ds16 = jnp.zeros((L, 16), F32)
        ddt16 = jnp.zeros((L, 16), F32)
        gdsk = jnp.zeros((1, 16), F32)

        def hsum(t, mask):
            return jnp.sum(jnp.where(mask, t, 0.0), axis=1, keepdims=True)

        for grp in range(2):
            gs = slice(grp * 128, (grp + 1) * 128)
            bg = b_ref[0, :, gs].astype(BF16)
            cg = c_ref[0, :, gs].astype(BF16)
            cb = lax.dot_general(cg, bg, (((1,), (1,)), ((), ())), preferred_element_type=F32)
            cbt = lax.dot_general(bg, cg, (((1,), (1,)), ((), ())), preferred_element_type=F32)
            dcb = jnp.zeros((L, L), F32)
            dcbt = jnp.zeros((L, L), F32)
            dc_g = jnp.zeros((L, 128), F32)
            db_g = jnp.zeros((L, 128), F32)
            for kk in range(4):
                k = grp * 4 + kk
                e0, e1 = 2 * k, 2 * k + 1
                sl = slice(k * 128, (k + 1) * 128)
                xp = x_ref[0, :, sl]
                dyp = dy_ref[0, :, sl]
                h = st_ref[0, 0, k]
                dh = dh_sc[k]
                cols = [_col(acs, e0), _col(acs, e1)]
                tots = [_col(tot, e0), _col(tot, e1)]
                dtp = _pair(lo, _col(dt, e0), _col(dt, e1))
                xdt = xp * dtp
                wp = _pair(lo, jnp.exp(cols[0]), jnp.exp(cols[1]))
                up = _pair(lo, jnp.exp(tots[0] - cols[0]), jnp.exp(tots[1] - cols[1]))
                etot = _pair(lo, jnp.exp(tots[0]), jnp.exp(tots[1]))
                hb = h.astype(BF16)
                dhb = dh.astype(BF16)
                yoff = jnp.dot(cg, hb, preferred_element_type=F32) * wp
                dyw = (dyp * wp).astype(BF16)
                dc_g = dc_g + lax.dot_general(dyw, hb, (((1,), (1,)), ((), ())), preferred_element_type=F32)
                dh_in = lax.dot_general(cg, dyw, (((0,), (0,)), ((), ())), preferred_element_type=F32) + dh * etot
                q_mat = jnp.dot(bg, dhb, preferred_element_type=F32)
                db_g = db_g + lax.dot_general((xdt * up).astype(BF16), dhb, (((1,), (1,)), ((), ())),
                                              preferred_element_type=F32)
                dxdt = up * q_mat
                t_u = q_mat * xdt * up
                t_w = dyp * yoff
                t_h = jnp.sum(dh * h, axis=0, keepdims=True) * etot
                for sub, e in enumerate((e0, e1)):
                    msk = lo if sub == 0 else hi
                    rowv = _rowvec(acs_t, e)
                    lam = jnp.exp(jnp.where(causal, cols[sub] - rowv, NEG))
                    lam_t = jnp.exp(jnp.where(causal_t, rowv - cols[sub], NEG))
                    xm = jnp.where(msk, xdt, 0.0).astype(BF16)
                    dym = jnp.where(msk, dyp, 0.0).astype(BF16)
                    g_mat = lax.dot_general(dym, xm, (((1,), (1,)), ((), ())), preferred_element_type=F32)
                    g_t = lax.dot_general(xm, dym, (((1,), (1,)), ((), ())), preferred_element_type=F32)
                    gl = g_mat * lam
                    glt = g_t * lam_t
                    dcb = dcb + gl
                    dcbt = dcbt + glt
                    e_r = jnp.sum(gl * cb, axis=1, keepdims=True)
                    e_c = jnp.sum(glt * cbt, axis=1, keepdims=True)
                    dxdt = dxdt + jnp.dot((cbt * lam_t).astype(BF16), dym, preferred_element_type=F32)
                    u_rows = hsum(t_u, msk)
                    ds_tot = jnp.sum(u_rows, axis=0, keepdims=True) + hsum(t_h, msk)
                    ds_col = e_r - e_c + hsum(t_w, msk) - u_rows + jnp.where(last_row, ds_tot, 0.0)
                    ds16 = ds16 + jnp.where(lane16 == e, ds_col, 0.0)
                dxp = dxdt * xp
                dyx = dyp * xp
                for sub, e in enumerate((e0, e1)):
                    msk = lo if sub == 0 else hi
                    ddt16 = ddt16 + jnp.where(lane16 == e, hsum(dxp, msk), 0.0)
                    gdsk = gdsk + jnp.where(lane16r == e, jnp.sum(hsum(dyx, msk), axis=0, keepdims=True), 0.0)
                dskp = _pair(lo, _col(dskv, e0), _col(dskv, e1))
                dxbc_ref[0, :, sl] = dxdt * dtp + dyp * dskp
                dh_sc[k] = dh_in
            dc_g = dc_g + jnp.dot(dcb.astype(BF16), bg, preferred_element_type=F32)
            db_g = db_g + jnp.dot(dcbt.astype(BF16), cg, preferred_element_type=F32)
            dxbc_ref[0, :, 1024 + grp * 128:1024 + (grp + 1) * 128] = db_g
            dxbc_ref[0, :, 1280 + grp * 128:1280 + (grp + 1) * 128] = dc_g
        da16 = jnp.dot(tri_up, ds16, precision=HI, preferred_element_type=F32)
        ddt16 = ddt16 + da16 * a_neg
        d_aneg = jnp.sum(da16 * dt, axis=0, keepdims=True)
        ddt_raw = ddt16 * _sigmoid(raw)
        ddt_ref[0] = ddt_raw
        gdtb_ref[...] += jnp.sum(ddt_raw, axis=0, keepdims=True)
        galog_ref[...] += d_aneg * a_neg
        gdsk_ref[...] += gdsk

    p16 = pl.BlockSpec((1, 16), lambda b, c: (0, 0))
    rev = lambda b, c: (b, nc - 1 - c, 0)
    return pl.pallas_call(
        body, name=name, grid=(bsz, nc),
        in_specs=[pl.BlockSpec((1, L, 1024), rev),
                  pl.BlockSpec((1, L, 256), lambda b, c: (b, nc - 1 - c, 4)),
                  pl.BlockSpec((1, L, 256), lambda b, c: (b, nc - 1 - c, 5)),
                  pl.BlockSpec((1, L, 16), rev), p16, p16, p16,
                  pl.BlockSpec((1, 1, SSD_PAIRS, 128, 128), lambda b, c: (b, nc - 1 - c, 0, 0, 0)),
                  pl.BlockSpec((1, L, 1024), rev)],
        out_specs=[pl.BlockSpec((1, L, CONV_DIM), rev), pl.BlockSpec((1, L, 16), rev), p16, p16, p16],
        out_shape=[jax.ShapeDtypeStruct((bsz, seq, CONV_DIM), F32), jax.ShapeDtypeStruct((bsz, seq, 16), F32),
                   jax.ShapeDtypeStruct((1, 16), F32), jax.ShapeDtypeStruct((1, 16), F32),
                   jax.ShapeDtypeStruct((1, 16), F32)],
        scratch_shapes=[pltpu.VMEM((SSD_PAIRS, 128, 128), F32)],
        compiler_params=_cp(("arbitrary", "arbitrary")))(xbc, xbc, xbc, dt_raw, dtb, alog, dsk, states, dy)


TAIL_BLK = TAIL_COL // 128


def _rope_apply(v, c, sa, sb):
    return v * c + pltpu.roll(v, 96, 1) * sa + pltpu.roll(v, 32, 1) * sb


def _rope_transposed(v, c, sa, sb):
    return v * c + pltpu.roll(v * sa, 32, 1) + pltpu.roll(v * sb, 96, 1)


def _head_rstd(lo, hi):
    ss = jnp.sum(lo * lo, axis=-1, keepdims=True) + jnp.sum(hi * hi, axis=-1, keepdims=True)
    return lax.rsqrt(ss * (1.0 / MLA_QK) + EPS)


def qk_prep_fwd(q_raw3, kv3, proj3, gq, gk, tabs, name, ts=256):
    bsz, seq, width = q_raw3.shape
    ts = min(ts, seq)

    def body(q_ref, kv_ref, tail_ref, gq_ref, gk_ref, c_ref, sa_ref, sb_ref, qo_ref, ko_ref):
        c, sa, sb = c_ref[0], sa_ref[0], sb_ref[0]
        k_hi = jnp.where(lax.broadcasted_iota(jnp.int32, (1, 128), 1) < 64, tail_ref[0], 0.0)
        for h in range(MLA_HEADS):
            lo_sl, hi_sl = slice(h * 256, h * 256 + 128), slice(h * 256 + 128, h * 256 + 256)
            for lo, hi, g_ref, o_ref in ((q_ref[0, :, lo_sl], q_ref[0, :, hi_sl], gq_ref, qo_ref),
                                         (kv_ref[0, :, lo_sl], k_hi, gk_ref, ko_ref)):
                r = _head_rstd(lo, hi)
                o_ref[0, :, lo_sl] = ((lo * r) * g_ref[:, lo_sl]).astype(BF16)
                o_ref[0, :, hi_sl] = _rope_apply((hi * r) * g_ref[:, hi_sl], c, sa, sb).astype(BF16)

    wide = pl.BlockSpec((1, ts, width), lambda b, i: (b, i, 0))
    tab = pl.BlockSpec((1, ts, 128), lambda b, i: (b, i, 0))
    gain = pl.BlockSpec((1, width), lambda b, i: (0, 0))
    return pl.pallas_call(
        body, name=name, grid=(bsz, seq // ts),
        in_specs=[wide, wide, pl.BlockSpec((1, ts, 128), lambda b, i: (b, i, TAIL_BLK)), gain, gain, tab, tab, tab],
        out_specs=[wide, wide], out_shape=[jax.ShapeDtypeStruct(q_raw3.shape, BF16)] * 2,
        compiler_params=_cp(("parallel", "parallel")))(q_raw3, kv3, proj3, gq, gk, *tabs)


def qk_prep_bwd(q_raw3, kv3, proj3, gq, gk, tabs, dq, dk, dv, name, ts=256):
    bsz, seq, width = q_raw3.shape
    ts = min(ts, seq)

    def body(q_ref, kv_ref, tail_ref, gq_ref, gk_ref, c_ref, sa_ref, sb_ref, dq_ref, dk_ref, dv_ref,
             dqr_ref, dkv_ref, dkr_ref, dgq_ref, dgk_ref):
        @pl.when(jnp.logical_and(pl.program_id(0) == 0, pl.program_id(1) == 0))
        def _():
            dgq_ref[...] = jnp.zeros(dgq_ref.shape, F32)
            dgk_ref[...] = jnp.zeros(dgk_ref.shape, F32)

        c, sa, sb = c_ref[0], sa_ref[0], sb_ref[0]
        k_hi = jnp.where(lax.broadcasted_iota(jnp.int32, (1, 128), 1) < 64, tail_ref[0], 0.0)
        dk_rope = jnp.zeros(k_hi.shape, F32)
        for h in range(MLA_HEADS):
            lo_sl, hi_sl = slice(h * 256, h * 256 + 128), slice(h * 256 + 128, h * 256 + 256)
            for is_k, lo, hi, g_ref, d_ref, dg_ref in ((False, q_ref[0, :, lo_sl], q_ref[0, :, hi_sl], gq_ref, dq_ref, dgq_ref),
                                                       (True, kv_ref[0, :, lo_sl], k_hi, gk_ref, dk_ref, dgk_ref)):
                r = _head_rstd(lo, hi)
                xh_lo, xh_hi = lo * r, hi * r
                dy_lo = d_ref[0, :, lo_sl]
                dy_hi = _rope_transposed(d_ref[0, :, hi_sl], c, sa, sb)
                dyg_lo, dyg_hi = dy_lo * g_ref[:, lo_sl], dy_hi * g_ref[:, hi_sl]
                cc = (jnp.sum(dyg_lo * xh_lo, axis=-1, keepdims=True)
                      + jnp.sum(dyg_hi * xh_hi, axis=-1, keepdims=True)) * (1.0 / MLA_QK)
                dx_lo, dx_hi = r * (dyg_lo - xh_lo * cc), r * (dyg_hi - xh_hi * cc)
                dg_ref[:, lo_sl] += jnp.sum(dy_lo * xh_lo, axis=0, keepdims=True)
                dg_ref[:, hi_sl] += jnp.sum(dy_hi * xh_hi, axis=0, keepdims=True)
                if is_k:
                    dkv_ref[0, :, lo_sl] = dx_lo
                    dkv_ref[0, :, hi_sl] = dv_ref[0, :, h * 128:(h + 1) * 128]
                    dk_rope = dk_rope + dx_hi
                else:
                    dqr_ref[0, :, lo_sl] = dx_lo
                    dqr_ref[0, :, hi_sl] = dx_hi
        dkr_ref[0] = dk_rope

    wide = pl.BlockSpec((1, ts, width), lambda b, i: (b, i, 0))
    tab = pl.BlockSpec((1, ts, 128), lambda b, i: (b, i, 0))
    gain = pl.BlockSpec((1, width), lambda b, i: (0, 0))
    return pl.pallas_call(
        body, name=name, grid=(bsz, seq // ts),
        in_specs=[wide, wide, pl.BlockSpec((1, ts, 128), lambda b, i: (b, i, TAIL_BLK)), gain, gain, tab, tab, tab,
                  wide, wide, pl.BlockSpec((1, ts, 1024), lambda b, i: (b, i, 0))],
        out_specs=[wide, wide, tab, gain, gain],
        out_shape=[jax.ShapeDtypeStruct(q_raw3.shape, F32), jax.ShapeDtypeStruct(q_raw3.shape, F32),
                   jax.ShapeDtypeStruct((bsz, seq, 128), F32), jax.ShapeDtypeStruct((1, width), F32),
                   jax.ShapeDtypeStruct((1, width), F32)],
        compiler_params=_cp(("arbitrary", "arbitrary")))(q_raw3, kv3, proj3, gq, gk, *tabs, dq, dk, dv)


def _scores(q_ref, k_ref, scale, causal, i, j, tq, tk):
    s = lax.dot_general(q_ref[0].astype(BF16), k_ref[0].astype(BF16), (((1,), (1,)), ((), ())),
                        preferred_element_type=F32) * scale
    if not causal:
        return s, None
    r = i * tq + lax.broadcasted_iota(jnp.int32, (tq, tk), 0)
    c = j * tk + lax.broadcasted_iota(jnp.int32, (tq, tk), 1)
    mask = r >= c
    return jnp.where(mask, s, NEG), mask


def attn_fwd(q, k, v, *, heads, dk, dv, v_mul, v_off, causal, scale, tq, tk, name, rider=None):
    bsz, sq, _ = q.shape
    sk = k.shape[1]
    tq, tk = min(tq, sq), min(tk, sk)
    nq, nk = sq // tq, sk // tk

    def body(q_ref, k_ref, v_ref, o_ref, lse_ref, m_sc, l_sc, acc_sc):
        i, j = pl.program_id(2), pl.program_id(3)

        @pl.when(j == 0)
        def _():
            m_sc[...] = jnp.full(m_sc.shape, NEG, F32)
            l_sc[...] = jnp.zeros(l_sc.shape, F32)
            acc_sc[...] = jnp.zeros(acc_sc.shape, F32)

        def compute(masked=False):
            s, _ = _scores(q_ref, k_ref, scale, masked, i, j, tq, tk)
            m_prev = m_sc[...]
            m_new = jnp.maximum(m_prev, jnp.max(s, axis=1, keepdims=True))
            p = jnp.exp(s - m_new)
            alpha = jnp.exp(m_prev - m_new)
            l_sc[...] = alpha * l_sc[...] + jnp.sum(p, axis=1, keepdims=True)
            acc_sc[...] = alpha * acc_sc[...] + jnp.dot(p.astype(BF16), v_ref[0].astype(BF16),
                                                        preferred_element_type=F32)
            m_sc[...] = m_new

        if causal:
            pl.when(j < i)(compute)
            pl.when(j == i)(functools.partial(compute, True))
        else:
            compute()

        @pl.when(j == nk - 1)
        def _():
            o_ref[0] = acc_sc[...] / l_sc[...]
            lse_ref[0, 0] = m_sc[...] + jnp.log(l_sc[...])

    kvj = (lambda i, j: jnp.minimum(i, j)) if causal else (lambda i, j: j)
    grid = (bsz, heads, nq, nk)
    in_specs = [pl.BlockSpec((1, tq, dk), lambda b, h, i, j: (b, i, h)),
                pl.BlockSpec((1, tk, dk), lambda b, h, i, j: (b, kvj(i, j), h)),
                pl.BlockSpec((1, tk, dv), lambda b, h, i, j: (b, kvj(i, j), v_mul * h + v_off))]
    out_specs = [pl.BlockSpec((1, tq, dv), lambda b, h, i, j: (b, i, h)),
                 pl.BlockSpec((1, 1, tq, 1), lambda b, h, i, j: (b, h, i, 0))]
    out_shape = [jax.ShapeDtypeStruct((bsz, sq, heads * dv), F32), jax.ShapeDtypeStruct((bsz, heads, sq, 1), F32)]
    scratch = [pltpu.VMEM((tq, 1), F32), pltpu.VMEM((tq, 1), F32), pltpu.VMEM((tq, dv), F32)]
    args, sem, io_aliases = [q, k, v], ("parallel", "parallel", "parallel", "arbitrary"), {}
    if rider is not None:
        body, in_specs, out_specs, out_shape, scratch, args, io_aliases = _attach(
            rider, body, grid, in_specs, out_specs, out_shape, scratch, args)
        sem = ("arbitrary",) * 4
    outs = pl.pallas_call(body, name=name, grid=grid, in_specs=in_specs, out_specs=out_specs, out_shape=out_shape,
                          scratch_shapes=scratch, input_output_aliases=io_aliases, compiler_params=_cp(sem))(*args)
    return outs[0], outs[1], outs[2:]


def _attn_grads(q_ref, k_ref, v_ref, do_ref, o_ref, lse_ref, scale, causal, i, j, tq, tk):
    s, mask = _scores(q_ref, k_ref, scale, causal, i, j, tq, tk)
    p = jnp.exp(s - lse_ref[0, 0])
    if mask is not None:
        p = jnp.where(mask, p, 0.0)
    do = do_ref[0]
    dp = lax.dot_general(do.astype(BF16), v_ref[0].astype(BF16), (((1,), (1,)), ((), ())),
                         preferred_element_type=F32)
    delta = jnp.sum(do * o_ref[0], axis=1, keepdims=True)
    ds = p * (dp - delta) * scale
    return p, ds


def attn_bwd(q, k, v, do, do_off, o, lse, *, heads, dk, dv, v_mul, v_off, causal, scale, tq, tk, name,
             rider_dq=None, rider_dkv=None):
    bsz, sq, _ = q.shape
    sk = k.shape[1]
    tq, tk = min(tq, sq), min(tk, sk)
    nq, nk = sq // tq, sk // tk

    def dq_body(q_ref, k_ref, v_ref, do_ref, o_ref, lse_ref, dq_ref, acc_sc):
        i, j = pl.program_id(2), pl.program_id(3)

        @pl.when(j == 0)
        def _():
            acc_sc[...] = jnp.zeros(acc_sc.shape, F32)

        def compute(masked=False):
            _, ds = _attn_grads(q_ref, k_ref, v_ref, do_ref, o_ref, lse_ref, scale, masked, i, j, tq, tk)
            acc_sc[...] += jnp.dot(ds.astype(BF16), k_ref[0].astype(BF16), preferred_element_type=F32)

        if causal:
            pl.when(j < i)(compute)
            pl.when(j == i)(functools.partial(compute, True))
        else:
            compute()

        @pl.when(j == nk - 1)
        def _():
            dq_ref[0] = acc_sc[...]

    kvj = (lambda i, j: jnp.minimum(i, j)) if causal else (lambda i, j: j)
    grid = (bsz, heads, nq, nk)
    in_specs = [pl.BlockSpec((1, tq, dk), lambda b, h, i, j: (b, i, h)),
                pl.BlockSpec((1, tk, dk), lambda b, h, i, j: (b, kvj(i, j), h)),
                pl.BlockSpec((1, tk, dv), lambda b, h, i, j: (b, kvj(i, j), v_mul * h + v_off)),
                pl.BlockSpec((1, tq, dv), lambda b, h, i, j: (b, i, h + do_off)),
                pl.BlockSpec((1, tq, dv), lambda b, h, i, j: (b, i, h)),
                pl.BlockSpec((1, 1, tq, 1), lambda b, h, i, j: (b, h, i, 0))]
    out_specs = [pl.BlockSpec((1, tq, dk), lambda b, h, i, j: (b, i, h))]
    out_shape = [jax.ShapeDtypeStruct((bsz, sq, heads * dk), F32)]
    scratch, args = [pltpu.VMEM((tq, dk), F32)], [q, k, v, do, o, lse]
    sem, io_aliases = ("parallel", "parallel", "parallel", "arbitrary"), {}
    if rider_dq is not None:
        dq_body, in_specs, out_specs, out_shape, scratch, args, io_aliases = _attach(
            rider_dq, dq_body, grid, in_specs, out_specs, out_shape, scratch, args)
        sem = ("arbitrary",) * 4
    dq_outs = pl.pallas_call(dq_body, name=name + "_dq", grid=grid, in_specs=in_specs, out_specs=out_specs,
                             out_shape=out_shape, scratch_shapes=scratch, input_output_aliases=io_aliases,
                             compiler_params=_cp(sem))(*args)
    dq = dq_outs[0]

    def dkv_body(q_ref, k_ref, v_ref, do_ref, o_ref, lse_ref, dk_ref, dv_ref, dk_sc, dv_sc):
        j, i = pl.program_id(2), pl.program_id(3)

        @pl.when(i == 0)
        def _():
            dk_sc[...] = jnp.zeros(dk_sc.shape, F32)
            dv_sc[...] = jnp.zeros(dv_sc.shape, F32)

        def compute(masked=False):
            p, ds = _attn_grads(q_ref, k_ref, v_ref, do_ref, o_ref, lse_ref, scale, masked, i, j, tq, tk)
            dv_sc[...] += lax.dot_general(p.astype(BF16), do_ref[0].astype(BF16), (((0,), (0,)), ((), ())),
                                          preferred_element_type=F32)
            dk_sc[...] += lax.dot_general(ds.astype(BF16), q_ref[0].astype(BF16), (((0,), (0,)), ((), ())),
                                          preferred_element_type=F32)

        if causal:
            pl.when(i > j)(compute)
            pl.when(i == j)(functools.partial(compute, True))
        else:
            compute()

        @pl.when(i == nq - 1)
        def _():
            dk_ref[0] = dk_sc[...]
            dv_ref[0] = dv_sc[...]

    qi = (lambda j, i: jnp.maximum(i, j)) if causal else (lambda j, i: i)
    grid = (bsz, heads, nk, nq)
    in_specs = [pl.BlockSpec((1, tq, dk), lambda b, h, j, i: (b, qi(j, i), h)),
                pl.BlockSpec((1, tk, dk), lambda b, h, j, i: (b, j, h)),
                pl.BlockSpec((1, tk, dv), lambda b, h, j, i: (b, j, v_mul * h + v_off)),
                pl.BlockSpec((1, tq, dv), lambda b, h, j, i: (b, qi(j, i), h + do_off)),
                pl.BlockSpec((1, tq, dv), lambda b, h, j, i: (b, qi(j, i), h)),
                pl.BlockSpec((1, 1, tq, 1), lambda b, h, j, i: (b, h, qi(j, i), 0))]
    out_specs = [pl.BlockSpec((1, tk, dk), lambda b, h, j, i: (b, j, h)),
                 pl.BlockSpec((1, tk, dv), lambda b, h, j, i: (b, j, h))]
    out_shape = [jax.ShapeDtypeStruct((bsz, sk, heads * dk), F32), jax.ShapeDtypeStruct((bsz, sk, heads * dv), F32)]
    scratch, args = [pltpu.VMEM((tk, dk), F32), pltpu.VMEM((tk, dv), F32)], [q, k, v, do, o, lse]
    sem, io_aliases = ("parallel", "parallel", "parallel", "arbitrary"), {}
    if rider_dkv is not None:
        dkv_body, in_specs, out_specs, out_shape, scratch, args, io_aliases = _attach(
            rider_dkv, dkv_body, grid, in_specs, out_specs, out_shape, scratch, args)
        sem = ("arbitrary",) * 4
    dkv_outs = pl.pallas_call(dkv_body, name=name + "_dkv", grid=grid, in_specs=in_specs, out_specs=out_specs,
                              out_shape=out_shape, scratch_shapes=scratch, input_output_aliases=io_aliases,
                              compiler_params=_cp(sem))(*args)
    return dq, dkv_outs[0], dkv_outs[1], dq_outs[1:], dkv_outs[2:]


def attn_bwd_fused(q, k, v, do, do_off, o, lse, *, heads, dk, dv, v_mul, v_off, causal, scale, tq, tk, name,
                   rider=None):
    bsz, sq, _ = q.shape
    sk = k.shape[1]
    tq, tk = min(tq, sq), min(tk, sk)
    nq, nk = sq // tq, sk // tk

    def body(q_ref, k_ref, v_ref, do_ref, o_ref, lse_ref, dq_ref, dk_ref, dv_ref, dk_sc, dv_sc):
        j, i = pl.program_id(2), pl.program_id(3)

        @pl.when(jnp.logical_and(j == 0, i == 0))
        def _():
            dq_ref[...] = jnp.zeros(dq_ref.shape, F32)

        @pl.when(i == 0)
        def _():
            dk_sc[...] = jnp.zeros(dk_sc.shape, F32)
            dv_sc[...] = jnp.zeros(dv_sc.shape, F32)

        def compute(masked=False):
            p, ds = _attn_grads(q_ref, k_ref, v_ref, do_ref, o_ref, lse_ref, scale, masked, i, j, tq, tk)
            ds_b = ds.astype(BF16)
            dv_sc[...] += lax.dot_general(p.astype(BF16), do_ref[0].astype(BF16), (((0,), (0,)), ((), ())),
                                          preferred_element_type=F32)
            dk_sc[...] += lax.dot_general(ds_b, q_ref[0].astype(BF16), (((0,), (0,)), ((), ())),
                                          preferred_element_type=F32)
            rows = pl.ds(pl.multiple_of(i * tq, tq), tq)
            dq_ref[0, rows, :] += jnp.dot(ds_b, k_ref[0].astype(BF16), preferred_element_type=F32)

        if causal:
            pl.when(i > j)(compute)
            pl.when(i == j)(functools.partial(compute, True))
        else:
            compute()

        @pl.when(i == nq - 1)
        def _():
            dk_ref[0] = dk_sc[...]
            dv_ref[0] = dv_sc[...]

    qi = (lambda j, i: jnp.maximum(i, j)) if causal else (lambda j, i: i)
    grid = (bsz, heads, nk, nq)
    in_specs = [pl.BlockSpec((1, tq, dk), lambda b, h, j, i: (b, qi(j, i), h)),
                pl.BlockSpec((1, tk, dk), lambda b, h, j, i: (b, j, h)),
                pl.BlockSpec((1, tk, dv), lambda b, h, j, i: (b, j, v_mul * h + v_off)),
                pl.BlockSpec((1, tq, dv), lambda b, h, j, i: (b, qi(j, i), h + do_off)),
                pl.BlockSpec((1, tq, dv), lambda b, h, j, i: (b, qi(j, i), h)),
                pl.BlockSpec((1, 1, tq, 1), lambda b, h, j, i: (b, h, qi(j, i), 0))]
    out_specs = [pl.BlockSpec((1, sq, dk), lambda b, h, j, i: (b, 0, h)),
                 pl.BlockSpec((1, tk, dk), lambda b, h, j, i: (b, j, h)),
                 pl.BlockSpec((1, tk, dv), lambda b, h, j, i: (b, j, h))]
    out_shape = [jax.ShapeDtypeStruct((bsz, sq, heads * dk), F32), jax.ShapeDtypeStruct((bsz, sk, heads * dk), F32),
                 jax.ShapeDtypeStruct((bsz, sk, heads * dv), F32)]
    scratch, args = [pltpu.VMEM((tk, dk), F32), pltpu.VMEM((tk, dv), F32)], [q, k, v, do, o, lse]
    sem, io_aliases = ("parallel", "parallel", "arbitrary", "arbitrary"), {}
    if rider is not None:
        body, in_specs, out_specs, out_shape, scratch, args, io_aliases = _attach(
            rider, body, grid, in_specs, out_specs, out_shape, scratch, args)
        sem = ("arbitrary",) * 4
    outs = pl.pallas_call(body, name=name, grid=grid, in_specs=in_specs, out_specs=out_specs, out_shape=out_shape,
                          scratch_shapes=scratch, input_output_aliases=io_aliases, compiler_params=_cp(sem))(*args)
    return outs[0], outs[1], outs[2], outs[3:]


def loss_head(y, target, name, tr=512):
    rows, width = y.shape
    tr = min(tr, rows)

    def body(y_ref, t_ref, dy_ref, l_ref):
        @pl.when(pl.program_id(0) == 0)
        def _():
            l_ref[...] = jnp.zeros(l_ref.shape, F32)

        err = y_ref[...] - t_ref[...]
        dy_ref[...] = err * (1.0 / width)
        l_ref[...] += jnp.sum(jnp.sum(err * err, axis=1, keepdims=True), axis=0, keepdims=True) * (0.5 / width)

    return pl.pallas_call(
        body, name=name, grid=(rows // tr,),
        in_specs=[pl.BlockSpec((tr, width), lambda i: (i, 0)), pl.BlockSpec((tr, width), lambda i: (i, 0))],
        out_specs=[pl.BlockSpec((tr, width), lambda i: (i, 0)), pl.BlockSpec((1, 1), lambda i: (0, 0))],
        out_shape=[jax.ShapeDtypeStruct((rows, width), F32), jax.ShapeDtypeStruct((1, 1), F32)],
        compiler_params=_cp(("arbitrary",)))(y, target)


def _mesh_pos():
    return lax.axis_index("x"), lax.axis_index("y"), lax.axis_index("c")


def all_gather(blocks, name):
    n_arr = len(blocks)

    def body(*refs):
        x_refs, out_refs = refs[:n_arr], refs[n_arr:2 * n_arr]
        send_sems, recv_sems, local_sems = refs[2 * n_arr:]
        x, y, c = _mesh_pos()
        me, sibling = (x, y, c), (x, y, 1 - c)
        chips = [(1 - x, y), (x, 1 - y), (1 - x, 1 - y)]

        def slot(a, px, py, pc):
            return out_refs[a].at[4 * px + 2 * py + pc]

        def copy(a, k, blk, to, src=None):
            return pltpu.make_async_remote_copy(
                src_ref=slot(a, *blk) if src is None else src, dst_ref=slot(a, *blk),
                send_sem=send_sems.at[7 * a + k], recv_sem=recv_sems.at[7 * a + k],
                device_id=to, device_id_type=pl.DeviceIdType.MESH)

        mine = [pltpu.make_async_copy(x_refs[a], slot(a, *me), local_sems.at[a]) for a in range(n_arr)]
        for cp in mine:
            cp.start()
        first = []
        for a in range(n_arr):
            first.append(copy(a, 0, me, sibling, src=x_refs[a]))
            first += [copy(a, 1 + j, me, (*chip, c), src=x_refs[a]) for j, chip in enumerate(chips)]
        for cp in first:
            cp.start()
        passed = []
        for j, chip in enumerate(chips):
            for a in range(n_arr):
                copy(a, 1 + j, (*chip, c), me).wait_recv()
                passed.append(copy(a, 4 + j, (*chip, c), sibling))
                passed[-1].start()
        for a in range(n_arr):
            copy(a, 0, sibling, me).wait_recv()
            for j, chip in enumerate(chips):
                copy(a, 4 + j, (*chip, 1 - c), me).wait_recv()
        for cp in first + passed:
            cp.wait_send()
        for cp in mine:
            cp.wait()

    return pl.pallas_call(
        body, name=name,
        out_shape=[jax.ShapeDtypeStruct((N_DEV,) + b.shape, b.dtype) for b in blocks],
        in_specs=[pl.BlockSpec(memory_space=pl.ANY)] * n_arr,
        out_specs=[pl.BlockSpec(memory_space=pl.ANY)] * n_arr,
        scratch_shapes=[pltpu.SemaphoreType.DMA((7 * n_arr,)), pltpu.SemaphoreType.DMA((7 * n_arr,)),
                        pltpu.SemaphoreType.DMA((n_arr,))],
    )(*blocks)


def _slot(px, py, pc):
    return 4 * px + 2 * py + pc


def gather_near_rider(blocks):
    n_arr = len(blocks)

    def parts(rins, routs, sems):
        send_sems, recv_sems, local_sems = sems
        x, y, c = _mesh_pos()
        peers = [(x, y, 1 - c), (1 - x, y, c), (x, 1 - y, c)]
        local = [pltpu.make_async_copy(rins[a], routs[a].at[_slot(x, y, c)], local_sems.at[a]) for a in range(n_arr)]
        sends = [pltpu.make_async_remote_copy(
            src_ref=rins[a], dst_ref=routs[a].at[_slot(x, y, c)], send_sem=send_sems.at[3 * a + k],
            recv_sem=recv_sems.at[3 * a + k], device_id=p, device_id_type=pl.DeviceIdType.MESH)
            for a in range(n_arr) for k, p in enumerate(peers)]
        recvs = [pltpu.make_async_remote_copy(
            src_ref=rins[a], dst_ref=routs[a].at[_slot(*p)], send_sem=send_sems.at[3 * a + k],
            recv_sem=recv_sems.at[3 * a + k], device_id=p, device_id_type=pl.DeviceIdType.MESH)
            for a in range(n_arr) for k, p in enumerate(peers)]
        return local, sends, recvs

    def start(rins, routs, sems):
        local, sends, _ = parts(rins, routs, sems)
        for cp in local + sends:
            cp.start()

    def finish(rins, routs, sems):
        local, sends, recvs = parts(rins, routs, sems)
        for cp in recvs:
            cp.wait_recv()
        for cp in sends:
            cp.wait_send()
        for cp in local:
            cp.wait()

    return Rider(blocks, [jax.ShapeDtypeStruct((N_DEV,) + b.shape, b.dtype) for b in blocks],
                 [pltpu.SemaphoreType.DMA((3 * n_arr,)), pltpu.SemaphoreType.DMA((3 * n_arr,)),
                  pltpu.SemaphoreType.DMA((n_arr,))], start, finish)


def gather_diagonal_rider(blocks, gathered):
    n_arr = len(blocks)

    def parts(rins, routs, sems):
        send_sems, recv_sems = sems
        x, y, c = _mesh_pos()
        p = (1 - x, 1 - y, c)
        sends = [pltpu.make_async_remote_copy(
            src_ref=rins[a], dst_ref=routs[a].at[_slot(x, y, c)], send_sem=send_sems.at[a],
            recv_sem=recv_sems.at[a], device_id=p, device_id_type=pl.DeviceIdType.MESH) for a in range(n_arr)]
        recvs = [pltpu.make_async_remote_copy(
            src_ref=rins[a], dst_ref=routs[a].at[_slot(*p)], send_sem=send_sems.at[a],
            recv_sem=recv_sems.at[a], device_id=p, device_id_type=pl.DeviceIdType.MESH) for a in range(n_arr)]
        return sends, recvs

    def start(rins, routs, sems):
        for cp in parts(rins, routs, sems)[0]:
            cp.start()

    def finish(rins, routs, sems):
        sends, recvs = parts(rins, routs, sems)
        for cp in recvs:
            cp.wait_recv()
        for cp in sends:
            cp.wait_send()

    return Rider(list(blocks) + list(gathered), [jax.ShapeDtypeStruct(g.shape, g.dtype) for g in gathered],
                 [pltpu.SemaphoreType.DMA((n_arr,)), pltpu.SemaphoreType.DMA((n_arr,))], start, finish,
                 aliases={n_arr + a: a for a in range(n_arr)})


def gather_forward_rider(gathered):
    n_arr = len(gathered)

    def parts(rins, routs, sems):
        send_sems, recv_sems = sems
        x, y, c = _mesh_pos()
        chips = [(1 - x, y), (x, 1 - y), (1 - x, 1 - y)]
        sends = [pltpu.make_async_remote_copy(
            src_ref=routs[a].at[_slot(px, py, c)], dst_ref=routs[a].at[_slot(px, py, c)],
            send_sem=send_sems.at[3 * a + k], recv_sem=recv_sems.at[3 * a + k],
            device_id=(x, y, 1 - c), device_id_type=pl.DeviceIdType.MESH)
            for a in range(n_arr) for k, (px, py) in enumerate(chips)]
        recvs = [pltpu.make_async_remote_copy(
            src_ref=routs[a].at[_slot(px, py, c)], dst_ref=routs[a].at[_slot(px, py, 1 - c)],
            send_sem=send_sems.at[3 * a + k], recv_sem=recv_sems.at[3 * a + k],
            device_id=(x, y, 1 - c), device_id_type=pl.DeviceIdType.MESH)
            for a in range(n_arr) for k, (px, py) in enumerate(chips)]
        return sends, recvs

    def start(rins, routs, sems):
        for cp in parts(rins, routs, sems)[0]:
            cp.start()

    def finish(rins, routs, sems):
        sends, recvs = parts(rins, routs, sems)
        for cp in recvs:
            cp.wait_recv()
        for cp in sends:
            cp.wait_send()

    return Rider(gathered, [jax.ShapeDtypeStruct(g.shape, g.dtype) for g in gathered],
                 [pltpu.SemaphoreType.DMA((3 * n_arr,)), pltpu.SemaphoreType.DMA((3 * n_arr,))], start, finish,
                 aliases={a: a for a in range(n_arr)})


def sibling_stage_rider(blobs):
    n_arr = len(blobs)

    def copies(rins, routs, sems):
        send_sems, recv_sems = sems
        x, y, c = _mesh_pos()
        return [pltpu.make_async_remote_copy(
            src_ref=rins[a].at[2 * j + (1 - c)], dst_ref=routs[a].at[j],
            send_sem=send_sems.at[4 * a + j], recv_sem=recv_sems.at[4 * a + j],
            device_id=(x, y, 1 - c), device_id_type=pl.DeviceIdType.MESH)
            for a in range(n_arr) for j in range(4)]

    def start(rins, routs, sems):
        for cp in copies(rins, routs, sems):
            cp.start()

    def finish(rins, routs, sems):
        for cp in copies(rins, routs, sems):
            cp.wait()

    return Rider(blobs, [jax.ShapeDtypeStruct((4,) + b.shape[1:], b.dtype) for b in blobs],
                 [pltpu.SemaphoreType.DMA((4 * n_arr,)), pltpu.SemaphoreType.DMA((4 * n_arr,))], start, finish)


def chip_partial(blob, from_sibling, name, tr):
    _, rows, cols = blob.shape
    c_idx = lax.axis_index("c").astype(jnp.int32).reshape(1)

    def body(c_ref, b_ref, s_ref, o_ref):
        o_ref[...] = (b_ref[...].astype(F32) + s_ref[...].astype(F32)).astype(o_ref.dtype)

    return pl.pallas_call(
        body, name=name,
        grid_spec=pltpu.PrefetchScalarGridSpec(
            num_scalar_prefetch=1, grid=(4, rows // tr),
            in_specs=[pl.BlockSpec((1, tr, cols), lambda j, i, c_ref: (2 * j + c_ref[0], i, 0)),
                      pl.BlockSpec((1, tr, cols), lambda j, i, c_ref: (j, i, 0))],
            out_specs=pl.BlockSpec((1, tr, cols), lambda j, i, c_ref: (j, i, 0))),
        out_shape=jax.ShapeDtypeStruct((4, rows, cols), blob.dtype),
        compiler_params=_cp(("parallel", "parallel")))(c_idx, blob, from_sibling)


def chip_stage_rider(partials, diagonal, row_ranges, received=None):
    n_arr = len(partials)
    n_remote = 1 if diagonal else 2

    def copies(rins, routs, sems):
        send_sems, recv_sems = sems[0], sems[1]
        x, y, c = _mesh_pos()
        chips = [(1 - x, 1 - y)] if diagonal else [(1 - x, y), (x, 1 - y)]
        return [pltpu.make_async_remote_copy(
            src_ref=rins[a].at[2 * px + py, pl.ds(*row_ranges[a])],
            dst_ref=routs[a].at[k if diagonal else 1 + k, pl.ds(*row_ranges[a])],
            send_sem=send_sems.at[n_remote * a + k], recv_sem=recv_sems.at[n_remote * a + k],
            device_id=(px, py, c), device_id_type=pl.DeviceIdType.MESH)
            for a in range(n_arr) for k, (px, py) in enumerate(chips)]

    def local(rins, routs, sems):
        x, y, _ = _mesh_pos()
        return [pltpu.make_async_copy(rins[a].at[2 * x + y, pl.ds(*row_ranges[a])],
                                      routs[a].at[0, pl.ds(*row_ranges[a])], sems[2].at[a]) for a in range(n_arr)]

    def start(rins, routs, sems):
        for cp in copies(rins, routs, sems) + ([] if diagonal else local(rins, routs, sems)):
            cp.start()

    def finish(rins, routs, sems):
        for cp in copies(rins, routs, sems) + ([] if diagonal else local(rins, routs, sems)):
            cp.wait()

    sem_list = [pltpu.SemaphoreType.DMA((n_remote * n_arr,)), pltpu.SemaphoreType.DMA((n_remote * n_arr,))]
    if not diagonal:
        sem_list.append(pltpu.SemaphoreType.DMA((n_arr,)))
    out_shapes = [jax.ShapeDtypeStruct((1 if diagonal else 3,) + p.shape[1:], p.dtype) for p in partials]
    if received is None:
        return Rider(partials, out_shapes, sem_list, start, finish)
    return Rider(list(partials) + list(received), out_shapes, sem_list, start, finish,
                 aliases={n_arr + a: a for a in range(n_arr)})


def _halves(partials):
    first, second = [], []
    for p in partials:
        rows = p.shape[1]
        cut = (rows // 32 + (rows // 16) % 2) * 16
        first.append((0, cut))
        second.append((cut, rows - cut))
    return first, second


def adamw(parts, w, m, v, layer, earlier, name, tr=128):
    n_layers, rows, cols = w.shape
    tr = min(tr, rows)
    n_arr = len(parts)
    n_pass = 0 if earlier is None else 4
    c1 = 1.0 - ADAM_B1 ** ADAM_STEP
    c2 = 1.0 - ADAM_B2 ** ADAM_STEP

    def body(*refs):
        p_refs = refs[:n_arr]
        w_ref, m_ref, v_ref = refs[n_arr:n_arr + 3]
        g_ref, d_ref, nm_ref, nv_ref = refs[n_arr + 3 + n_pass:]
        g = None
        for p_ref in p_refs:
            for s in range(p_ref.shape[0]):
                term = p_ref[s].astype(F32)
                g = term if g is None else g + term
        nm = ADAM_B1 * m_ref[0] + (1.0 - ADAM_B1) * g
        nv = ADAM_B2 * v_ref[0] + (1.0 - ADAM_B2) * (g * g)
        m_hat = nm / c1
        v_hat = nv / c2
        g_ref[0] = g
        d_ref[0] = -ADAM_LR * (m_hat / (jnp.sqrt(v_hat) + ADAM_EPS) + ADAM_WD * w_ref[0])
        nm_ref[0] = nm
        nv_ref[0] = nv

    blk = pl.BlockSpec((1, tr, cols), lambda i: (layer, i, 0))
    in_specs = [pl.BlockSpec((p.shape[0], tr, cols), lambda i: (0, i, 0)) for p in parts] + [blk, blk, blk]
    args = list(parts) + [w, m, v]
    aliases = {}
    if earlier is not None:
        in_specs += [pl.BlockSpec(memory_space=pl.ANY)] * 4
        aliases = {n_arr + 3 + k: k for k in range(4)}
        args += list(earlier)
    return pl.pallas_call(
        body, name=name, grid=(rows // tr,), in_specs=in_specs, out_specs=[blk, blk, blk, blk],
        out_shape=[jax.ShapeDtypeStruct((n_layers, rows, cols), F32)] * 4, input_output_aliases=aliases,
        compiler_params=_cp(("parallel",)))(*args)


def _pack_local(fam, shards, dtype):
    parts = []
    for n, rows, prows, transposed in fam:
        a = shards[n].T if transposed else shards[n]
        parts.append(jnp.pad(a.astype(dtype), ((0, prows - rows), (0, 0))))
    return jnp.concatenate(parts, axis=0)


def _unpack_gathered(fam, gathered):
    out, off = {}, 0
    for n, rows, prows, _ in fam:
        out[n] = gathered[:, off:off + rows].reshape(N_DEV * rows, gathered.shape[2])
        off += prows
    return out


def _pack_full_grads(fam, grads):
    segs = []
    for n, rows, prows, _ in fam:
        g = grads[n].astype(BF16)
        g = g.reshape(N_DEV, rows, g.shape[1])
        segs.append(jnp.pad(g, ((0, 0), (0, prows - rows), (0, 0))))
    return jnp.concatenate(segs, axis=1)


def _pack_local_layers(fam, src, dtype):
    parts = []
    for n, rows, prows, transposed in fam:
        a = src[n].transpose(0, 2, 1) if transposed else src[n]
        parts.append(jnp.pad(a.astype(dtype), ((0, 0), (0, prows - rows), (0, 0))))
    return jnp.concatenate(parts, axis=1)


def _unpack_local_layers(fam, blob):
    out, off = {}, 0
    for n, rows, prows, transposed in fam:
        a = blob[:, off:off + rows]
        out[n] = a.transpose(0, 2, 1) if transposed else a
        off += prows
    return out


def _pack_small(vals):
    flat = jnp.concatenate([vals[n].reshape(-1).astype(F32) for n, _ in SMALL])
    flat = jnp.pad(flat, (0, SMALL_ROWS * 128 - flat.shape[0]))
    return flat.reshape(SMALL_ROWS, 128)


def _unpack_small(flat2d):
    flat = flat2d.reshape(-1)
    out, off = {}, 0
    for n, k in SMALL:
        out[n] = flat[off:off + DEPTH * k].reshape(DEPTH, k)
        off += DEPTH * k
    return out


def _pad_w_in_t(w_t):
    return jnp.concatenate([w_t[:2560], w_t[2576:IN_COLS], w_t[2560:2576],
                            jnp.zeros((IN_PAD - IN_COLS, w_t.shape[1]), w_t.dtype)], axis=0)


def _unpad_w_in_t(g_t):
    return jnp.concatenate([g_t[:2560], g_t[3648:3664], g_t[2560:3648]], axis=0)


def _pad_head_rows(w_t):
    k = w_t.shape[1]
    v = jnp.pad(w_t.reshape(MLA_HEADS, MLA_QK, k), ((0, 0), (0, MLA_QK_PAD - MLA_QK), (0, 0)))
    return v.reshape(MLA_HEADS * MLA_QK_PAD, k)


def _unpad_head_rows(g_t):
    k = g_t.shape[1]
    return g_t.reshape(MLA_HEADS, MLA_QK_PAD, k)[:, :MLA_QK].reshape(MLA_HEADS * MLA_QK, k)


def _row(v):
    return v.reshape(1, -1).astype(F32)


def layer_fwd(x, memf, w, p, tabs, bsz, next_blocks):
    t_rows = x.shape[0]
    seq = t_rows // bsz
    r = {}
    r["x"] = x
    h1 = rms_fwd(x, 0, D_MODEL, _row(p["attn_norm_g"]), 1, D_MODEL, BF16, "norm1_fwd")
    proj = mm([(h1, w["w_in"])], "nt", tn=768, name="in_proj")
    r["h1"], r["proj"] = h1, proj
    proj3 = proj.reshape(bsz, seq, IN_PAD)

    w8 = jnp.concatenate([p["conv_w_full"], p["conv_b"].reshape(1, -1), jnp.zeros((3, CONV_DIM), F32)], 0)
    xbc = conv_fwd(proj3, w8, "conv_fwd")
    tail = proj[:, TAIL_COL:TAIL_COL + 128]
    dt_raw = tail[:, 64:80].reshape(bsz, seq, 16)
    y, states = ssd_fwd(xbc, dt_raw, _row(p["dt_bias"]), _row(p["a_log"]), _row(p["d_skip"]), "ssd_fwd")
    y2 = y.reshape(t_rows, SSD_INNER)
    yg = ew(_gate_fwd, [(y2, 0), (proj, 0)], SSD_INNER, (F32,), "ssd_gate_fwd")
    y_ssd = rms_fwd(yg, 0, SSD_INNER, _row(p["ssd_norm_g"]), 2, 512, BF16, "ssd_norm_fwd")
    r.update(w8=w8, xbc=xbc, dt_raw=dt_raw, states=states, y=y2, yg=yg)

    qn = rms_fwd(proj, 5, Q_LORA, _row(p["q_a_norm_g"]), 1, Q_LORA, BF16, "qa_norm_fwd")
    kvn = rms_fwd(proj, 6, Q_LORA, _row(p["kv_a_norm_g"]), 1, Q_LORA, BF16, "kva_norm_fwd")
    q_raw = mm([(qn, w["w_q_b"])], "nt", name="q_b_proj")
    kv = mm([(kvn, w["w_kv_b"])], "nt", name="kv_b_proj")
    gq = _row(jnp.tile(jnp.pad(p["mla_q_norm_g"], (0, 64)), MLA_HEADS))
    gk = _row(jnp.tile(jnp.pad(p["mla_k_norm_g"], (0, 64)), MLA_HEADS))
    q_raw3, kv3 = q_raw.reshape(bsz, seq, 2048), kv.reshape(bsz, seq, 2048)
    q_fin, k_fin = qk_prep_fwd(q_raw3, kv3, proj3, gq, gk, tabs, "qk_prep_fwd")
    riding = next_blocks is not None
    o, lse, gathered = attn_fwd(q_fin, k_fin, kv3, heads=MLA_HEADS, dk=MLA_QK_PAD, dv=MLA_V, v_mul=2, v_off=1,
                                causal=True, scale=MLA_QK ** -0.5, tq=512, tk=512,
                                name="mla_attn_fwd_gather_near" if riding else "mla_attn_fwd",
                                rider=gather_near_rider(next_blocks) if riding else None)
    r.update(qn=qn, kvn=kvn, q_raw3=q_raw3, gq=gq, gk=gk, q_fin=q_fin, k_fin=k_fin, kv3=kv3, o=o, lse=lse)
    x2 = mm([(y_ssd, w["w_out"][:SSD_INNER]), (o.reshape(t_rows, SSD_INNER), w["w_out"][SSD_INNER:])], "nn",
            extras=(x,), epilogue=_add_res, name="out_proj")
    r.update(y_ssd=y_ssd, x2=x2)

    h2 = rms_fwd(x2, 0, D_MODEL, _row(p["xattn_norm_g"]), 1, D_MODEL, BF16, "norm2_fwd")
    mn = rms_fwd(memf, 0, D_MODEL, _row(p["mem_norm_g"]), 1, D_MODEL, BF16, "mem_norm_fwd")
    xq_raw = mm([(h2, w["w_xq"])], "nn", name="xq_proj")
    xk_raw = mm([(mn, w["w_xk"])], "nn", name="xk_proj")
    xv = mm([(mn, w["w_xv"])], "nn", name="xv_proj")
    gxq = _row(jnp.tile(p["xq_norm_g"], X_HEADS))
    gxk = _row(jnp.tile(p["xk_norm_g"], X_HEADS))
    xq = rms_fwd(xq_raw, 0, X_INNER, gxq, X_HEADS, X_HEAD_DIM, BF16, "xq_norm_fwd")
    xk = rms_fwd(xk_raw, 0, X_INNER, gxk, X_HEADS, X_HEAD_DIM, BF16, "xk_norm_fwd")
    mlen = memf.shape[0] // bsz
    xq3, xk3, xv3 = xq.reshape(bsz, seq, X_INNER), xk.reshape(bsz, mlen, X_INNER), xv.reshape(bsz, mlen, X_INNER)
    xo, xlse, _ = attn_fwd(xq3, xk3, xv3, heads=X_HEADS, dk=X_HEAD_DIM, dv=X_HEAD_DIM, v_mul=1, v_off=0,
                           causal=False, scale=X_HEAD_DIM ** -0.5, tq=512, tk=256, name="xattn_fwd")
    xo2 = xo.reshape(t_rows, X_INNER)
    x3 = mm([(xo2, w["w_xo"])], "nt", extras=(x2,), epilogue=_add_res, name="xo_proj")
    r.update(h2=h2, mn=mn, xq_raw=xq_raw, xk_raw=xk_raw, gxq=gxq, gxk=gxk, xq3=xq3, xk3=xk3, xv3=xv3,
             xo=xo, xlse=xlse, x3=x3)

    h3 = rms_fwd(x3, 0, D_MODEL, _row(p["ffn_norm_g"]), 1, D_MODEL, BF16, "norm3_fwd")
    if riding:
        (gate, up, act), gathered = mm(
            [(h3, w["w_gate"]), (h3, w["w_up"])], "nt", epilogue=_swiglu_fwd, separate=True,
            out_dtypes=(BF16, BF16, BF16), name="gate_up_proj_gather_diagonal",
            rider=gather_diagonal_rider(next_blocks, gathered))
        x4, gathered = mm([(act, w["w_down"])], "nn", extras=(x3,), epilogue=_add_res, tm=512,
                          name="down_proj_gather_forward", rider=gather_forward_rider(gathered))
    else:
        gate, up, act = mm([(h3, w["w_gate"]), (h3, w["w_up"])], "nt", epilogue=_swiglu_fwd, separate=True,
                           out_dtypes=(BF16, BF16, BF16), name="gate_up_proj")
        x4 = mm([(act, w["w_down"])], "nn", extras=(x3,), epilogue=_add_res, tm=512, name="down_proj")
        gathered = None
    r.update(h3=h3, gate=gate, up=up, act=act)
    return x4, r, gathered


def layer_bwd(dx4, r, memf, w, p, tabs, bsz, pending):
    t_rows = dx4.shape[0]
    seq = t_rows // bsz
    big, small = {}, {}
    rider_xy = rider_diag = None

    if pending is None:
        dgate, dup = mm([(dx4, w["w_down"])], "nt", extras=(r["gate"], r["up"]), epilogue=_swiglu_bwd,
                        out_dtypes=(BF16, BF16), name="down_proj_bwd")
    else:
        (dgate, dup), from_sib = mm([(dx4, w["w_down"])], "nt", extras=(r["gate"], r["up"]), epilogue=_swiglu_bwd,
                                    out_dtypes=(BF16, BF16), name="down_proj_bwd_sibling_exchange",
                                    rider=sibling_stage_rider(pending))
        partials = [chip_partial(b, s, "grad_chip_partial", tr) for b, s, tr in zip(pending, from_sib, PARTIAL_TR)]
        first_half, second_half = _halves(partials)
    if pending is None:
        big["w_down"] = mm([(r["act"], dx4)], "tn", out_dtypes=(BF16,), name="down_proj_wgrad")
    else:
        big["w_down"], recv_diag = mm([(r["act"], dx4)], "tn", out_dtypes=(BF16,),
                                      name="down_proj_wgrad_diagonal_exchange",
                                      rider=chip_stage_rider(partials, True, first_half))
    big["w_gate"] = mm([(dgate, r["h3"])], "tn", out_dtypes=(BF16,), tm=512, tn=1024, name="gate_proj_wgrad")
    big["w_up"] = mm([(dup, r["h3"])], "tn", out_dtypes=(BF16,), tm=512, tn=1024, name="gate_proj_wgrad")
    if pending is None:
        dh3 = mm([(dgate, w["w_gate"]), (dup, w["w_up"])], "nn", tm=512, tn=256, name="gate_up_proj_bwd")
    else:
        dh3, recv_xy = mm([(dgate, w["w_gate"]), (dup, w["w_up"])], "nn", tm=512, tn=256,
                          name="gate_up_proj_bwd_near_exchange",
                          rider=chip_stage_rider(partials, False, first_half))
        rider_xy = chip_stage_rider(partials, False, second_half, received=recv_xy)
        rider_diag = chip_stage_rider(partials, True, second_half, received=recv_diag)
    dx3, g = rms_bwd(r["x3"], 0, D_MODEL, _row(p["ffn_norm_g"]), 1, D_MODEL, dh3, dx4, "norm3_bwd")
    small["ffn_norm_g"] = g[0]

    xo2 = r["xo"].reshape(t_rows, X_INNER)
    dxo = mm([(dx3, w["w_xo"])], "nn", name="xo_proj_bwd")
    big["w_xo"] = mm([(dx3, xo2)], "tn", out_dtypes=(BF16,), tm=512, name="xo_proj_wgrad")
    dxq, dxk, dxv, _ = attn_bwd_fused(r["xq3"], r["xk3"], r["xv3"], dxo.reshape(bsz, seq, X_INNER), 0, r["xo"],
                                      r["xlse"], heads=X_HEADS, dk=X_HEAD_DIM, dv=X_HEAD_DIM, v_mul=1, v_off=0,
                                      causal=False, scale=X_HEAD_DIM ** -0.5, tq=512, tk=256, name="xattn_bwd")
    mrows = memf.shape[0]
    dxq_raw, g = rms_bwd(r["xq_raw"], 0, X_INNER, r["gxq"], X_HEADS, X_HEAD_DIM, dxq.reshape(t_rows, X_INNER),
                         None, "xq_norm_bwd")
    small["xq_norm_g"] = g.reshape(X_HEADS, X_HEAD_DIM).sum(0)
    dxk_raw, g = rms_bwd(r["xk_raw"], 0, X_INNER, r["gxk"], X_HEADS, X_HEAD_DIM, dxk.reshape(mrows, X_INNER),
                         None, "xk_norm_bwd")
    small["xk_norm_g"] = g.reshape(X_HEADS, X_HEAD_DIM).sum(0)
    dxv2 = dxv.reshape(mrows, X_INNER)
    big["w_xq"] = mm([(r["h2"], dxq_raw)], "tn", out_dtypes=(BF16,), name="xq_proj_wgrad")
    big["w_xk"] = mm([(r["mn"], dxk_raw)], "tn", out_dtypes=(BF16,), name="xkv_proj_wgrad")
    big["w_xv"] = mm([(r["mn"], dxv2)], "tn", out_dtypes=(BF16,), name="xkv_proj_wgrad")
    dh2 = mm([(dxq_raw, w["w_xq"])], "nt", name="xq_proj_bwd")
    dmn = mm([(dxk_raw, w["w_xk"]), (dxv2, w["w_xv"])], "nt", name="xkv_proj_bwd")
    _, g = rms_bwd(memf, 0, D_MODEL, _row(p["mem_norm_g"]), 1, D_MODEL, dmn, None, "mem_norm_bwd")
    small["mem_norm_g"] = g[0]
    dx2, g = rms_bwd(r["x2"], 0, D_MODEL, _row(p["xattn_norm_g"]), 1, D_MODEL, dh2, dx3, "norm2_bwd")
    small["xattn_norm_g"] = g[0]

    dmixed = mm([(dx2, w["w_out"])], "nt", name="out_proj_bwd")
    big["w_out"] = jnp.concatenate(
        [mm([(r["y_ssd"], dx2)], "tn", out_dtypes=(BF16,), tm=512, name="out_proj_wgrad"),
         mm([(r["o"].reshape(t_rows, SSD_INNER), dx2)], "tn", out_dtypes=(BF16,), tm=512, name="out_proj_wgrad")],
        axis=0)
    dmixed3 = dmixed.reshape(bsz, seq, D_MODEL)

    joined, split = (None, None) if pending is None else join_riders([rider_xy, rider_diag])
    dq_fin, dk_fin, dv, rider_outs = attn_bwd_fused(
        r["q_fin"], r["k_fin"], r["kv3"], dmixed3, MLA_HEADS, r["o"], r["lse"], heads=MLA_HEADS, dk=MLA_QK_PAD,
        dv=MLA_V, v_mul=2, v_off=1, causal=True, scale=MLA_QK ** -0.5, tq=512, tk=512,
        name="mla_attn_bwd" if pending is None else "mla_attn_bwd_chip_exchange", rider=joined)
    if pending is not None:
        recv_xy, recv_diag = split(rider_outs)
    dq_raw, dkv, dkrope, g_q, g_k = qk_prep_bwd(r["q_raw3"], r["kv3"], r["proj"].reshape(bsz, seq, IN_PAD), r["gq"],
                                                r["gk"], tabs, dq_fin, dk_fin, dv, "qk_prep_bwd")
    dq_raw, dkv, dkrope = dq_raw.reshape(t_rows, 2048), dkv.reshape(t_rows, 2048), dkrope.reshape(t_rows, 128)
    small["mla_q_norm_g"] = g_q.reshape(MLA_HEADS, MLA_QK_PAD)[:, :MLA_QK].sum(0)
    small["mla_k_norm_g"] = g_k.reshape(MLA_HEADS, MLA_QK_PAD)[:, :MLA_QK].sum(0)
    big["w_q_b"] = _unpad_head_rows(mm([(dq_raw, r["qn"])], "tn", out_dtypes=(BF16,), tm=512,
                                       name="qkv_b_proj_wgrad"))
    dqn = mm([(dq_raw, w["w_q_b"])], "nn", name="qkv_b_proj_bwd")
    big["w_kv_b"] = mm([(dkv, r["kvn"])], "tn", out_dtypes=(BF16,), tm=512, name="qkv_b_proj_wgrad")
    dkvn = mm([(dkv, w["w_kv_b"])], "nn", name="qkv_b_proj_bwd")
    dq_a, g = rms_bwd(r["proj"], 5, Q_LORA, _row(p["q_a_norm_g"]), 1, Q_LORA, dqn, None, "qa_norm_bwd")
    small["q_a_norm_g"] = g[0]
    dkv_a, g = rms_bwd(r["proj"], 6, Q_LORA, _row(p["kv_a_norm_g"]), 1, Q_LORA, dkvn, None, "qa_norm_bwd")
    small["kv_a_norm_g"] = g[0]

    dyg, g = rms_bwd(r["yg"], 0, SSD_INNER, _row(p["ssd_norm_g"]), 2, 512, dmixed, None, "ssd_norm_bwd")
    small["ssd_norm_g"] = g[0]
    dy, dz = ew(_gate_bwd, [(dyg, 0), (r["y"], 0), (r["proj"], 0)], SSD_INNER, (F32, F32), "ssd_gate_bwd")
    dxbc_c, ddt_raw, g_dtb, g_alog, g_dsk = ssd_bwd(
        r["xbc"], r["dt_raw"], _row(p["dt_bias"]), _row(p["a_log"]), _row(p["d_skip"]), r["states"],
        dy.reshape(bsz, seq, SSD_INNER), "ssd_bwd")
    small["dt_bias"], small["a_log"], small["d_skip"] = g_dtb[0], g_alog[0], g_dsk[0]
    dxbc, dw8 = conv_bwd(r["proj"].reshape(bsz, seq, IN_PAD), r["w8"], dxbc_c, "conv_bwd")
    small["conv_w"] = dw8[:4]
    small["conv_b"] = dw8[4]

    dtail = jnp.concatenate([dkrope[:, :64], ddt_raw.reshape(t_rows, 16), jnp.zeros((t_rows, 48 + 128), F32)], axis=1)
    dproj = jnp.concatenate([dz.astype(BF16), dxbc.reshape(t_rows, CONV_DIM).astype(BF16), dq_a.astype(BF16),
                             dkv_a.astype(BF16), dtail.astype(BF16)], axis=1)
    big["w_in"] = _unpad_w_in_t(mm([(dproj, r["h1"])], "tn", out_dtypes=(BF16,), tm=768, tn=1024,
                                   name="in_proj_wgrad"))
    dh1 = mm([(dproj, w["w_in"])], "nn", name="in_proj_bwd")
    dx, g = rms_bwd(r["x"], 0, D_MODEL, _row(p["attn_norm_g"]), 1, D_MODEL, dh1, dx2, "norm1_bwd")
    small["attn_norm_g"] = g[0]
    if pending is None:
        recv_xy = recv_diag = None
    return dx, big, small, recv_xy, recv_diag


def _rope_tables(positions):
    inv_freq = 1.0 / (ROPE_THETA ** (jnp.arange(0, 64, 2, dtype=F32) / 64))
    ang = positions.astype(F32)[..., None] * inv_freq
    c, s = jnp.cos(ang), jnp.sin(ang)
    z32, z64 = jnp.zeros_like(c), jnp.zeros(c.shape[:-1] + (64,), F32)
    return (jnp.concatenate([c, c, z64], -1), jnp.concatenate([-s, z32, z64], -1),
            jnp.concatenate([z32, s, z64], -1))


def kernel(x, mem, positions, attn_norm_g, w_in, conv_w, conv_b, dt_bias, a_log, d_skip, ssd_norm_g, q_a_norm_g, w_q_b, kv_a_norm_g, w_kv_b, mla_q_norm_g, mla_k_norm_g, w_out, xattn_norm_g, mem_norm_g, w_xq, w_xk, w_xv, xq_norm_g, xk_norm_g, w_xo, ffn_norm_g, w_gate, w_up, w_down, loss_target, m_attn_norm_g, m_w_in, m_conv_w, m_conv_b, m_dt_bias, m_a_log, m_d_skip, m_ssd_norm_g, m_q_a_norm_g, m_w_q_b, m_kv_a_norm_g, m_w_kv_b, m_mla_q_norm_g, m_mla_k_norm_g, m_w_out, m_xattn_norm_g, m_mem_norm_g, m_w_xq, m_w_xk, m_w_xv, m_xq_norm_g, m_xk_norm_g, m_w_xo, m_ffn_norm_g, m_w_gate, m_w_up, m_w_down, v_attn_norm_g, v_w_in, v_conv_w, v_conv_b, v_dt_bias, v_a_log, v_d_skip, v_ssd_norm_g, v_q_a_norm_g, v_w_q_b, v_kv_a_norm_g, v_w_kv_b, v_mla_q_norm_g, v_mla_k_norm_g, v_w_out, v_xattn_norm_g, v_mem_norm_g, v_w_xq, v_w_xk, v_w_xv, v_xq_norm_g, v_xk_norm_g, v_w_xo, v_ffn_norm_g, v_w_gate, v_w_up, v_w_down):
    args = dict(locals())
    weights = {n: args[n] for n in WEIGHT_ORDER}
    mom_m = {n: args["m_" + n] for n in WEIGHT_ORDER}
    mom_v = {n: args["v_" + n] for n in WEIGHT_ORDER}
    bsz, seq, _ = x.shape
    t_rows = bsz * seq
    xf = x.reshape(t_rows, D_MODEL)
    memf = mem.reshape(-1, D_MODEL)
    tabs = _rope_tables(positions)

    full = []
    def local_blocks(l):
        return [_pack_local(fam, {n: weights[n][l] for n, _, _, _ in fam}, BF16) for fam, _ in FAMS]

    def full_weights(gathered):
        wl = {}
        for (fam, _), g in zip(FAMS, gathered):
            wl.update(_unpack_gathered(fam, g))
        wl["w_in"] = _pad_w_in_t(wl["w_in"])
        wl["w_q_b"] = _pad_head_rows(wl["w_q_b"])
        return wl

    full.append(full_weights(all_gather(local_blocks(0), "weight_all_gather")))
    small_p = [{n: weights[n][l] for n, _ in SMALL} for l in range(DEPTH)]
    cw = all_gather([conv_w.reshape(-1, 128)], "conv_w_all_gather")[0].reshape(N_DEV, DEPTH, 4, -1)
    cw = cw.transpose(1, 2, 0, 3).reshape(DEPTH, 4, CONV_DIM)
    for l in range(DEPTH):
        small_p[l]["conv_w_full"] = cw[l]

    saved = []
    h = xf
    for l in range(DEPTH):
        h, res, gathered = layer_fwd(h, memf, full[l], small_p[l], tabs, bsz,
                                     local_blocks(l + 1) if l + 1 < DEPTH else None)
        saved.append(res)
        if gathered is not None:
            full.append(full_weights(gathered))
    dy, loss_part = loss_head(h, loss_target.reshape(t_rows, D_MODEL), "loss_head")
    loss = lax.psum(loss_part[0, 0], ("x", "y", "c"))

    small_g = [None] * DEPTH
    outs = {k: {} for k in ("g", "d", "m", "v")}
    results = {}

    def update_layer(l, recv_xy, recv_diag):
        for a, (fam, _) in enumerate(FAMS):
            off = 0
            for n, rows, prows, transposed in fam:
                part = jnp.concatenate([recv_xy[a][:, off:off + rows], recv_diag[a][:, off:off + rows]], axis=0)
                off += prows
                if n in ADAMW_TRANSPOSED:
                    state = [src[n].transpose(0, 2, 1) for src in (weights, mom_m, mom_v)]
                else:
                    state = [weights[n], mom_m[n], mom_v[n]]
                    if transposed:
                        part = part.transpose(0, 2, 1)
                results[n] = adamw([part], *state, l, results.get(n), "adamw_" + n, tr=ADAMW_TR[n])

    dh, pending = dy, None
    for l in reversed(range(DEPTH)):
        dh, big_g, small_g[l], recv_xy, recv_diag = layer_bwd(dh, saved[l], memf, full[l], small_p[l], tabs, bsz,
                                                             pending)
        if pending is not None:
            update_layer(l + 1, recv_xy, recv_diag)
        pending = [_pack_full_grads(fam, big_g) for fam, _ in FAMS]
    from_sib = comm_call([sibling_stage_rider(pending)], "grad_sibling_exchange")[0]
    partials = [chip_partial(b, s, "grad_chip_partial", tr) for b, s, tr in zip(pending, from_sib, PARTIAL_TR)]
    whole = [(0, part.shape[1]) for part in partials]
    recv_xy, recv_diag = comm_call([chip_stage_rider(partials, False, whole), chip_stage_rider(partials, True, whole)],
                                   "grad_chip_exchange")
    update_layer(0, recv_xy, recv_diag)
    for ki, key in enumerate(("g", "d", "m", "v")):
        for n, res in results.items():
            outs[key][n] = res[ki].transpose(0, 2, 1) if n in ADAMW_TRANSPOSED else res[ki]

    sg = _pack_small({n: jnp.stack([small_g[l][n] for l in range(DEPTH)]) for n, _ in SMALL})
    cg = jnp.stack([small_g[l]["conv_w"] for l in range(DEPTH)]).reshape(CONVW_ROWS, 128)
    parts = all_gather([jnp.concatenate([sg, cg], axis=0)], "small_grad_all_gather")[0]
    sm = adamw([parts[:, :SMALL_ROWS]], _pack_small({n: weights[n] for n, _ in SMALL})[None],
               _pack_small({n: mom_m[n] for n, _ in SMALL})[None], _pack_small({n: mom_v[n] for n, _ in SMALL})[None],
               0, None, "adamw_small", tr=SMALL_ROWS)
    for ki, key in enumerate(("g", "d", "m", "v")):
        outs[key].update(_unpack_small(sm[ki][0]))
    my_idx = 4 * lax.axis_index("x") + 2 * lax.axis_index("y") + lax.axis_index("c")
    cparts = parts[:, SMALL_ROWS:].reshape(N_DEV, DEPTH, 4, N_DEV, -1)
    cparts = lax.dynamic_index_in_dim(cparts, my_idx, axis=3, keepdims=False).reshape(N_DEV, -1, 128)
    cres = adamw([cparts], conv_w.reshape(1, -1, 128), m_conv_w.reshape(1, -1, 128), v_conv_w.reshape(1, -1, 128),
                 0, None, "adamw_conv_w")
    for ki, key in enumerate(("g", "d", "m", "v")):
        outs[key]["conv_w"] = cres[ki].reshape(conv_w.shape)

    grad_x = dh.reshape(bsz, seq, D_MODEL)
    return (loss, grad_x, *[outs["g"][n] for n in WEIGHT_ORDER], *[outs["d"][n] for n in WEIGHT_ORDER],
            *[outs["m"][n] for n in WEIGHT_ORDER], *[outs["v"][n] for n in WEIGHT_ORDER])
```
